```python
import jax, jax.numpy as jnp
from jax import lax
import numpy as np

D_MODEL = 1024
BATCH = 2
SEQ = 8192
DEPTH = 4

GRID_W = 64
CTX_LEN = 256
N_MIXERS = 2
EPS = 1e-6
NEG_INF = -1e30

GLA_HEADS = 4
GLA_DK = D_MODEL // 2
GLA_DV = D_MODEL
GLA_HK = GLA_DK // GLA_HEADS
GLA_HV = GLA_DV // GLA_HEADS
GLA_GATE_RANK = 16
GLA_GATE_NORM = 16.0
GLA_CHUNK = 64
GLA_IN = 2 * GLA_DK + 2 * GLA_DV + 2 * GLA_GATE_RANK

SWA_HEADS = 16
SWA_KV_HEADS = 2
SWA_HD = 64
SWA_WINDOW = 128
SWA_BLOCK = 128
SWA_Q = SWA_HEADS * SWA_HD
SWA_KV = SWA_KV_HEADS * SWA_HD
SWA_QKV = SWA_Q + 2 * SWA_KV
ROPE_THETA = 10000.0

FFN_HIDDEN = ((8 * D_MODEL + 3 * 256 - 1) // (3 * 256)) * 256

kernel_name = "hybrid_gla_swa_sink_prefix_dit"

f32 = jnp.float32


def rmsnorm(x, w):
    xf = x.astype(f32)
    y = xf * lax.rsqrt(jnp.mean(xf * xf, axis=-1, keepdims=True) + EPS)
    return (y * w.astype(f32)).astype(x.dtype)


def swiglu(h, w_in, w_out):
    g, u = jnp.split(h @ w_in, 2, axis=-1)
    return (jax.nn.silu(g) * u) @ w_out


def axial_rope(rows, head_dim):
    row = jnp.broadcast_to(jnp.arange(rows)[:, None], (rows, GRID_W)).reshape(-1).astype(f32)
    col = jnp.broadcast_to(jnp.arange(GRID_W)[None, :], (rows, GRID_W)).reshape(-1).astype(f32)
    n_freq = head_dim // 4
    inv = ROPE_THETA ** (-jnp.arange(n_freq, dtype=f32) / n_freq)
    ang = jnp.concatenate([row[:, None] * inv, col[:, None] * inv], axis=-1)
    return jnp.cos(ang), jnp.sin(ang)


def apply_rope(x, cos, sin):
    half = x.shape[-1] // 2
    x1, x2 = x[..., :half], x[..., half:]
    return jnp.concatenate([x1 * cos - x2 * sin, x2 * cos + x1 * sin], axis=-1).astype(x.dtype)


def gla_chunk_scan(q, k, v, g, s0):
    B, H, T, _ = q.shape
    HV = v.shape[-1]
    n = T // GLA_CHUNK

    def to_chunks(a):
        return a.reshape(B, H, n, GLA_CHUNK, a.shape[-1])

    q, k, v, g = to_chunks(q), to_chunks(k), to_chunks(v), to_chunks(g)
    b = jnp.cumsum(g, axis=3)
    b_last = b[:, :, :, -1:, :]
    q_dec = q * jnp.exp(b)
    k_inv = k * jnp.exp(-b)
    k_end = k * jnp.exp(b_last - b)
    tri = jnp.tril(jnp.ones((GLA_CHUNK, GLA_CHUNK), dtype=bool))
    a = jnp.where(tri, jnp.einsum('bhnid,bhnjd->bhnij', q_dec, k_inv), 0.0)
    o_intra = jnp.einsum('bhnij,bhnjv->bhniv', a, v)

    def step(s, inp):
        q_n, k_n, v_n, dec_n = inp
        o_n = jnp.einsum('bhid,bhdv->bhiv', q_n, s)
        s = s * dec_n[..., None] + jnp.einsum('bhjd,bhjv->bhdv', k_n, v_n)
        return s, o_n

    xs = (jnp.moveaxis(q_dec, 2, 0), jnp.moveaxis(k_end, 2, 0), jnp.moveaxis(v, 2, 0),
          jnp.moveaxis(jnp.exp(b_last[:, :, :, 0, :]), 2, 0))
    s_fin, o_inter = lax.scan(step, s0, xs)
    o = o_intra + jnp.moveaxis(o_inter, 0, 2)
    return o.reshape(B, H, T, HV), s_fin


def gla_mixer(hx, hc, w_in, w_gate_up, b_gate_up, w_o, o_norm, need_ctx):
    B = hx.shape[0]
    idx = [GLA_DK, 2 * GLA_DK, 2 * GLA_DK + GLA_DV, 2 * GLA_DK + 2 * GLA_DV,
           2 * GLA_DK + 2 * GLA_DV + GLA_GATE_RANK]

    def heads(a, hd):
        T = a.shape[1]
        return a.reshape(B, T, GLA_HEADS, hd).transpose(0, 2, 1, 3).astype(f32)

    def project(h):
        q, k, v, r, gf, gb = jnp.split(h @ w_in, idx, axis=-1)
        lg_f = jax.nn.log_sigmoid((gf @ w_gate_up[0] + b_gate_up[0]).astype(f32)) / GLA_GATE_NORM
        lg_b = jax.nn.log_sigmoid((gb @ w_gate_up[1] + b_gate_up[1]).astype(f32)) / GLA_GATE_NORM
        return (heads(q, GLA_HK) * (GLA_HK ** -0.5), heads(k, GLA_HK), heads(v, GLA_HV),
                heads(lg_f, GLA_HK), heads(lg_b, GLA_HK), r)

    qx, kx, vx, gxf, gxb, rx = project(hx)
    qc, kc, vc, gcf, gcb, rc = project(hc)
    s0 = jnp.zeros((B, GLA_HEADS, GLA_HK, GLA_HV), f32)

    def flip(a):
        return jnp.flip(a, axis=2)

    ocf, s_cf = gla_chunk_scan(qc, kc, vc, gcf, s0)
    ocb_rev, s_cb = gla_chunk_scan(flip(qc), flip(kc), flip(vc), flip(gcb), s0)
    oxf, _ = gla_chunk_scan(qx, kx, vx, gxf, s_cf)
    oxb_rev, _ = gla_chunk_scan(flip(qx), flip(kx), flip(vx), flip(gxb), s_cb)

    def finish(o, r):
        o = rmsnorm(o, o_norm)
        T = o.shape[2]
        o = o.transpose(0, 2, 1, 3).reshape(B, T, GLA_DV).astype(r.dtype) * jax.nn.silu(r)
        return o @ w_o

    out_x = finish(oxf + flip(oxb_rev), rx)
    out_c = finish(ocf + flip(ocb_rev), rc) if need_ctx else None
    return out_x, out_c


def swa_mixer(hx, hc, w_qkv, b_qkv, w_o, sinks, cos, sin, need_ctx):
    B, S, _ = hx.shape
    G = SWA_HEADS // SWA_KV_HEADS
    nb = S // SWA_BLOCK
    scale = SWA_HD ** -0.5

    def project(h):
        T = h.shape[1]
        q, k, v = jnp.split(h @ w_qkv + b_qkv, [SWA_Q, SWA_Q + SWA_KV], axis=-1)
        return (q.reshape(B, T, SWA_KV_HEADS, G, SWA_HD), k.reshape(B, T, SWA_KV_HEADS, SWA_HD),
                v.reshape(B, T, SWA_KV_HEADS, SWA_HD))

    qx, kx, vx = project(hx)
    qc, kc, vc = project(hc)
    qx = apply_rope(qx, cos[:, None, None, :], sin[:, None, None, :])
    kx = apply_rope(kx, cos[:, None, :], sin[:, None, :])

    qb = qx.reshape(B, nb, SWA_BLOCK, SWA_KV_HEADS, G, SWA_HD)

    def band(a):
        ap = jnp.pad(a, ((0, 0), (SWA_BLOCK, SWA_BLOCK), (0, 0), (0, 0)))
        ap = ap.reshape(B, nb + 2, SWA_BLOCK, SWA_KV_HEADS, SWA_HD)
        return jnp.concatenate([ap[:, :-2], ap[:, 1:-1], ap[:, 2:]], axis=2)

    kb, vb = band(kx), band(vx)
    qi = jnp.arange(nb)[:, None, None] * SWA_BLOCK + jnp.arange(SWA_BLOCK)[None, :, None]
    kj = jnp.arange(nb)[:, None, None] * SWA_BLOCK - SWA_BLOCK + jnp.arange(3 * SWA_BLOCK)[None, None, :]
    mask = (jnp.abs(kj - qi) <= SWA_WINDOW) & (kj >= 0) & (kj < S)

    s_win = jnp.einsum('bnqhgd,bnkhd->bhgnqk', qb, kb, preferred_element_type=f32) * scale
    s_win = jnp.where(mask, s_win, NEG_INF)
    s_ctx = jnp.einsum('bnqhgd,bkhd->bhgnqk', qb, kc, preferred_element_type=f32) * scale
    sink = sinks.astype(f32).reshape(SWA_KV_HEADS, G)[None, :, :, None, None, None]
    m = jnp.maximum(jnp.maximum(s_win.max(-1, keepdims=True), s_ctx.max(-1, keepdims=True)), sink)
    p_win = jnp.exp(s_win - m)
    p_ctx = jnp.exp(s_ctx - m)
    den = p_win.sum(-1, keepdims=True) + p_ctx.sum(-1, keepdims=True) + jnp.exp(sink - m)
    o = (jnp.einsum('bhgnqk,bnkhd->bnqhgd', p_win, vb.astype(f32))
         + jnp.einsum('bhgnqk,bkhd->bnqhgd', p_ctx, vc.astype(f32)))
    o = o / jnp.transpose(den, (0, 3, 4, 1, 2, 5))
    out_x = o.reshape(B, S, SWA_Q).astype(hx.dtype) @ w_o

    out_c = None
    if need_ctx:
        L = hc.shape[1]
        s_c = jnp.einsum('blhgd,bkhd->bhglk', qc, kc, preferred_element_type=f32) * scale
        sink_c = sinks.astype(f32).reshape(SWA_KV_HEADS, G)[None, :, :, None, None]
        m_c = jnp.maximum(s_c.max(-1, keepdims=True), sink_c)
        p_c = jnp.exp(s_c - m_c)
        den_c = p_c.sum(-1, keepdims=True) + jnp.exp(sink_c - m_c)
        o_c = jnp.einsum('bhglk,bkhd->blhgd', p_c, vc.astype(f32)) / jnp.transpose(den_c, (0, 3, 1, 2, 4))
        out_c = o_c.reshape(B, L, SWA_Q).astype(hc.dtype) @ w_o
    return out_x, out_c


def setup_inputs(seed: int = 0) -> dict:
    key = jax.random.key(seed)
    ks = jax.random.split(key, 20)
    n_gla = (DEPTH + 1) // 2
    n_swa = DEPTH // 2
    D = D_MODEL

    def nrm(k, shape, s):
        return jax.random.normal(k, shape, f32) * s

    return {
        "x": nrm(ks[0], (BATCH, SEQ, D), 1.0),
        "c": nrm(ks[1], (BATCH, D), 1.0),
        "ctx": nrm(ks[2], (BATCH, CTX_LEN, D), 1.0),
        "c_ctx": nrm(ks[3], (D,), 1.0),
        "w_ada": nrm(ks[4], (DEPTH, D, 6 * D), 0.5 * D ** -0.5),
        "b_ada": nrm(ks[5], (DEPTH, 6 * D), 0.01),
        "norm_mix": 1.0 + nrm(ks[6], (DEPTH, D), 0.02),
        "norm_ffn": 1.0 + nrm(ks[7], (DEPTH, D), 0.02),
        "gla_w_in": nrm(ks[8], (n_gla, D, GLA_IN), D ** -0.5),
        "gla_w_gate_up": nrm(ks[9], (n_gla, 2, GLA_GATE_RANK, GLA_DK), GLA_GATE_RANK ** -0.5),
        "gla_b_gate_up": nrm(ks[10], (n_gla, 2, GLA_DK), 0.1),
        "gla_w_o": nrm(ks[11], (n_gla, GLA_DV, D), GLA_DV ** -0.5),
        "gla_o_norm": 1.0 + nrm(ks[12], (n_gla, GLA_HV), 0.02),
        "swa_w_qkv": nrm(ks[13], (n_swa, D, SWA_QKV), D ** -0.5),
        "swa_b_qkv": nrm(ks[14], (n_swa, SWA_QKV), 0.01),
        "swa_w_o": nrm(ks[15], (n_swa, SWA_Q, D), SWA_Q ** -0.5),
        "swa_sinks": nrm(ks[16], (n_swa, SWA_HEADS), 0.5),
        "ffn_w_in": nrm(ks[17], (DEPTH, D, 2 * FFN_HIDDEN), D ** -0.5),
        "ffn_w_out": nrm(ks[18], (DEPTH, FFN_HIDDEN, D), FFN_HIDDEN ** -0.5),
        "norm_final": 1.0 + nrm(ks[19], (D,), 0.02),
    }


def reference(x, c, ctx, c_ctx, w_ada, b_ada, norm_mix, norm_ffn, gla_w_in, gla_w_gate_up,
              gla_b_gate_up, gla_w_o, gla_o_norm, swa_w_qkv, swa_b_qkv, swa_w_o, swa_sinks,
              ffn_w_in, ffn_w_out, norm_final):
    S = x.shape[1]
    rows = S // GRID_W
    cos, sin = axial_rope(rows, SWA_HD)

    mod_x = jnp.einsum('bd,lde->lbe', jax.nn.silu(c), w_ada) + b_ada[:, None, :]
    mod_c = jnp.einsum('d,lde->le', jax.nn.silu(c_ctx), w_ada) + b_ada

    for i in range(DEPTH):
        last = i == DEPTH - 1
        sh1, sc1, g1, sh2, sc2, g2 = jnp.split(mod_x[i][:, None, :], 6, axis=-1)
        csh1, csc1, cg1, csh2, csc2, cg2 = jnp.split(mod_c[i][None, None, :], 6, axis=-1)

        hx = rmsnorm(x, norm_mix[i]) * (1.0 + sc1) + sh1
        hc = rmsnorm(ctx, norm_mix[i]) * (1.0 + csc1) + csh1
        j = i // N_MIXERS
        if i % N_MIXERS == 0:
            ox, oc = gla_mixer(hx, hc, gla_w_in[j], gla_w_gate_up[j], gla_b_gate_up[j],
                               gla_w_o[j], gla_o_norm[j], not last)
        else:
            ox, oc = swa_mixer(hx, hc, swa_w_qkv[j], swa_b_qkv[j], swa_w_o[j], swa_sinks[j],
                               cos, sin, not last)
        x = x + g1 * ox
        x = x + g2 * swiglu(rmsnorm(x, norm_ffn[i]) * (1.0 + sc2) + sh2, ffn_w_in[i], ffn_w_out[i])
        if not last:
            ctx = ctx + cg1 * oc
            ctx = ctx + cg2 * swiglu(rmsnorm(ctx, norm_ffn[i]) * (1.0 + csc2) + csh2,
                                     ffn_w_in[i], ffn_w_out[i])

    return rmsnorm(x, norm_final)
```

```python
import functools

import jax
import jax.numpy as jnp
from jax import lax
from jax.experimental import pallas as pl
from jax.experimental.pallas import tpu as pltpu

f32 = jnp.float32
bf16 = jnp.bfloat16

EPS = 1e-6
NEG_INF = -1e30
GRID_W = 64
ROPE_THETA = 10000.0

GLA_HEADS = 4
GLA_GATE_RANK = 16
GLA_GATE_NORM = 16.0
GLA_CHUNK = 64
GLA_BLOCK = 256

SWA_HEADS = 16
SWA_KV_HEADS = 2
SWA_HD = 64
SWA_WINDOW = 128
SWA_BLOCK = 128

LANES = 128
FFN_CHUNK = 256
ROW_TILE = 512
VMEM_LIMIT = 56 * 1024 * 1024


def _nt(a, b):
    return lax.dot_general(a, b, (((1,), (1,)), ((), ())), preferred_element_type=f32)


def _nn(a, b):
    return jnp.dot(a, b, preferred_element_type=f32)


def _silu(v):
    return v * (1.0 / (1.0 + jnp.exp(-v)))


def _rms(v, w):
    return v * lax.rsqrt(jnp.mean(v * v, axis=-1, keepdims=True) + EPS) * w


def _const_spec(shape):
    zeros = (0,) * len(shape)
    return pl.BlockSpec(shape, lambda *_: zeros, pipeline_mode=pl.Buffered(1))


def _ada_kernel(cb_ref, w_ref, b_ref, out_ref, *, n_rows, d_model, tn):
    ncol = tn // LANES

    def body(i, accs):
        d0 = pl.multiple_of(i * 8, 8)
        w8 = w_ref[0, pl.ds(d0, 8), :]
        new = []
        for r in range(n_rows):
            s8 = _silu(cb_ref[r, pl.ds(d0, 8), :])
            new.append(tuple(accs[r][j] + w8[:, j * LANES:(j + 1) * LANES] * s8 for j in range(ncol)))
        return tuple(new)

    init = tuple(tuple(jnp.zeros((8, LANES), f32) for _ in range(ncol)) for _ in range(n_rows))
    accs = lax.fori_loop(0, d_model // 8, body, init)
    rows = [jnp.concatenate([jnp.sum(a, axis=0, keepdims=True) for a in accs[r]], axis=1)
            for r in range(n_rows)]
    rows += [jnp.zeros((1, tn), f32)] * (8 - n_rows)
    out_ref[0] = jnp.concatenate(rows, axis=0) + b_ref[0]


def _ada_mod(cvecs, w_ada, b_ada):
    n_rows, d_model = cvecs.shape
    depth, _, e = w_ada.shape
    tn = 768
    cb = jnp.broadcast_to(cvecs[:, :, None], (n_rows, d_model, LANES))
    return pl.pallas_call(
        functools.partial(_ada_kernel, n_rows=n_rows, d_model=d_model, tn=tn),
        grid=(depth, e // tn),
        in_specs=[
            pl.BlockSpec((n_rows, d_model, LANES), lambda l, j: (0, 0, 0)),
            pl.BlockSpec((1, d_model, tn), lambda l, j: (l, 0, j)),
            pl.BlockSpec((1, 1, tn), lambda l, j: (l, 0, j)),
        ],
        out_specs=pl.BlockSpec((1, 8, tn), lambda l, j: (l, 0, j)),
        out_shape=jax.ShapeDtypeStruct((depth, 8, e), f32),
        compiler_params=pltpu.CompilerParams(dimension_semantics=("arbitrary", "arbitrary")),
        name="ada_mod",
    )(cb, w_ada, b_ada.reshape(depth, 1, e))


def _mod_slices(mod_ref, d, first):
    return tuple(mod_ref[0, :, (first + k) * d:(first + k + 1) * d] for k in range(3))


def _group_map(n_lat_tiles, tiles_per_batch, n_batch):
    def index_map(i):
        return (jnp.where(i < n_lat_tiles, i // tiles_per_batch, n_batch), 0, 0)
    return index_map


def _gla_proj_kernel(x_ref, mod_ref, nw_ref, wqkr_ref, wvt_ref, wg_ref, wup_ref, bup_ref,
                     qk_ref, r_ref, vt_ref, g_ref, *, d, dk, dv, heads):
    sh, sc, _ = _mod_slices(mod_ref, d, 0)
    h = (_rms(x_ref[...], nw_ref[...]) * (1.0 + sc) + sh).astype(bf16)
    hk = dk // heads
    qkr = _nn(h, wqkr_ref[...])
    q = qkr[:, :dk] * (hk ** -0.5)
    for hh in range(heads):
        qk_ref[hh] = q[:, hh * hk:(hh + 1) * hk].astype(bf16)
        qk_ref[heads + hh] = qkr[:, dk + hh * hk:dk + (hh + 1) * hk].astype(bf16)
    r_ref[...] = qkr[:, 2 * dk:].astype(bf16)
    vt_ref[...] = _nt(wvt_ref[...], h).astype(bf16)
    low = _nn(h, wg_ref[...]).astype(bf16)
    z = _nn(low, wup_ref[...]) + bup_ref[...]
    lg = (jnp.minimum(z, 0.0) - jnp.log(1.0 + jnp.exp(-jnp.abs(z)))) * (1.0 / GLA_GATE_NORM)
    for hh in range(2 * heads):
        g_ref[hh] = lg[:, hh * hk:(hh + 1) * hk]


def _gla_proj(x_all, mod, norm_w, w_in, w_gate_up, b_gate_up, geo):
    t_all, d = x_all.shape
    dk, dv, heads = d // 2, d, GLA_HEADS
    hk = dk // heads
    rank = GLA_GATE_RANK
    tm = ROW_TILE
    wqkr = jnp.concatenate([w_in[:, :2 * dk], w_in[:, 2 * dk + dv:2 * dk + 2 * dv]], axis=1).astype(bf16)
    wvt = w_in[:, 2 * dk:2 * dk + dv].T.astype(bf16)
    wg = jnp.zeros((d, LANES), f32).at[:, :2 * rank].set(w_in[:, 2 * dk + 2 * dv:]).astype(bf16)
    wup = jnp.zeros((LANES, 2 * dk), f32)
    wup = wup.at[:rank, :dk].set(w_gate_up[0]).at[rank:2 * rank, dk:].set(w_gate_up[1]).astype(bf16)
    bup = b_gate_up.reshape(1, 2 * dk)
    kern = functools.partial(_gla_proj_kernel, d=d, dk=dk, dv=dv, heads=heads)
    return pl.pallas_call(
        kern,
        grid=(t_all // tm,),
        in_specs=[
            pl.BlockSpec((tm, d), lambda i: (i, 0)),
            pl.BlockSpec((1, 1, 6 * d), _group_map(*geo)),
            _const_spec((1, d)),
            _const_spec((d, 2 * dk + dv)),
            _const_spec((dv, d)),
            _const_spec((d, LANES)),
            _const_spec((LANES, 2 * dk)),
            _const_spec((1, 2 * dk)),
        ],
        out_specs=[
            pl.BlockSpec((2 * heads, tm, hk), lambda i: (0, i, 0)),
            pl.BlockSpec((tm, dv), lambda i: (i, 0)),
            pl.BlockSpec((dv, tm), lambda i: (0, i)),
            pl.BlockSpec((2 * heads, tm, hk), lambda i: (0, i, 0)),
        ],
        out_shape=[
            jax.ShapeDtypeStruct((2 * heads, t_all, hk), bf16),
            jax.ShapeDtypeStruct((t_all, dv), bf16),
            jax.ShapeDtypeStruct((dv, t_all), bf16),
            jax.ShapeDtypeStruct((2 * heads, t_all, hk), f32),
        ],
        compiler_params=pltpu.CompilerParams(dimension_semantics=("arbitrary",),
                                             vmem_limit_bytes=VMEM_LIMIT),
        name="gla_proj",
    )(x_all, mod, norm_w.reshape(1, d), wqkr, wvt, wg, wup, bup)


def _gla_block(q, k, vt, g, s_ref, o_ref, reverse):
    tb = q.shape[0]
    c = GLA_CHUNK
    nchunk = tb // c
    row = lax.broadcasted_iota(jnp.int32, (tb, tb), 0)
    col = lax.broadcasted_iota(jnp.int32, (tb, tb), 1)
    same = (row // c) == (col // c)
    causal = same & ((col >= row) if reverse else (col <= row))
    cum = jnp.where(causal, 1.0, 0.0).astype(bf16)
    g_hi = g.astype(bf16)
    g_lo = (g - g_hi.astype(f32)).astype(bf16)
    b = _nn(cum, g_hi) + _nn(cum, g_lo)
    pieces = []
    for ci in range(nchunk):
        last = ci * c if reverse else ci * c + c - 1
        pieces.append(jnp.broadcast_to(b[last:last + 1, :], (c, b.shape[1])))
    b_tot = jnp.concatenate(pieces, axis=0)
    kf = k.astype(f32)
    qd = (q.astype(f32) * jnp.exp(b)).astype(bf16)
    ki = (kf * jnp.exp(-b)).astype(bf16)
    ke = kf * jnp.exp(b_tot - b)
    scores = jnp.where(causal, _nt(qd, ki), 0.0).astype(bf16)
    o_intra = _nt(scores, vt)
    rchunk = lax.broadcasted_iota(jnp.int32, ke.shape, 0) // c
    order = range(nchunk - 1, -1, -1) if reverse else range(nchunk)
    for ci in order:
        lo = ci * c
        s = s_ref[...]
        o_ref[lo:lo + c, :] = o_intra[lo:lo + c, :] + _nt(qd[lo:lo + c, :], s.astype(bf16))
        dec = jnp.exp(b_tot[lo:lo + 1, :])
        ke_c = jnp.where(rchunk == ci, ke, 0.0).astype(bf16)
        s_ref[...] = s * dec + _nn(vt, ke_c)


def _gla_scan_kernel(qf_ref, kf_ref, vtf_ref, gf_ref, qb_ref, kb_ref, vtb_ref, gb_ref,
                     of_ref, ob_ref, sf_ref, sb_ref):
    @pl.when(pl.program_id(2) == 0)
    def _():
        sf_ref[...] = jnp.zeros_like(sf_ref)
        sb_ref[...] = jnp.zeros_like(sb_ref)

    _gla_block(qf_ref[0], kf_ref[0], vtf_ref[...], gf_ref[0], sf_ref, of_ref, False)
    _gla_block(qb_ref[0], kb_ref[0], vtb_ref[...], gb_ref[0], sb_ref, ob_ref, True)


def _gla_scan(qk, vt, gates, n_batch, seq, ctx_len):
    heads = GLA_HEADS
    _, t_all, hk = qk.shape
    dv = vt.shape[0]
    hv = dv // heads
    tb = GLA_BLOCK
    assert ctx_len == tb and seq % tb == 0
    nlat = seq // tb
    ctx0 = n_batch * nlat

    def fwd_row(b, t):
        return jnp.where(t == 0, ctx0 + b, b * nlat + t - 1)

    def bwd_row(b, t):
        return jnp.where(t == 0, ctx0 + b, b * nlat + nlat - t)

    def specs(row_fn, gate_off):
        return [
            pl.BlockSpec((1, tb, hk), lambda b, h, t: (h, row_fn(b, t), 0)),
            pl.BlockSpec((1, tb, hk), lambda b, h, t: (heads + h, row_fn(b, t), 0)),
            pl.BlockSpec((hv, tb), lambda b, h, t: (h, row_fn(b, t))),
            pl.BlockSpec((1, tb, hk), lambda b, h, t: (gate_off + h, row_fn(b, t), 0)),
        ]

    return pl.pallas_call(
        _gla_scan_kernel,
        grid=(n_batch, heads, nlat + 1),
        in_specs=specs(fwd_row, 0) + specs(bwd_row, heads),
        out_specs=[
            pl.BlockSpec((tb, hv), lambda b, h, t: (fwd_row(b, t), h)),
            pl.BlockSpec((tb, hv), lambda b, h, t: (bwd_row(b, t), h)),
        ],
        out_shape=[jax.ShapeDtypeStruct((t_all, dv), f32)] * 2,
        scratch_shapes=[pltpu.VMEM((hv, hk), f32)] * 2,
        compiler_params=pltpu.CompilerParams(
            dimension_semantics=("arbitrary", "arbitrary", "arbitrary")),
        name="gla_scan",
    )(qk, qk, vt, gates, qk, qk, vt, gates)


def _swa_proj_kernel(x_ref, mod_ref, nw_ref, w_ref, b_ref, cos_ref, sin_ref,
                     q_ref, kx_ref, vx_ref, *, d, nq):
    sh, sc, _ = _mod_slices(mod_ref, d, 0)
    h = (_rms(x_ref[...], nw_ref[...]) * (1.0 + sc) + sh).astype(bf16)
    qkv = _nn(h, w_ref[...]) + b_ref[...]
    cos = cos_ref[...]
    sin = sin_ref[...]
    lane = lax.broadcasted_iota(jnp.int32, cos.shape, 1)
    half = SWA_HD // 2
    first_half = (lane % SWA_HD) < half
    low_head = lane < SWA_HD

    def rope(v):
        partner = jnp.where(first_half, pltpu.roll(v, LANES - half, 1), pltpu.roll(v, half, 1))
        return v * cos + partner * sin

    for j in range(nq // LANES):
        q_ref[:, j * LANES:(j + 1) * LANES] = (
            rope(qkv[:, j * LANES:(j + 1) * LANES]) * (SWA_HD ** -0.5)).astype(bf16)
    k = rope(qkv[:, nq:nq + LANES])
    v = qkv[:, nq + LANES:nq + 2 * LANES]
    for src, dst in ((k, kx_ref), (v, vx_ref)):
        swapped = pltpu.roll(src, SWA_HD, 1)
        dst[0] = jnp.where(low_head, src, 0.0).astype(bf16)
        dst[1] = jnp.where(low_head, 0.0, swapped).astype(bf16)
        dst[2] = jnp.where(low_head, swapped, 0.0).astype(bf16)
        dst[3] = jnp.where(low_head, 0.0, src).astype(bf16)


def _rope_tables(seq, extra_rows):
    rows = seq // GRID_W
    row = jnp.broadcast_to(jnp.arange(rows)[:, None], (rows, GRID_W)).reshape(-1).astype(f32)
    col = jnp.broadcast_to(jnp.arange(GRID_W)[None, :], (rows, GRID_W)).reshape(-1).astype(f32)
    n_freq = SWA_HD // 4
    inv = ROPE_THETA ** (-jnp.arange(n_freq, dtype=f32) / n_freq)
    ang = jnp.concatenate([row[:, None] * inv, col[:, None] * inv], axis=-1)
    cos, sin = jnp.cos(ang), jnp.sin(ang)
    reps = LANES // SWA_HD
    cos_l = jnp.tile(jnp.concatenate([cos, cos], axis=-1), (1, reps))
    sin_l = jnp.tile(jnp.concatenate([-sin, sin], axis=-1), (1, reps))
    cos_l = jnp.concatenate([cos_l, jnp.ones((extra_rows, LANES), f32)], axis=0)
    sin_l = jnp.concatenate([sin_l, jnp.zeros((extra_rows, LANES), f32)], axis=0)
    return cos_l, sin_l


def _swa_proj(x_all, mod, norm_w, w_qkv, b_qkv, cos_l, sin_l, geo):
    t_all, d = x_all.shape
    nq = SWA_HEADS * SWA_HD
    nqkv = w_qkv.shape[1]
    tm = ROW_TILE
    n_lat_tiles, tiles_per_batch, _ = geo

    def pos_map(i):
        return (jnp.where(i < n_lat_tiles, i % tiles_per_batch, tiles_per_batch), 0)

    return pl.pallas_call(
        functools.partial(_swa_proj_kernel, d=d, nq=nq),
        grid=(t_all // tm,),
        in_specs=[
            pl.BlockSpec((tm, d), lambda i: (i, 0)),
            pl.BlockSpec((1, 1, 6 * d), _group_map(*geo)),
            _const_spec((1, d)),
            _const_spec((d, nqkv)),
            _const_spec((1, nqkv)),
            pl.BlockSpec((tm, LANES), pos_map),
            pl.BlockSpec((tm, LANES), pos_map),
        ],
        out_specs=[
            pl.BlockSpec((tm, nq), lambda i: (i, 0)),
            pl.BlockSpec((4, tm, LANES), lambda i: (0, i, 0)),
            pl.BlockSpec((4, tm, LANES), lambda i: (0, i, 0)),
        ],
        out_shape=[
            jax.ShapeDtypeStruct((t_all, nq), bf16),
            jax.ShapeDtypeStruct((4, t_all, LANES), bf16),
            jax.ShapeDtypeStruct((4, t_all, LANES), bf16),
        ],
        compiler_params=pltpu.CompilerParams(dimension_semantics=("arbitrary",),
                                             vmem_limit_bytes=VMEM_LIMIT),
        name="swa_proj",
    )(x_all, mod, norm_w.reshape(1, d), w_qkv.astype(bf16), b_qkv.reshape(1, nqkv), cos_l, sin_l)


def _attend(q_ref, o_ref, sink_ref, k_pieces, v_pieces, mask, n_masked):
    tq = q_ref.shape[0]
    pairs = q_ref.shape[1] // LANES
    group = pairs // SWA_KV_HEADS
    lane = lax.broadcasted_iota(jnp.int32, (tq, LANES), 1)
    low_head = lane < SWA_HD
    for g in range(SWA_KV_HEADS):
        k_cat = jnp.concatenate(k_pieces[g][0] + k_pieces[g][1], axis=0)
        nk = k_cat.shape[0] // 2
        lane2 = lax.broadcasted_iota(jnp.int32, (nk, LANES), 1)
        ones_lo = jnp.where(lane2 < SWA_HD, 1.0, 0.0).astype(bf16)
        ones_hi = jnp.where(lane2 < SWA_HD, 0.0, 1.0).astype(bf16)
        v_cat = jnp.concatenate([
            jnp.concatenate([jnp.concatenate(v_pieces[g][0], axis=0), ones_lo], axis=1),
            jnp.concatenate([jnp.concatenate(v_pieces[g][1], axis=0), ones_hi], axis=1)], axis=0)
        for jj in range(group):
            j = g * group + jj
            s = _nt(q_ref[:, j * LANES:(j + 1) * LANES], k_cat)
            probs, sink_terms = [], []
            for p in range(2):
                sp = s[:, p * nk:(p + 1) * nk]
                if mask is not None:
                    sp = jnp.concatenate(
                        [jnp.where(mask, sp[:, :n_masked], NEG_INF), sp[:, n_masked:]], axis=1)
                sink = sink_ref[2 * j + p]
                m = jnp.maximum(jnp.max(sp, axis=-1, keepdims=True), sink)
                probs.append(jnp.exp(sp - m).astype(bf16))
                sink_terms.append(jnp.exp(sink - m))
            out = _nn(jnp.concatenate(probs, axis=1), v_cat)
            den = out[:, LANES:] + jnp.where(low_head, sink_terms[0], sink_terms[1])
            o_ref[:, j * LANES:(j + 1) * LANES] = (out[:, :LANES] / den).astype(bf16)


def _swa_lat_kernel(sink_ref, q_ref, kp_ref, ko_ref, kn_ref, kc_ref, vp_ref, vo_ref, vn_ref, vc_ref,
                    o_ref, *, seq):
    i = pl.program_id(1)
    blk = SWA_BLOCK
    qi = i * blk + lax.broadcasted_iota(jnp.int32, (blk, 3 * blk), 0)
    kj = (i - 1) * blk + lax.broadcasted_iota(jnp.int32, (blk, 3 * blk), 1)
    mask = (jnp.abs(kj - qi) <= SWA_WINDOW) & (kj >= 0) & (kj < seq)
    k_pieces = [[[r[2 * g + p] for r in (kp_ref, ko_ref, kn_ref, kc_ref)] for p in range(2)]
                for g in range(SWA_KV_HEADS)]
    v_pieces = [[[r[2 * g + p] for r in (vp_ref, vo_ref, vn_ref, vc_ref)] for p in range(2)]
                for g in range(SWA_KV_HEADS)]
    _attend(q_ref, o_ref, sink_ref, k_pieces, v_pieces, mask, 3 * blk)


def _swa_ctx_kernel(sink_ref, q_ref, kc_ref, vc_ref, o_ref):
    k_pieces = [[[kc_ref[2 * g + p]] for p in range(2)] for g in range(SWA_KV_HEADS)]
    v_pieces = [[[vc_ref[2 * g + p]] for p in range(2)] for g in range(SWA_KV_HEADS)]
    _attend(q_ref, o_ref, sink_ref, k_pieces, v_pieces, None, 0)


def _swa_attention(q, kx, vx, sinks, n_batch, seq, ctx_len, need_ctx):
    t_all, nq = q.shape
    blk = SWA_BLOCK
    nb = seq // blk
    ncb = ctx_len // blk
    lat_blocks = n_batch * nb
    smem = pl.BlockSpec(memory_space=pltpu.SMEM)

    def win(off):
        return pl.BlockSpec((4, blk, LANES),
                            lambda b, i: (0, b * nb + jnp.clip(i + off, 0, nb - 1), 0))

    def ctx_spec():
        return pl.BlockSpec((4, ctx_len, LANES), lambda b, i: (0, (n_batch * seq) // ctx_len + b, 0))

    lat = pl.pallas_call(
        functools.partial(_swa_lat_kernel, seq=seq),
        grid=(n_batch, nb),
        in_specs=[smem, pl.BlockSpec((blk, nq), lambda b, i: (b * nb + i, 0)),
                  win(-1), win(0), win(1), ctx_spec(), win(-1), win(0), win(1), ctx_spec()],
        out_specs=pl.BlockSpec((blk, nq), lambda b, i: (b * nb + i, 0)),
        out_shape=jax.ShapeDtypeStruct((n_batch * seq, nq), bf16),
        compiler_params=pltpu.CompilerParams(dimension_semantics=("arbitrary", "arbitrary"),
                                             vmem_limit_bytes=VMEM_LIMIT),
        name="swa_attn",
    )(sinks, q, kx, kx, kx, kx, vx, vx, vx, vx)
    if not need_ctx:
        return lat
    ctx = pl.pallas_call(
        _swa_ctx_kernel,
        grid=(n_batch, ncb),
        in_specs=[smem, pl.BlockSpec((blk, nq), lambda b, i: (lat_blocks + b * ncb + i, 0)),
                  ctx_spec(), ctx_spec()],
        out_specs=pl.BlockSpec((blk, nq), lambda b, i: (b * ncb + i, 0)),
        out_shape=jax.ShapeDtypeStruct((n_batch * ctx_len, nq), bf16),
        compiler_params=pltpu.CompilerParams(dimension_semantics=("arbitrary", "arbitrary")),
        name="swa_attn_ctx",
    )(sinks, q, kx, vx)
    return jnp.concatenate([lat, ctx], axis=0)


def _post_kernel(*refs, d, gla, final, n_chunks):
    if gla:
        x_ref, of_ref, ob_ref, r_ref, on_ref = refs[:5]
        rest = refs[5:]
    else:
        x_ref, a_ref = refs[:2]
        rest = refs[2:]
    mod_ref, wo_ref, nw_ref, wgu_ref, wout_ref = rest[:5]
    rest = rest[5:]
    if final:
        nf_ref, out_ref, acc_ref = rest
    else:
        out_ref, acc_ref = rest

    if gla:
        o = of_ref[...] + ob_ref[...]
        hv = d // GLA_HEADS
        parts = []
        for hh in range(GLA_HEADS):
            seg = o[:, hh * hv:(hh + 1) * hv]
            parts.append(_rms(seg, on_ref[:, hh * hv:(hh + 1) * hv]))
        a = (jnp.concatenate(parts, axis=1) * _silu(r_ref[...].astype(f32))).astype(bf16)
    else:
        a = a_ref[...]
    _, _, g1 = _mod_slices(mod_ref, d, 0)
    sh2, sc2, g2 = _mod_slices(mod_ref, d, 3)
    x1 = x_ref[...] + g1 * _nn(a, wo_ref[...])
    hn = (_rms(x1, nw_ref[...]) * (1.0 + sc2) + sh2).astype(bf16)
    acc_ref[...] = jnp.zeros_like(acc_ref)

    def body(ci, carry):
        gu = _nn(hn, wgu_ref[ci])
        act = (_silu(gu[:, :FFN_CHUNK]) * gu[:, FFN_CHUNK:]).astype(bf16)
        acc_ref[...] += _nn(act, wout_ref[ci])
        return carry

    lax.fori_loop(0, n_chunks, body, 0)
    x2 = x1 + g2 * acc_ref[...]
    if final:
        x2 = _rms(x2, nf_ref[...])
    out_ref[...] = x2


def _post(x_all, mixer_in, mod, w_o, norm_w, w_in, w_out, geo, gla, norm_final, n_tiles, o_norm=None):
    t_all, d = x_all.shape
    tm = ROW_TILE
    hidden = w_out.shape[0]
    n_chunks = hidden // FFN_CHUNK
    wgu = jnp.concatenate([w_in[:, :hidden].reshape(d, n_chunks, FFN_CHUNK),
                           w_in[:, hidden:].reshape(d, n_chunks, FFN_CHUNK)], axis=2)
    wgu = wgu.transpose(1, 0, 2).astype(bf16)
    wout = w_out.reshape(n_chunks, FFN_CHUNK, d).astype(bf16)
    final = norm_final is not None
    row = lambda i: (i, 0)
    args = [x_all]
    in_specs = [pl.BlockSpec((tm, d), row)]
    if gla:
        o_f, o_b, r = mixer_in
        args += [o_f, o_b, r, jnp.tile(o_norm, GLA_HEADS).reshape(1, d)]
        in_specs += [pl.BlockSpec((tm, d), row)] * 3 + [_const_spec((1, d))]
    else:
        args += [mixer_in]
        in_specs += [pl.BlockSpec((tm, d), row)]
    args += [mod, w_o.astype(bf16), norm_w.reshape(1, d), wgu, wout]
    in_specs += [pl.BlockSpec((1, 1, 6 * d), _group_map(*geo)), _const_spec((d, d)), _const_spec((1, d)),
                 _const_spec((n_chunks, d, 2 * FFN_CHUNK)), _const_spec((n_chunks, FFN_CHUNK, d))]
    if final:
        args += [norm_final.reshape(1, d)]
        in_specs += [_const_spec((1, d))]
    return pl.pallas_call(
        functools.partial(_post_kernel, d=d, gla=gla, final=final, n_chunks=n_chunks),
        grid=(n_tiles,),
        in_specs=in_specs,
        out_specs=pl.BlockSpec((tm, d), row),
        out_shape=jax.ShapeDtypeStruct((n_tiles * tm, d), f32),
        scratch_shapes=[pltpu.VMEM((tm, d), f32)],
        compiler_params=pltpu.CompilerParams(dimension_semantics=("arbitrary",),
                                             vmem_limit_bytes=VMEM_LIMIT),
        name="post_gla" if gla else "post_swa",
    )(*args)


def kernel(x, c, ctx, c_ctx, w_ada, b_ada, norm_mix, norm_ffn, gla_w_in, gla_w_gate_up, gla_b_gate_up,
           gla_w_o, gla_o_norm, swa_w_qkv, swa_b_qkv, swa_w_o, swa_sinks, ffn_w_in, ffn_w_out,
           norm_final):
    n_batch, seq, d = x.shape
    ctx_len = ctx.shape[1]
    depth = w_ada.shape[0]
    tm = ROW_TILE
    assert seq % tm == 0 and (n_batch * ctx_len) == tm
    n_lat_tiles = n_batch * seq // tm
    geo = (n_lat_tiles, seq // tm, n_batch)

    mod_all = _ada_mod(jnp.concatenate([c, c_ctx[None, :]], axis=0), w_ada, b_ada)
    x_all = jnp.concatenate([x.reshape(n_batch * seq, d), ctx.reshape(n_batch * ctx_len, d)], axis=0)
    cos_l, sin_l = _rope_tables(seq, tm)

    for i in range(depth):
        last = i == depth - 1
        j = i // 2
        mod = mod_all[i].reshape(8, 1, 6 * d)
        n_tiles = n_lat_tiles if last else n_lat_tiles + 1
        nf = norm_final if last else None
        if i % 2 == 0:
            qk, r, vt, gates = _gla_proj(x_all, mod, norm_mix[i], gla_w_in[j], gla_w_gate_up[j],
                                         gla_b_gate_up[j], geo)
            o_f, o_b = _gla_scan(qk, vt, gates, n_batch, seq, ctx_len)
            x_all = _post(x_all, (o_f, o_b, r), mod, gla_w_o[j], norm_ffn[i], ffn_w_in[i], ffn_w_out[i],
                          geo, True, nf, n_tiles, o_norm=gla_o_norm[j])
        else:
            q, kx, vx = _swa_proj(x_all, mod, norm_mix[i], swa_w_qkv[j], swa_b_qkv[j], cos_l, sin_l, geo)
            attn = _swa_attention(q, kx, vx, swa_sinks[j], n_batch, seq, ctx_len, not last)
            x_all = _post(x_all, attn, mod, swa_w_o[j], norm_ffn[i], ffn_w_in[i], ffn_w_out[i],
                          geo, False, nf, n_tiles)
    return x_all.reshape(n_batch, seq, d)
```

```python
import functools

import jax
import jax.numpy as jnp
from jax import lax
from jax.experimental import pallas as pl
from jax.experimental.pallas import tpu as pltpu

f32 = jnp.float32
bf16 = jnp.bfloat16

EPS = 1e-6
NEG_INF = -1e30
GRID_W = 64
ROPE_THETA = 10000.0

GLA_HEADS = 4
GLA_GATE_RANK = 16
GLA_GATE_NORM = 16.0
GLA_CHUNK = 64
GLA_BLOCK = 256

SWA_HEADS = 16
SWA_KV_HEADS = 2
SWA_HD = 64
SWA_WINDOW = 128
SWA_BLOCK = 128
ATTN_STACK = 1

LANES = 128
FFN_CHUNK = 256
ROW_TILE = 512
VMEM_LIMIT = 56 * 1024 * 1024


def _nt(a, b):
    return lax.dot_general(a, b, (((1,), (1,)), ((), ())), preferred_element_type=f32)


def _nn(a, b):
    return jnp.dot(a, b, preferred_element_type=f32)


def _silu(v):
    return v * (1.0 / (1.0 + jnp.exp(-v)))


def _rms(v, w):
    return v * lax.rsqrt(jnp.mean(v * v, axis=-1, keepdims=True) + EPS) * w


def _const_spec(shape):
    zeros = (0,) * len(shape)
    return pl.BlockSpec(shape, lambda *_: zeros, pipeline_mode=pl.Buffered(1))


def _ada_kernel(cb_ref, w_ref, b_ref, out_ref, *, n_rows, d_model, tn):
    ncol = tn // LANES

    def body(i, accs):
        d0 = pl.multiple_of(i * 8, 8)
        w8 = w_ref[0, pl.ds(d0, 8), :]
        new = []
        for r in range(n_rows):
            s8 = _silu(cb_ref[r, pl.ds(d0, 8), :])
            new.append(tuple(accs[r][j] + w8[:, j * LANES:(j + 1) * LANES] * s8 for j in range(ncol)))
        return tuple(new)

    init = tuple(tuple(jnp.zeros((8, LANES), f32) for _ in range(ncol)) for _ in range(n_rows))
    accs = lax.fori_loop(0, d_model // 8, body, init)
    rows = [jnp.concatenate([jnp.sum(a, axis=0, keepdims=True) for a in accs[r]], axis=1)
            for r in range(n_rows)]
    rows += [jnp.zeros((1, tn), f32)] * (8 - n_rows)
    out_ref[0] = jnp.concatenate(rows, axis=0) + b_ref[0]


def _ada_mod(cvecs, w_ada, b_ada):
    n_rows, d_model = cvecs.shape
    depth, _, e = w_ada.shape
    tn = 768
    cb = jnp.broadcast_to(cvecs[:, :, None], (n_rows, d_model, LANES))
    return pl.pallas_call(
        functools.partial(_ada_kernel, n_rows=n_rows, d_model=d_model, tn=tn),
        grid=(depth, e // tn),
        in_specs=[
            pl.BlockSpec((n_rows, d_model, LANES), lambda l, j: (0, 0, 0)),
            pl.BlockSpec((1, d_model, tn), lambda l, j: (l, 0, j)),
            pl.BlockSpec((1, 1, tn), lambda l, j: (l, 0, j)),
        ],
        out_specs=pl.BlockSpec((1, 8, tn), lambda l, j: (l, 0, j)),
        out_shape=jax.ShapeDtypeStruct((depth, 8, e), f32),
        compiler_params=pltpu.CompilerParams(dimension_semantics=("arbitrary", "arbitrary")),
        name="ada_mod",
    )(cb, w_ada, b_ada.reshape(depth, 1, e))


def _mod_slices(mod_ref, d, first):
    return tuple(mod_ref[0, :, (first + k) * d:(first + k + 1) * d] for k in range(3))


def _x_specs(x_src, n_lat_tiles):
    tm = ROW_TILE
    if isinstance(x_src, tuple):
        lat, ctx = x_src
        d = lat.shape[1]
        assert ctx.shape == (tm, d)
        return [lat, ctx], [pl.BlockSpec((tm, d), lambda i: (jnp.minimum(i, n_lat_tiles - 1), 0)),
                            pl.BlockSpec((tm, d), lambda i: (0, 0))]
    return [x_src], [pl.BlockSpec((tm, x_src.shape[1]), lambda i: (i, 0))]


def _load_x(x_refs, n_lat_tiles):
    if len(x_refs) == 2:
        return jnp.where(pl.program_id(0) < n_lat_tiles, x_refs[0][...], x_refs[1][...])
    return x_refs[0][...]


def _group_map(n_lat_tiles, tiles_per_batch, n_batch):
    def index_map(i):
        return (jnp.where(i < n_lat_tiles, i // tiles_per_batch, n_batch), 0, 0)
    return index_map


def _gla_proj_kernel(*refs, d, dk, dv, heads, n_x, n_lat_tiles):
    x_refs, (mod_ref, nw_ref, win_ref, wvt_ref, wup_ref, bup_ref, qk_ref, r_ref, vt_ref, g_ref) = (
        refs[:n_x], refs[n_x:])
    sh, sc, _ = _mod_slices(mod_ref, d, 0)
    h = (_rms(_load_x(x_refs, n_lat_tiles), nw_ref[...]) * (1.0 + sc) + sh).astype(bf16)
    hk = dk // heads
    qk = _nn(h, win_ref[:, :2 * dk])
    q = qk[:, :dk] * (hk ** -0.5)
    for hh in range(heads):
        qk_ref[hh] = q[:, hh * hk:(hh + 1) * hk].astype(bf16)
        qk_ref[heads + hh] = qk[:, dk + hh * hk:dk + (hh + 1) * hk].astype(bf16)
    r_ref[...] = _silu(_nn(h, win_ref[:, 2 * dk + dv:2 * dk + 2 * dv])).astype(bf16)
    vt_ref[...] = _nt(wvt_ref[...], h).astype(bf16)
    low = _nn(h, win_ref[:, 2 * dk + 2 * dv:]).astype(bf16)
    z = _nn(low, wup_ref[...]) + bup_ref[...]
    lg = (jnp.minimum(z, 0.0) - jnp.log(1.0 + jnp.exp(-jnp.abs(z)))) * (1.0 / GLA_GATE_NORM)
    for hh in range(2 * heads):
        g_ref[hh] = lg[:, hh * hk:(hh + 1) * hk]


def _gla_proj(x_src, t_all, mod, norm_w, w_in, w_gate_up, b_gate_up, geo):
    d = w_in.shape[0]
    dk, dv, heads = d // 2, d, GLA_HEADS
    hk = dk // heads
    rank = GLA_GATE_RANK
    tm = ROW_TILE
    wvt = w_in[:, 2 * dk:2 * dk + dv].T.astype(bf16)
    wup = jnp.zeros((2 * rank, 2 * dk), f32)
    wup = wup.at[:rank, :dk].set(w_gate_up[0]).at[rank:, dk:].set(w_gate_up[1]).astype(bf16)
    bup = b_gate_up.reshape(1, 2 * dk)
    x_args, x_specs = _x_specs(x_src, geo[0])
    kern = functools.partial(_gla_proj_kernel, d=d, dk=dk, dv=dv, heads=heads, n_x=len(x_args),
                             n_lat_tiles=geo[0])
    return pl.pallas_call(
        kern,
        grid=(t_all // tm,),
        in_specs=x_specs + [
            pl.BlockSpec((1, 1, 6 * d), _group_map(*geo)),
            _const_spec((1, d)),
            _const_spec(w_in.shape),
            _const_spec((dv, d)),
            _const_spec((2 * rank, 2 * dk)),
            _const_spec((1, 2 * dk)),
        ],
        out_specs=[
            pl.BlockSpec((2 * heads, tm, hk), lambda i: (0, i, 0)),
            pl.BlockSpec((tm, dv), lambda i: (i, 0)),
            pl.BlockSpec((dv, tm), lambda i: (0, i)),
            pl.BlockSpec((2 * heads, tm, hk), lambda i: (0, i, 0)),
        ],
        out_shape=[
            jax.ShapeDtypeStruct((2 * heads, t_all, hk), bf16),
            jax.ShapeDtypeStruct((t_all, dv), bf16),
            jax.ShapeDtypeStruct((dv, t_all), bf16),
            jax.ShapeDtypeStruct((2 * heads, t_all, hk), f32),
        ],
        compiler_params=pltpu.CompilerParams(dimension_semantics=("arbitrary",),
                                             vmem_limit_bytes=VMEM_LIMIT),
        name="gla_proj",
    )(*x_args, mod, norm_w.reshape(1, d), w_in.astype(bf16), wvt, wup, bup)


def _gla_block(q, k, vt, g, s_ref, o_ref, reverse):
    tb, hk = q.shape
    c = GLA_CHUNK
    half = tb // 2
    assert tb == 4 * c
    row = lax.broadcasted_iota(jnp.int32, (tb, tb), 0)
    col = lax.broadcasted_iota(jnp.int32, (tb, tb), 1)
    causal = ((row // c) == (col // c)) & ((col >= row) if reverse else (col <= row))
    cum = jnp.where(causal, 1.0, 0.0).astype(bf16)
    g_hi = g.astype(bf16)
    g_lo = (g - g_hi.astype(f32)).astype(bf16)
    b2 = _nn(cum, jnp.concatenate([g_hi, g_lo], axis=1))
    b = b2[:, :hk] + b2[:, hk:]
    order = (3, 2, 1, 0) if reverse else (0, 1, 2, 3)
    pos = {ch: s for s, ch in enumerate(order)}
    tot = []
    for ch in range(4):
        last = ch * c if reverse else ch * c + c - 1
        tot.append(b[last:last + 1, :])
    zero = jnp.zeros_like(tot[0])
    before = [sum((tot[order[s2]] for s2 in range(pos[ch])), zero) for ch in range(4)]
    after = [sum((tot[order[s2]] for s2 in range(pos[ch] + 1, 4)), zero) for ch in range(4)]

    def rows(vals):
        return jnp.concatenate([jnp.broadcast_to(v, (c, hk)) for v in vals], axis=0)

    b_tot = rows(tot)
    kf = k.astype(f32)
    qd = q.astype(f32) * jnp.exp(b)
    ki = kf * jnp.exp(-b)
    ke = kf * jnp.exp(b_tot - b)
    qd16, ki16, ke16 = qd.astype(bf16), ki.astype(bf16), ke.astype(bf16)

    lrow = lax.broadcasted_iota(jnp.int32, (half, half), 0)
    lcol = lax.broadcasted_iota(jnp.int32, (half, half), 1)
    m_same = ((lrow // c) == (lcol // c)) & ((lcol >= lrow) if reverse else (lcol <= lrow))
    m_next = ((lrow < c) & (lcol >= c)) if reverse else ((lrow >= c) & (lcol < c))
    diag = []
    for lo in (0, half):
        x = _nt(qd16[lo:lo + half], jnp.concatenate([ki16[lo:lo + half], ke16[lo:lo + half]], axis=0))
        diag.append(jnp.where(m_same, x[:, :half], jnp.where(m_next, x[:, half:], 0.0)).astype(bf16))

    late, early = (0, half) if reverse else (half, 0)
    q_far = order[3] * c - late
    k_far = order[0] * c - early
    q_scale = [zero, zero]
    q_scale[q_far // c] = tot[order[2]]
    k_scale = [zero, zero]
    k_scale[k_far // c] = tot[order[1]]
    q2 = (qd[late:late + half] * jnp.exp(rows(q_scale))).astype(bf16)
    k2 = (ke[early:early + half] * jnp.exp(rows(k_scale))).astype(bf16)
    cross = _nt(q2, k2).astype(bf16)

    if reverse:
        p_early = diag[1]
        p_late = jnp.concatenate([diag[0], cross], axis=1)
    else:
        p_early = diag[0]
        p_late = jnp.concatenate([cross, diag[1]], axis=1)
    s = s_ref[...]
    inter = _nt((qd * jnp.exp(rows(before))).astype(bf16), s.astype(bf16))
    o_ref[early:early + half, :] = _nt(p_early, vt[:, early:early + half]) + inter[early:early + half]
    o_ref[late:late + half, :] = _nt(p_late, vt) + inter[late:late + half]
    k_end = (ke * jnp.exp(rows(after))).astype(bf16)
    s_ref[...] = s * jnp.exp(before[order[3]] + tot[order[3]]) + _nn(vt, k_end)


def _gla_scan_kernel(qkf_ref, vtf_ref, gf_ref, qkb_ref, vtb_ref, gb_ref, of_ref, ob_ref, s_ref, *, heads):
    @pl.when(pl.program_id(1) == 0)
    def _():
        s_ref[...] = jnp.zeros_like(s_ref)

    hv = vtf_ref.shape[0] // heads
    for hh in range(heads):
        cols = slice(hh * hv, (hh + 1) * hv)
        _gla_block(qkf_ref[hh], qkf_ref[heads + hh], vtf_ref[cols, :], gf_ref[hh],
                   s_ref.at[0, hh], of_ref.at[:, cols], False)
        _gla_block(qkb_ref[hh], qkb_ref[heads + hh], vtb_ref[cols, :], gb_ref[hh],
                   s_ref.at[1, hh], ob_ref.at[:, cols], True)


def _gla_scan(qk, vt, gates, n_batch, seq, ctx_len):
    heads = GLA_HEADS
    _, t_all, hk = qk.shape
    dv = vt.shape[0]
    hv = dv // heads
    tb = GLA_BLOCK
    assert ctx_len == tb and seq % tb == 0
    nlat = seq // tb
    ctx0 = n_batch * nlat

    def fwd_row(b, t):
        return jnp.where(t == 0, ctx0 + b, b * nlat + t - 1)

    def bwd_row(b, t):
        return jnp.where(t == 0, ctx0 + b, b * nlat + nlat - t)

    def specs(row_fn, direction):
        return [
            pl.BlockSpec((2 * heads, tb, hk), lambda b, t: (0, row_fn(b, t), 0)),
            pl.BlockSpec((dv, tb), lambda b, t: (0, row_fn(b, t))),
            pl.BlockSpec((heads, tb, hk), lambda b, t: (direction, row_fn(b, t), 0)),
        ]

    return pl.pallas_call(
        functools.partial(_gla_scan_kernel, heads=heads),
        grid=(n_batch, nlat + 1),
        in_specs=specs(fwd_row, 0) + specs(bwd_row, 1),
        out_specs=[
            pl.BlockSpec((tb, dv), lambda b, t: (fwd_row(b, t), 0)),
            pl.BlockSpec((tb, dv), lambda b, t: (bwd_row(b, t), 0)),
        ],
        out_shape=[jax.ShapeDtypeStruct((t_all, dv), f32)] * 2,
        scratch_shapes=[pltpu.VMEM((2, heads, hv, hk), f32)],
        compiler_params=pltpu.CompilerParams(dimension_semantics=("arbitrary", "arbitrary"),
                                             vmem_limit_bytes=VMEM_LIMIT),
        name="gla_scan",
    )(qk, vt, gates, qk, vt, gates)


def _swa_proj_kernel(x_ref, mod_ref, nw_ref, w_ref, b_ref, cos_ref, sin_ref,
                     q_ref, kx_ref, vx_ref, *, d, nq):
    sh, sc, _ = _mod_slices(mod_ref, d, 0)
    h = (_rms(x_ref[...], nw_ref[...]) * (1.0 + sc) + sh).astype(bf16)
    qkv = _nn(h, w_ref[...]) + b_ref[...]
    cos = cos_ref[...]
    sin = sin_ref[...]
    lane = lax.broadcasted_iota(jnp.int32, cos.shape, 1)
    half = SWA_HD // 2
    first_half = (lane % SWA_HD) < half
    low_head = lane < SWA_HD

    def rope(v):
        partner = jnp.where(first_half, pltpu.roll(v, LANES - half, 1), pltpu.roll(v, half, 1))
        return v * cos + partner * sin

    for j in range(nq // LANES):
        q_ref[:, j * LANES:(j + 1) * LANES] = (
            rope(qkv[:, j * LANES:(j + 1) * LANES]) * (SWA_HD ** -0.5)).astype(bf16)
    k = rope(qkv[:, nq:nq + LANES])
    v = qkv[:, nq + LANES:nq + 2 * LANES]
    for src, dst in ((k, kx_ref), (v, vx_ref)):
        swapped = pltpu.roll(src, SWA_HD, 1)
        dst[0] = jnp.where(low_head, src, 0.0).astype(bf16)
        dst[1] = jnp.where(low_head, 0.0, swapped).astype(bf16)
        dst[2] = jnp.where(low_head, swapped, 0.0).astype(bf16)
        dst[3] = jnp.where(low_head, 0.0, src).astype(bf16)


def _rope_tables(seq, extra_rows):
    rows = seq // GRID_W
    row = jnp.broadcast_to(jnp.arange(rows)[:, None], (rows, GRID_W)).reshape(-1).astype(f32)
    col = jnp.broadcast_to(jnp.arange(GRID_W)[None, :], (rows, GRID_W)).reshape(-1).astype(f32)
    n_freq = SWA_HD // 4
    inv = ROPE_THETA ** (-jnp.arange(n_freq, dtype=f32) / n_freq)
    ang = jnp.concatenate([row[:, None] * inv, col[:, None] * inv], axis=-1)
    cos, sin = jnp.cos(ang), jnp.sin(ang)
    reps = LANES // SWA_HD
    cos_l = jnp.tile(jnp.concatenate([cos, cos], axis=-1), (1, reps))
    sin_l = jnp.tile(jnp.concatenate([-sin, sin], axis=-1), (1, reps))
    cos_l = jnp.concatenate([cos_l, jnp.ones((extra_rows, LANES), f32)], axis=0)
    sin_l = jnp.concatenate([sin_l, jnp.zeros((extra_rows, LANES), f32)], axis=0)
    return cos_l, sin_l


def _swa_proj(x_all, mod, norm_w, w_qkv, b_qkv, cos_l, sin_l, geo):
    t_all, d = x_all.shape
    nq = SWA_HEADS * SWA_HD
    nqkv = w_qkv.shape[1]
    tm = ROW_TILE
    n_lat_tiles, tiles_per_batch, _ = geo

    def pos_map(i):
        return (jnp.where(i < n_lat_tiles, i % tiles_per_batch, tiles_per_batch), 0)

    return pl.pallas_call(
        functools.partial(_swa_proj_kernel, d=d, nq=nq),
        grid=(t_all // tm,),
        in_specs=[
            pl.BlockSpec((tm, d), lambda i: (i, 0)),
            pl.BlockSpec((1, 1, 6 * d), _group_map(*geo)),
            _const_spec((1, d)),
            _const_spec((d, nqkv)),
            _const_spec((1, nqkv)),
            pl.BlockSpec((tm, LANES), pos_map),
            pl.BlockSpec((tm, LANES), pos_map),
        ],
        out_specs=[
            pl.BlockSpec((tm, nq), lambda i: (i, 0)),
            pl.BlockSpec((4, tm, LANES), lambda i: (0, i, 0)),
            pl.BlockSpec((4, tm, LANES), lambda i: (0, i, 0)),
        ],
        out_shape=[
            jax.ShapeDtypeStruct((t_all, nq), bf16),
            jax.ShapeDtypeStruct((4, t_all, LANES), bf16),
            jax.ShapeDtypeStruct((4, t_all, LANES), bf16),
        ],
        compiler_params=pltpu.CompilerParams(dimension_semantics=("arbitrary",),
                                             vmem_limit_bytes=VMEM_LIMIT),
        name="swa_proj",
    )(x_all, mod, norm_w.reshape(1, d), w_qkv.astype(bf16), b_qkv.reshape(1, nqkv), cos_l, sin_l)


def _attend(q_ref, o_ref, sink_ref, k_pieces, v_pieces, mask, n_masked):
    tq = q_ref.shape[0]
    pairs = q_ref.shape[1] // LANES
    group = pairs // SWA_KV_HEADS
    stack = ATTN_STACK
    lane = lax.broadcasted_iota(jnp.int32, (tq, LANES), 1)
    low_head = lane < SWA_HD
    k_cat, v_cat = [], []
    for g in range(SWA_KV_HEADS):
        k_cat.append(jnp.concatenate(k_pieces[g][0] + k_pieces[g][1], axis=0))
        nk = k_cat[g].shape[0] // 2
        lane2 = lax.broadcasted_iota(jnp.int32, (nk, LANES), 1)
        ones_lo = jnp.where(lane2 < SWA_HD, 1.0, 0.0).astype(bf16)
        ones_hi = jnp.where(lane2 < SWA_HD, 0.0, 1.0).astype(bf16)
        v_cat.append(jnp.concatenate([
            jnp.concatenate([jnp.concatenate(v_pieces[g][0], axis=0), ones_lo], axis=1),
            jnp.concatenate([jnp.concatenate(v_pieces[g][1], axis=0), ones_hi], axis=1)], axis=0))
    units = [(u * stack) // group for u in range(pairs // stack)]

    def scores(u):
        q_stack = jnp.concatenate(
            [q_ref[:, (u * stack + jj) * LANES:(u * stack + jj + 1) * LANES] for jj in range(stack)], axis=0)
        return _nt(q_stack, k_cat[units[u]])

    s_next = scores(0)
    for u, g in enumerate(units):
        s_all = s_next
        if u + 1 < len(units):
            s_next = scores(u + 1)
        prob_rows, sink_rows = [], []
        for jj in range(stack):
            j = u * stack + jj
            probs, sink_terms = [], []
            for p in range(2):
                sp = s_all[jj * tq:(jj + 1) * tq, p * nk:(p + 1) * nk]
                if mask is not None:
                    sp = jnp.concatenate(
                        [jnp.where(mask, sp[:, :n_masked], NEG_INF), sp[:, n_masked:]], axis=1)
                sink = sink_ref[2 * j + p]
                m = jnp.maximum(jnp.max(sp, axis=-1, keepdims=True), sink)
                probs.append(jnp.exp(sp - m).astype(bf16))
                sink_terms.append(jnp.exp(sink - m))
            prob_rows.append(jnp.concatenate(probs, axis=1))
            sink_rows.append(jnp.where(low_head, sink_terms[0], sink_terms[1]))
        out_all = _nn(jnp.concatenate(prob_rows, axis=0), v_cat[g])
        for jj in range(stack):
            j = u * stack + jj
            out = out_all[jj * tq:(jj + 1) * tq]
            den = out[:, LANES:] + sink_rows[jj]
            o_ref[:, j * LANES:(j + 1) * LANES] = (out[:, :LANES] / den).astype(bf16)


def _swa_lat_kernel(sink_ref, q_ref, kp_ref, ko_ref, kn_ref, kc_ref, vp_ref, vo_ref, vn_ref, vc_ref,
                    o_ref, *, seq):
    i = pl.program_id(1)
    blk = SWA_BLOCK
    qi = i * blk + lax.broadcasted_iota(jnp.int32, (blk, 3 * blk), 0)
    kj = (i - 1) * blk + lax.broadcasted_iota(jnp.int32, (blk, 3 * blk), 1)
    mask = (jnp.abs(kj - qi) <= SWA_WINDOW) & (kj >= 0) & (kj < seq)
    k_pieces = [[[r[2 * g + p] for r in (kp_ref, ko_ref, kn_ref, kc_ref)] for p in range(2)]
                for g in range(SWA_KV_HEADS)]
    v_pieces = [[[r[2 * g + p] for r in (vp_ref, vo_ref, vn_ref, vc_ref)] for p in range(2)]
                for g in range(SWA_KV_HEADS)]
    _attend(q_ref, o_ref, sink_ref, k_pieces, v_pieces, mask, 3 * blk)


def _swa_ctx_kernel(sink_ref, q_ref, kc_ref, vc_ref, lat_ref, o_ref):
    del lat_ref
    k_pieces = [[[kc_ref[2 * g + p]] for p in range(2)] for g in range(SWA_KV_HEADS)]
    v_pieces = [[[vc_ref[2 * g + p]] for p in range(2)] for g in range(SWA_KV_HEADS)]
    _attend(q_ref, o_ref, sink_ref, k_pieces, v_pieces, None, 0)


def _swa_attention(q, kx, vx, sinks, n_batch, seq, ctx_len, need_ctx):
    t_all, nq = q.shape
    blk = SWA_BLOCK
    nb = seq // blk
    ncb = ctx_len // blk
    lat_blocks = n_batch * nb
    out_rows = t_all if need_ctx else n_batch * seq
    smem = pl.BlockSpec(memory_space=pltpu.SMEM)

    def win(off):
        return pl.BlockSpec((4, blk, LANES),
                            lambda b, i: (0, b * nb + jnp.clip(i + off, 0, nb - 1), 0))

    def ctx_spec():
        return pl.BlockSpec((4, ctx_len, LANES), lambda b, i: (0, (n_batch * seq) // ctx_len + b, 0))

    out = pl.pallas_call(
        functools.partial(_swa_lat_kernel, seq=seq),
        grid=(n_batch, nb),
        in_specs=[smem, pl.BlockSpec((blk, nq), lambda b, i: (b * nb + i, 0)),
                  win(-1), win(0), win(1), ctx_spec(), win(-1), win(0), win(1), ctx_spec()],
        out_specs=pl.BlockSpec((blk, nq), lambda b, i: (b * nb + i, 0)),
        out_shape=jax.ShapeDtypeStruct((out_rows, nq), bf16),
        compiler_params=pltpu.CompilerParams(dimension_semantics=("arbitrary", "arbitrary"),
                                             vmem_limit_bytes=VMEM_LIMIT),
        name="swa_attn",
    )(sinks, q, kx, kx, kx, kx, vx, vx, vx, vx)
    if not need_ctx:
        return out
    return pl.pallas_call(
        _swa_ctx_kernel,
        grid=(n_batch, ncb),
        in_specs=[smem, pl.BlockSpec((blk, nq), lambda b, i: (lat_blocks + b * ncb + i, 0)),
                  ctx_spec(), ctx_spec(), pl.BlockSpec(memory_space=pl.ANY)],
        out_specs=pl.BlockSpec((blk, nq), lambda b, i: (lat_blocks + b * ncb + i, 0)),
        out_shape=jax.ShapeDtypeStruct((out_rows, nq), bf16),
        input_output_aliases={4: 0},
        compiler_params=pltpu.CompilerParams(dimension_semantics=("arbitrary", "arbitrary")),
        name="swa_attn_ctx",
    )(sinks, q, kx, vx, out)


def _post_kernel(*refs, d, gla, final, hidden, n_x, n_lat_tiles):
    x_refs, refs = refs[:n_x], refs[n_x:]
    if gla:
        of_ref, ob_ref, r_ref, on_ref = refs[:4]
        rest = refs[4:]
    else:
        a_ref = refs[0]
        rest = refs[1:]
    mod_ref, wo_ref, nw_ref, win_ref, wout_ref = rest[:5]
    rest = rest[5:]
    if final:
        nf_ref, out_ref, acc_ref = rest
    else:
        out_ref, acc_ref = rest

    if gla:
        o = of_ref[...] + ob_ref[...]
        hv = d // GLA_HEADS
        parts = []
        for hh in range(GLA_HEADS):
            seg = o[:, hh * hv:(hh + 1) * hv]
            parts.append(_rms(seg, on_ref[:, hh * hv:(hh + 1) * hv]))
        a = (jnp.concatenate(parts, axis=1) * r_ref[...].astype(f32)).astype(bf16)
    else:
        a = a_ref[...]
    _, _, g1 = _mod_slices(mod_ref, d, 0)
    sh2, sc2, g2 = _mod_slices(mod_ref, d, 3)
    x1 = _load_x(x_refs, n_lat_tiles) + g1 * _nn(a, wo_ref[...])
    hn = (_rms(x1, nw_ref[...]) * (1.0 + sc2) + sh2).astype(bf16)
    acc_ref[...] = jnp.zeros_like(acc_ref)

    def body(ci, carry):
        off = pl.multiple_of(ci * FFN_CHUNK, FFN_CHUNK)
        gate = _nn(hn, win_ref[:, pl.ds(off, FFN_CHUNK)])
        up = _nn(hn, win_ref[:, pl.ds(hidden + off, FFN_CHUNK)])
        act = (_silu(gate) * up).astype(bf16)
        acc_ref[...] += _nn(act, wout_ref[pl.ds(off, FFN_CHUNK), :])
        return carry

    lax.fori_loop(0, hidden // FFN_CHUNK, body, 0)
    x2 = x1 + g2 * acc_ref[...]
    if final:
        x2 = _rms(x2, nf_ref[...])
    out_ref[...] = x2


def _post(x_src, mixer_in, mod, w_o, norm_w, w_in, w_out, geo, gla, norm_final, n_tiles, o_norm=None):
    hidden, d = w_out.shape
    assert hidden % FFN_CHUNK == 0
    tm = ROW_TILE
    final = norm_final is not None
    row = lambda i: (i, 0)
    args, in_specs = _x_specs(x_src, geo[0])
    n_x = len(args)
    if gla:
        o_f, o_b, r = mixer_in
        args += [o_f, o_b, r, jnp.tile(o_norm, GLA_HEADS).reshape(1, d)]
        in_specs += [pl.BlockSpec((tm, d), row)] * 3 + [_const_spec((1, d))]
    else:
        args += [mixer_in]
        in_specs += [pl.BlockSpec((tm, d), row)]
    args += [mod, w_o.astype(bf16), norm_w.reshape(1, d), w_in.astype(bf16), w_out.astype(bf16)]
    in_specs += [pl.BlockSpec((1, 1, 6 * d), _group_map(*geo)), _const_spec((d, d)), _const_spec((1, d)),
                 _const_spec((d, 2 * hidden)), _const_spec((hidden, d))]
    if final:
        args += [norm_final.reshape(1, d)]
        in_specs += [_const_spec((1, d))]
    return pl.pallas_call(
        functools.partial(_post_kernel, d=d, gla=gla, final=final, hidden=hidden, n_x=n_x,
                          n_lat_tiles=geo[0]),
        grid=(n_tiles,),
        in_specs=in_specs,
        out_specs=pl.BlockSpec((tm, d), row),
        out_shape=jax.ShapeDtypeStruct((n_tiles * tm, d), f32),
        scratch_shapes=[pltpu.VMEM((tm, d), f32)],
        compiler_params=pltpu.CompilerParams(dimension_semantics=("arbitrary",),
                                             vmem_limit_bytes=VMEM_LIMIT),
        name="post_gla" if gla else "post_swa",
    )(*args)


def kernel(x, c, ctx, c_ctx, w_ada, b_ada, norm_mix, norm_ffn, gla_w_in, gla_w_gate_up, gla_b_gate_up,
           gla_w_o, gla_o_norm, swa_w_qkv, swa_b_qkv, swa_w_o, swa_sinks, ffn_w_in, ffn_w_out,
           norm_final):
    n_batch, seq, d = x.shape
    ctx_len = ctx.shape[1]
    depth = w_ada.shape[0]
    tm = ROW_TILE
    assert seq % tm == 0 and (n_batch * ctx_len) == tm
    n_lat_tiles = n_batch * seq // tm
    geo = (n_lat_tiles, seq // tm, n_batch)

    mod_all = _ada_mod(jnp.concatenate([c, c_ctx[None, :]], axis=0), w_ada, b_ada)
    t_all = n_batch * (seq + ctx_len)
    x_src = (x.reshape(n_batch * seq, d), ctx.reshape(n_batch * ctx_len, d))
    cos_l, sin_l = _rope_tables(seq, tm)

    for i in range(depth):
        last = i == depth - 1
        j = i // 2
        mod = mod_all[i].reshape(8, 1, 6 * d)
        n_tiles = n_lat_tiles if last else n_lat_tiles + 1
        nf = norm_final if last else None
        if i % 2 == 0:
            qk, r, vt, gates = _gla_proj(x_src, t_all, mod, norm_mix[i], gla_w_in[j], gla_w_gate_up[j],
                                         gla_b_gate_up[j], geo)
            o_f, o_b = _gla_scan(qk, vt, gates, n_batch, seq, ctx_len)
            x_src = _post(x_src, (o_f, o_b, r), mod, gla_w_o[j], norm_ffn[i], ffn_w_in[i], ffn_w_out[i],
                          geo, True, nf, n_tiles, o_norm=gla_o_norm[j])
        else:
            q, kx, vx = _swa_proj(x_src, mod, norm_mix[i], swa_w_qkv[j], swa_b_qkv[j], cos_l, sin_l, geo)
            attn = _swa_attention(q, kx, vx, swa_sinks[j], n_batch, seq, ctx_len, not last)
            x_src = _post(x_src, attn, mod, swa_w_o[j], norm_ffn[i], ffn_w_in[i], ffn_w_out[i],
                          geo, False, nf, n_tiles)
    return x_src.reshape(n_batch, seq, d)
```

```python
import functools

import jax
import jax.numpy as jnp
from jax import lax
from jax.experimental import pallas as pl
from jax.experimental.pallas import tpu as pltpu

f32 = jnp.float32
bf16 = jnp.bfloat16

EPS = 1e-6
NEG_INF = -1e30
GRID_W = 64
ROPE_THETA = 10000.0

GLA_HEADS = 4
GLA_GATE_RANK = 16
GLA_GATE_NORM = 16.0
GLA_CHUNK = 64
GLA_BLOCK = 256

SWA_HEADS = 16
SWA_KV_HEADS = 2
SWA_HD = 64
SWA_WINDOW = 128
SWA_BLOCK = 128
ATTN_STACK = 1

LANES = 128
FFN_CHUNK = 256
ROW_TILE = 512
VMEM_LIMIT = 56 * 1024 * 1024


def _nt(a, b):
    return lax.dot_general(a, b, (((1,), (1,)), ((), ())), preferred_element_type=f32)


def _nn(a, b):
    return jnp.dot(a, b, preferred_element_type=f32)


def _silu(v):
    return v * (1.0 / (1.0 + jnp.exp(-v)))


def _rms(v, w):
    return v * lax.rsqrt(jnp.mean(v * v, axis=-1, keepdims=True) + EPS) * w


def _const_spec(shape):
    zeros = (0,) * len(shape)
    return pl.BlockSpec(shape, lambda *_: zeros, pipeline_mode=pl.Buffered(1))


def _ada_kernel(cb_ref, w_ref, b_ref, out_ref, *, n_rows, d_model, tn):
    ncol = tn // LANES

    def body(i, accs):
        d0 = pl.multiple_of(i * 8, 8)
        w8 = w_ref[0, pl.ds(d0, 8), :]
        new = []
        for r in range(n_rows):
            s8 = _silu(cb_ref[r, pl.ds(d0, 8), :])
            new.append(tuple(accs[r][j] + w8[:, j * LANES:(j + 1) * LANES] * s8 for j in range(ncol)))
        return tuple(new)

    init = tuple(tuple(jnp.zeros((8, LANES), f32) for _ in range(ncol)) for _ in range(n_rows))
    accs = lax.fori_loop(0, d_model // 8, body, init, unroll=8)
    rows = [jnp.concatenate([jnp.sum(a, axis=0, keepdims=True) for a in accs[r]], axis=1)
            for r in range(n_rows)]
    rows += [jnp.zeros((1, tn), f32)] * (8 - n_rows)
    out_ref[0] = jnp.concatenate(rows, axis=0) + b_ref[0]


def _ada_mod(cvecs, w_ada, b_ada):
    n_rows, d_model = cvecs.shape
    depth, _, e = w_ada.shape
    tn = 768
    cb = jnp.broadcast_to(cvecs[:, :, None], (n_rows, d_model, LANES))
    return pl.pallas_call(
        functools.partial(_ada_kernel, n_rows=n_rows, d_model=d_model, tn=tn),
        grid=(depth, e // tn),
        in_specs=[
            pl.BlockSpec((n_rows, d_model, LANES), lambda l, j: (0, 0, 0)),
            pl.BlockSpec((1, d_model, tn), lambda l, j: (l, 0, j)),
            pl.BlockSpec((1, 1, tn), lambda l, j: (l, 0, j)),
        ],
        out_specs=pl.BlockSpec((1, 8, tn), lambda l, j: (l, 0, j)),
        out_shape=jax.ShapeDtypeStruct((depth, 8, e), f32),
        compiler_params=pltpu.CompilerParams(dimension_semantics=("arbitrary", "arbitrary")),
        name="ada_mod",
    )(cb, w_ada, b_ada.reshape(depth, 1, e))


def _mod_slices(mod_ref, d, first):
    return tuple(mod_ref[0, :, (first + k) * d:(first + k + 1) * d] for k in range(3))


def _x_specs(x_src, n_lat_tiles):
    tm = ROW_TILE
    if isinstance(x_src, tuple):
        lat, ctx = x_src
        d = lat.shape[1]
        assert ctx.shape == (tm, d)
        return [lat, ctx], [pl.BlockSpec((tm, d), lambda i: (jnp.minimum(i, n_lat_tiles - 1), 0)),
                            pl.BlockSpec((tm, d), lambda i: (0, 0))]
    return [x_src], [pl.BlockSpec((tm, x_src.shape[1]), lambda i: (i, 0))]


def _load_x(x_refs, n_lat_tiles):
    if len(x_refs) == 2:
        return jnp.where(pl.program_id(0) < n_lat_tiles, x_refs[0][...], x_refs[1][...])
    return x_refs[0][...]


def _group_map(n_lat_tiles, tiles_per_batch, n_batch):
    def index_map(i):
        return (jnp.where(i < n_lat_tiles, i // tiles_per_batch, n_batch), 0, 0)
    return index_map


def _gla_proj_kernel(*refs, d, dk, dv, heads, n_x, n_lat_tiles):
    x_refs, (mod_ref, nw_ref, win_ref, wvt_ref, wup_ref, bup_ref, qk_ref, r_ref, vt_ref, g_ref) = (
        refs[:n_x], refs[n_x:])
    sh, sc, _ = _mod_slices(mod_ref, d, 0)
    h = (_rms(_load_x(x_refs, n_lat_tiles), nw_ref[...]) * (1.0 + sc) + sh).astype(bf16)
    hk = dk // heads
    qk = _nn(h, win_ref[:, :2 * dk])
    q = qk[:, :dk] * (hk ** -0.5)
    for hh in range(heads):
        qk_ref[hh] = q[:, hh * hk:(hh + 1) * hk].astype(bf16)
        qk_ref[heads + hh] = qk[:, dk + hh * hk:dk + (hh + 1) * hk].astype(bf16)
    r_ref[...] = _silu(_nn(h, win_ref[:, 2 * dk + dv:2 * dk + 2 * dv])).astype(bf16)
    vt_ref[...] = _nt(wvt_ref[...], h).astype(bf16)
    low = _nn(h, win_ref[:, 2 * dk + 2 * dv:]).astype(bf16)
    z = _nn(low, wup_ref[...]) + bup_ref[...]
    lg = (jnp.minimum(z, 0.0) - jnp.log(1.0 + jnp.exp(-jnp.abs(z)))) * (1.0 / GLA_GATE_NORM)
    for hh in range(2 * heads):
        g_ref[hh] = lg[:, hh * hk:(hh + 1) * hk]


def _gla_proj(x_src, t_all, mod, norm_w, w_in, w_gate_up, b_gate_up, geo):
    d = w_in.shape[0]
    dk, dv, heads = d // 2, d, GLA_HEADS
    hk = dk // heads
    rank = GLA_GATE_RANK
    tm = ROW_TILE
    wvt = w_in[:, 2 * dk:2 * dk + dv].T.astype(bf16)
    wup = jnp.zeros((2 * rank, 2 * dk), f32)
    wup = wup.at[:rank, :dk].set(w_gate_up[0]).at[rank:, dk:].set(w_gate_up[1]).astype(bf16)
    bup = b_gate_up.reshape(1, 2 * dk)
    x_args, x_specs = _x_specs(x_src, geo[0])
    kern = functools.partial(_gla_proj_kernel, d=d, dk=dk, dv=dv, heads=heads, n_x=len(x_args),
                             n_lat_tiles=geo[0])
    return pl.pallas_call(
        kern,
        grid=(t_all // tm,),
        in_specs=x_specs + [
            pl.BlockSpec((1, 1, 6 * d), _group_map(*geo)),
            _const_spec((1, d)),
            _const_spec(w_in.shape),
            _const_spec((dv, d)),
            _const_spec((2 * rank, 2 * dk)),
            _const_spec((1, 2 * dk)),
        ],
        out_specs=[
            pl.BlockSpec((2 * heads, tm, hk), lambda i: (0, i, 0)),
            pl.BlockSpec((tm, dv), lambda i: (i, 0)),
            pl.BlockSpec((dv, tm), lambda i: (0, i)),
            pl.BlockSpec((2 * heads, tm, hk), lambda i: (0, i, 0)),
        ],
        out_shape=[
            jax.ShapeDtypeStruct((2 * heads, t_all, hk), bf16),
            jax.ShapeDtypeStruct((t_all, dv), bf16),
            jax.ShapeDtypeStruct((dv, t_all), bf16),
            jax.ShapeDtypeStruct((2 * heads, t_all, hk), f32),
        ],
        compiler_params=pltpu.CompilerParams(dimension_semantics=("arbitrary",),
                                             vmem_limit_bytes=VMEM_LIMIT),
        name="gla_proj",
    )(*x_args, mod, norm_w.reshape(1, d), w_in.astype(bf16), wvt, wup, bup)


def _gla_block(q, k, vt, g, s_ref, o_ref, reverse):
    tb, hk = q.shape
    c = GLA_CHUNK
    half = tb // 2
    assert tb == 4 * c
    row = lax.broadcasted_iota(jnp.int32, (tb, tb), 0)
    col = lax.broadcasted_iota(jnp.int32, (tb, tb), 1)
    causal = ((row // c) == (col // c)) & ((col >= row) if reverse else (col <= row))
    cum = jnp.where(causal, 1.0, 0.0).astype(bf16)
    g_hi = g.astype(bf16)
    g_lo = (g - g_hi.astype(f32)).astype(bf16)
    b2 = _nn(cum, jnp.concatenate([g_hi, g_lo], axis=1))
    yield
    b = b2[:, :hk] + b2[:, hk:]
    order = (3, 2, 1, 0) if reverse else (0, 1, 2, 3)
    pos = {ch: s for s, ch in enumerate(order)}
    tot = []
    for ch in range(4):
        last = ch * c if reverse else ch * c + c - 1
        tot.append(b[last:last + 1, :])
    zero = jnp.zeros_like(tot[0])
    before = [sum((tot[order[s2]] for s2 in range(pos[ch])), zero) for ch in range(4)]
    after = [sum((tot[order[s2]] for s2 in range(pos[ch] + 1, 4)), zero) for ch in range(4)]

    def rows(vals):
        return jnp.concatenate([jnp.broadcast_to(v, (c, hk)) for v in vals], axis=0)

    b_tot = rows(tot)
    kf = k.astype(f32)
    qd = q.astype(f32) * jnp.exp(b)
    ki = kf * jnp.exp(-b)
    ke = kf * jnp.exp(b_tot - b)
    qd16, ki16, ke16 = qd.astype(bf16), ki.astype(bf16), ke.astype(bf16)
    yield

    lrow = lax.broadcasted_iota(jnp.int32, (half, half), 0)
    lcol = lax.broadcasted_iota(jnp.int32, (half, half), 1)
    m_same = ((lrow // c) == (lcol // c)) & ((lcol >= lrow) if reverse else (lcol <= lrow))
    m_next = ((lrow < c) & (lcol >= c)) if reverse else ((lrow >= c) & (lcol < c))
    xs = [_nt(qd16[lo:lo + half], jnp.concatenate([ki16[lo:lo + half], ke16[lo:lo + half]], axis=0))
          for lo in (0, half)]

    late, early = (0, half) if reverse else (half, 0)
    q_far = order[3] * c - late
    k_far = order[0] * c - early
    q_scale = [zero, zero]
    q_scale[q_far // c] = tot[order[2]]
    k_scale = [zero, zero]
    k_scale[k_far // c] = tot[order[1]]
    q2 = (qd[late:late + half] * jnp.exp(rows(q_scale))).astype(bf16)
    k2 = (ke[early:early + half] * jnp.exp(rows(k_scale))).astype(bf16)
    cross_f32 = _nt(q2, k2)
    yield
    cross = cross_f32.astype(bf16)

    diag = [jnp.where(m_same, x[:, :half], jnp.where(m_next, x[:, half:], 0.0)).astype(bf16) for x in xs]
    if reverse:
        p_early = diag[1]
        p_late = jnp.concatenate([diag[0], cross], axis=1)
    else:
        p_early = diag[0]
        p_late = jnp.concatenate([cross, diag[1]], axis=1)
    s = s_ref[...]
    inter = _nt((qd * jnp.exp(rows(before))).astype(bf16), s.astype(bf16))
    pv_early = _nt(p_early, vt[:, early:early + half])
    pv_late = _nt(p_late, vt)
    k_end = (ke * jnp.exp(rows(after))).astype(bf16)
    upd = _nn(vt, k_end)
    yield
    o_ref[early:early + half, :] = pv_early + inter[early:early + half]
    o_ref[late:late + half, :] = pv_late + inter[late:late + half]
    s_ref[...] = s * jnp.exp(before[order[3]] + tot[order[3]]) + upd


def _gla_scan_kernel(qkf_ref, vtf_ref, gf_ref, qkb_ref, vtb_ref, gb_ref, of_ref, ob_ref, s_ref, *, heads):
    @pl.when(pl.program_id(1) == 0)
    def _():
        s_ref[...] = jnp.zeros_like(s_ref)

    hv = vtf_ref.shape[0] // heads
    chains = []
    for hh in range(heads):
        cols = slice(hh * hv, (hh + 1) * hv)
        chains.append(_gla_block(qkf_ref[hh], qkf_ref[heads + hh], vtf_ref[cols, :], gf_ref[hh],
                                 s_ref.at[0, hh], of_ref.at[:, cols], False))
        chains.append(_gla_block(qkb_ref[hh], qkb_ref[heads + hh], vtb_ref[cols, :], gb_ref[hh],
                                 s_ref.at[1, hh], ob_ref.at[:, cols], True))
    while chains:
        alive = []
        for chain in chains:
            try:
                next(chain)
                alive.append(chain)
            except StopIteration:
                pass
        chains = alive


def _gla_scan(qk, vt, gates, n_batch, seq, ctx_len):
    heads = GLA_HEADS
    _, t_all, hk = qk.shape
    dv = vt.shape[0]
    hv = dv // heads
    tb = GLA_BLOCK
    assert ctx_len == tb and seq % tb == 0
    nlat = seq // tb
    ctx0 = n_batch * nlat

    def fwd_row(b, t):
        return jnp.where(t == 0, ctx0 + b, b * nlat + t - 1)

    def bwd_row(b, t):
        return jnp.where(t == 0, ctx0 + b, b * nlat + nlat - t)

    def specs(row_fn, direction):
        return [
            pl.BlockSpec((2 * heads, tb, hk), lambda b, t: (0, row_fn(b, t), 0)),
            pl.BlockSpec((dv, tb), lambda b, t: (0, row_fn(b, t))),
            pl.BlockSpec((heads, tb, hk), lambda b, t: (direction, row_fn(b, t), 0)),
        ]

    return pl.pallas_call(
        functools.partial(_gla_scan_kernel, heads=heads),
        grid=(n_batch, nlat + 1),
        in_specs=specs(fwd_row, 0) + specs(bwd_row, 1),
        out_specs=[
            pl.BlockSpec((tb, dv), lambda b, t: (fwd_row(b, t), 0)),
            pl.BlockSpec((tb, dv), lambda b, t: (bwd_row(b, t), 0)),
        ],
        out_shape=[jax.ShapeDtypeStruct((t_all, dv), f32)] * 2,
        scratch_shapes=[pltpu.VMEM((2, heads, hv, hk), f32)],
        compiler_params=pltpu.CompilerParams(dimension_semantics=("arbitrary", "arbitrary"),
                                             vmem_limit_bytes=VMEM_LIMIT),
        name="gla_scan",
    )(qk, vt, gates, qk, vt, gates)


def _swa_proj_kernel(x_ref, mod_ref, nw_ref, w_ref, b_ref, cos_ref, sin_ref,
                     q_ref, kx_ref, vx_ref, *, d, nq):
    sh, sc, _ = _mod_slices(mod_ref, d, 0)
    h = (_rms(x_ref[...], nw_ref[...]) * (1.0 + sc) + sh).astype(bf16)
    qkv = _nn(h, w_ref[...]) + b_ref[...]
    cos = cos_ref[...]
    sin = sin_ref[...]
    lane = lax.broadcasted_iota(jnp.int32, cos.shape, 1)
    half = SWA_HD // 2
    first_half = (lane % SWA_HD) < half
    low_head = lane < SWA_HD

    def rope(v):
        partner = jnp.where(first_half, pltpu.roll(v, LANES - half, 1), pltpu.roll(v, half, 1))
        return v * cos + partner * sin

    for j in range(nq // LANES):
        q_ref[:, j * LANES:(j + 1) * LANES] = (
            rope(qkv[:, j * LANES:(j + 1) * LANES]) * (SWA_HD ** -0.5)).astype(bf16)
    k = rope(qkv[:, nq:nq + LANES])
    v = qkv[:, nq + LANES:nq + 2 * LANES]
    for src, dst in ((k, kx_ref), (v, vx_ref)):
        swapped = pltpu.roll(src, SWA_HD, 1)
        dst[0] = jnp.where(low_head, src, 0.0).astype(bf16)
        dst[1] = jnp.where(low_head, 0.0, swapped).astype(bf16)
        dst[2] = jnp.where(low_head, swapped, 0.0).astype(bf16)
        dst[3] = jnp.where(low_head, 0.0, src).astype(bf16)


def _rope_tables(seq, extra_rows):
    rows = seq // GRID_W
    row = jnp.broadcast_to(jnp.arange(rows)[:, None], (rows, GRID_W)).reshape(-1).astype(f32)
    col = jnp.broadcast_to(jnp.arange(GRID_W)[None, :], (rows, GRID_W)).reshape(-1).astype(f32)
    n_freq = SWA_HD // 4
    inv = ROPE_THETA ** (-jnp.arange(n_freq, dtype=f32) / n_freq)
    ang = jnp.concatenate([row[:, None] * inv, col[:, None] * inv], axis=-1)
    cos, sin = jnp.cos(ang), jnp.sin(ang)
    reps = LANES // SWA_HD
    cos_l = jnp.tile(jnp.concatenate([cos, cos], axis=-1), (1, reps))
    sin_l = jnp.tile(jnp.concatenate([-sin, sin], axis=-1), (1, reps))
    cos_l = jnp.concatenate([cos_l, jnp.ones((extra_rows, LANES), f32)], axis=0)
    sin_l = jnp.concatenate([sin_l, jnp.zeros((extra_rows, LANES), f32)], axis=0)
    return cos_l, sin_l


def _swa_proj(x_all, mod, norm_w, w_qkv, b_qkv, cos_l, sin_l, geo):
    t_all, d = x_all.shape
    nq = SWA_HEADS * SWA_HD
    nqkv = w_qkv.shape[1]
    tm = ROW_TILE
    n_lat_tiles, tiles_per_batch, _ = geo

    def pos_map(i):
        return (jnp.where(i < n_lat_tiles, i % tiles_per_batch, tiles_per_batch), 0)

    return pl.pallas_call(
        functools.partial(_swa_proj_kernel, d=d, nq=nq),
        grid=(t_all // tm,),
        in_specs=[
            pl.BlockSpec((tm, d), lambda i: (i, 0)),
            pl.BlockSpec((1, 1, 6 * d), _group_map(*geo)),
            _const_spec((1, d)),
            _const_spec((d, nqkv)),
            _const_spec((1, nqkv)),
            pl.BlockSpec((tm, LANES), pos_map),
            pl.BlockSpec((tm, LANES), pos_map),
        ],
        out_specs=[
            pl.BlockSpec((tm, nq), lambda i: (i, 0)),
            pl.BlockSpec((4, tm, LANES), lambda i: (0, i, 0)),
            pl.BlockSpec((4, tm, LANES), lambda i: (0, i, 0)),
        ],
        out_shape=[
            jax.ShapeDtypeStruct((t_all, nq), bf16),
            jax.ShapeDtypeStruct((4, t_all, LANES), bf16),
            jax.ShapeDtypeStruct((4, t_all, LANES), bf16),
        ],
        compiler_params=pltpu.CompilerParams(dimension_semantics=("arbitrary",),
                                             vmem_limit_bytes=VMEM_LIMIT),
        name="swa_proj",
    )(x_all, mod, norm_w.reshape(1, d), w_qkv.astype(bf16), b_qkv.reshape(1, nqkv), cos_l, sin_l)


def _attend(q_ref, o_ref, sink_ref, k_pieces, v_pieces, mask, n_masked):
    tq = q_ref.shape[0]
    pairs = q_ref.shape[1] // LANES
    group = pairs // SWA_KV_HEADS
    stack = ATTN_STACK
    lane = lax.broadcasted_iota(jnp.int32, (tq, LANES), 1)
    low_head = lane < SWA_HD
    k_cat, v_cat = [], []
    for g in range(SWA_KV_HEADS):
        k_cat.append(jnp.concatenate(k_pieces[g][0] + k_pieces[g][1], axis=0))
        nk = k_cat[g].shape[0] // 2
        lane2 = lax.broadcasted_iota(jnp.int32, (nk, LANES), 1)
        ones_lo = jnp.where(lane2 < SWA_HD, 1.0, 0.0).astype(bf16)
        ones_hi = jnp.where(lane2 < SWA_HD, 0.0, 1.0).astype(bf16)
        v_cat.append(jnp.concatenate([
            jnp.concatenate([jnp.concatenate(v_pieces[g][0], axis=0), ones_lo], axis=1),
            jnp.concatenate([jnp.concatenate(v_pieces[g][1], axis=0), ones_hi], axis=1)], axis=0))
    units = [(u * stack) // group for u in range(pairs // stack)]

    def scores(u):
        q_stack = jnp.concatenate(
            [q_ref[:, (u * stack + jj) * LANES:(u * stack + jj + 1) * LANES] for jj in range(stack)], axis=0)
        return _nt(q_stack, k_cat[units[u]])

    s_next = scores(0)
    for u, g in enumerate(units):
        s_all = s_next
        if u + 1 < len(units):
            s_next = scores(u + 1)
        prob_rows, sink_rows = [], []
        for jj in range(stack):
            j = u * stack + jj
            probs, sink_terms = [], []
            for p in range(2):
                sp = s_all[jj * tq:(jj + 1) * tq, p * nk:(p + 1) * nk]
                if mask is not None:
                    sp = jnp.concatenate(
                        [jnp.where(mask, sp[:, :n_masked], NEG_INF), sp[:, n_masked:]], axis=1)
                sink = sink_ref[2 * j + p]
                m = jnp.maximum(jnp.max(sp, axis=-1, keepdims=True), sink)
                probs.append(jnp.exp(sp - m).astype(bf16))
                sink_terms.append(jnp.exp(sink - m))
            prob_rows.append(jnp.concatenate(probs, axis=1))
            sink_rows.append(jnp.where(low_head, sink_terms[0], sink_terms[1]))
        out_all = _nn(jnp.concatenate(prob_rows, axis=0), v_cat[g])
        for jj in range(stack):
            j = u * stack + jj
            out = out_all[jj * tq:(jj + 1) * tq]
            den = out[:, LANES:] + sink_rows[jj]
            o_ref[:, j * LANES:(j + 1) * LANES] = (out[:, :LANES] / den).astype(bf16)


def _swa_kernel(sink_ref, q_ref, kp_ref, ko_ref, kn_ref, kc_ref, vp_ref, vo_ref, vn_ref, vc_ref,
                o_ref, *, seq, nb, need_ctx):
    i = pl.program_id(1)
    blk = SWA_BLOCK

    @pl.when(i < nb)
    def _():
        qi = i * blk + lax.broadcasted_iota(jnp.int32, (blk, 3 * blk), 0)
        kj = (i - 1) * blk + lax.broadcasted_iota(jnp.int32, (blk, 3 * blk), 1)
        mask = (jnp.abs(kj - qi) <= SWA_WINDOW) & (kj >= 0) & (kj < seq)
        k_pieces = [[[r[2 * g + p] for r in (kp_ref, ko_ref, kn_ref, kc_ref)] for p in range(2)]
                    for g in range(SWA_KV_HEADS)]
        v_pieces = [[[r[2 * g + p] for r in (vp_ref, vo_ref, vn_ref, vc_ref)] for p in range(2)]
                    for g in range(SWA_KV_HEADS)]
        _attend(q_ref, o_ref, sink_ref, k_pieces, v_pieces, mask, 3 * blk)

    if need_ctx:
        @pl.when(i >= nb)
        def _():
            k_pieces = [[[kc_ref[2 * g + p]] for p in range(2)] for g in range(SWA_KV_HEADS)]
            v_pieces = [[[vc_ref[2 * g + p]] for p in range(2)] for g in range(SWA_KV_HEADS)]
            _attend(q_ref, o_ref, sink_ref, k_pieces, v_pieces, None, 0)


def _swa_attention(q, kx, vx, sinks, n_batch, seq, ctx_len, need_ctx):
    t_all, nq = q.shape
    blk = SWA_BLOCK
    nb = seq // blk
    ncb = ctx_len // blk
    lat_blocks = n_batch * nb
    out_rows = t_all if need_ctx else n_batch * seq
    smem = pl.BlockSpec(memory_space=pltpu.SMEM)

    def win(off):
        return pl.BlockSpec((4, blk, LANES),
                            lambda b, i: (0, b * nb + jnp.clip(i + off, 0, nb - 1), 0))

    def ctx_spec():
        return pl.BlockSpec((4, ctx_len, LANES), lambda b, i: (0, (n_batch * seq) // ctx_len + b, 0))

    def q_row(b, i):
        return (jnp.where(i < nb, b * nb + i, lat_blocks + b * ncb + i - nb), 0)

    return pl.pallas_call(
        functools.partial(_swa_kernel, seq=seq, nb=nb, need_ctx=need_ctx),
        grid=(n_batch, nb + ncb if need_ctx else nb),
        in_specs=[smem, pl.BlockSpec((blk, nq), q_row),
                  win(-1), win(0), win(1), ctx_spec(), win(-1), win(0), win(1), ctx_spec()],
        out_specs=pl.BlockSpec((blk, nq), q_row),
        out_shape=jax.ShapeDtypeStruct((out_rows, nq), bf16),
        compiler_params=pltpu.CompilerParams(dimension_semantics=("arbitrary", "arbitrary"),
                                             vmem_limit_bytes=VMEM_LIMIT),
        name="swa_attn",
    )(sinks, q, kx, kx, kx, kx, vx, vx, vx, vx)


def _post_kernel(*refs, d, gla, final, hidden, n_x, n_lat_tiles):
    x_refs, refs = refs[:n_x], refs[n_x:]
    if gla:
        of_ref, ob_ref, r_ref, on_ref = refs[:4]
        rest = refs[4:]
    else:
        a_ref = refs[0]
        rest = refs[1:]
    mod_ref, wo_ref, nw_ref, win_ref, wout_ref = rest[:5]
    rest = rest[5:]
    if final:
        nf_ref, out_ref, acc_ref = rest
    else:
        out_ref, acc_ref = rest

    if gla:
        o = of_ref[...] + ob_ref[...]
        hv = d // GLA_HEADS
        parts = []
        for hh in range(GLA_HEADS):
            seg = o[:, hh * hv:(hh + 1) * hv]
            parts.append(_rms(seg, on_ref[:, hh * hv:(hh + 1) * hv]))
        a = (jnp.concatenate(parts, axis=1) * r_ref[...].astype(f32)).astype(bf16)
    else:
        a = a_ref[...]
    _, _, g1 = _mod_slices(mod_ref, d, 0)
    sh2, sc2, g2 = _mod_slices(mod_ref, d, 3)
    x1 = _load_x(x_refs, n_lat_tiles) + g1 * _nn(a, wo_ref[...])
    hn = (_rms(x1, nw_ref[...]) * (1.0 + sc2) + sh2).astype(bf16)
    acc_ref[...] = jnp.zeros_like(acc_ref)

    def body(ci, carry):
        off = pl.multiple_of(ci * FFN_CHUNK, FFN_CHUNK)
        gate = _nn(hn, win_ref[:, pl.ds(off, FFN_CHUNK)])
        up = _nn(hn, win_ref[:, pl.ds(hidden + off, FFN_CHUNK)])
        act = (_silu(gate) * up).astype(bf16)
        acc_ref[...] += _nn(act, wout_ref[pl.ds(off, FFN_CHUNK), :])
        return carry

    lax.fori_loop(0, hidden // FFN_CHUNK, body, 0, unroll=True)
    x2 = x1 + g2 * acc_ref[...]
    if final:
        x2 = _rms(x2, nf_ref[...])
    out_ref[...] = x2


def _post(x_src, mixer_in, mod, w_o, norm_w, w_in, w_out, geo, gla, norm_final, n_tiles, o_norm=None):
    hidden, d = w_out.shape
    assert hidden % FFN_CHUNK == 0
    tm = ROW_TILE
    final = norm_final is not None
    row = lambda i: (i, 0)
    args, in_specs = _x_specs(x_src, geo[0])
    n_x = len(args)
    if gla:
        o_f, o_b, r = mixer_in
        args += [o_f, o_b, r, jnp.tile(o_norm, GLA_HEADS).reshape(1, d)]
        in_specs += [pl.BlockSpec((tm, d), row)] * 3 + [_const_spec((1, d))]
    else:
        args += [mixer_in]
        in_specs += [pl.BlockSpec((tm, d), row)]
    args += [mod, w_o.astype(bf16), norm_w.reshape(1, d), w_in.astype(bf16), w_out.astype(bf16)]
    in_specs += [pl.BlockSpec((1, 1, 6 * d), _group_map(*geo)), _const_spec((d, d)), _const_spec((1, d)),
                 _const_spec((d, 2 * hidden)), _const_spec((hidden, d))]
    if final:
        args += [norm_final.reshape(1, d)]
        in_specs += [_const_spec((1, d))]
    return pl.pallas_call(
        functools.partial(_post_kernel, d=d, gla=gla, final=final, hidden=hidden, n_x=n_x,
                          n_lat_tiles=geo[0]),
        grid=(n_tiles,),
        in_specs=in_specs,
        out_specs=pl.BlockSpec((tm, d), row),
        out_shape=jax.ShapeDtypeStruct((n_tiles * tm, d), f32),
        scratch_shapes=[pltpu.VMEM((tm, d), f32)],
        compiler_params=pltpu.CompilerParams(dimension_semantics=("arbitrary",),
                                             vmem_limit_bytes=VMEM_LIMIT),
        name="post_gla" if gla else "post_swa",
    )(*args)


def kernel(x, c, ctx, c_ctx, w_ada, b_ada, norm_mix, norm_ffn, gla_w_in, gla_w_gate_up, gla_b_gate_up,
           gla_w_o, gla_o_norm, swa_w_qkv, swa_b_qkv, swa_w_o, swa_sinks, ffn_w_in, ffn_w_out,
           norm_final):
    n_batch, seq, d = x.shape
    ctx_len = ctx.shape[1]
    depth = w_ada.shape[0]
    tm = ROW_TILE
    assert seq % tm == 0 and (n_batch * ctx_len) == tm
    n_lat_tiles = n_batch * seq // tm
    geo = (n_lat_tiles, seq // tm, n_batch)

    mod_all = _ada_mod(jnp.concatenate([c, c_ctx[None, :]], axis=0), w_ada, b_ada)
    t_all = n_batch * (seq + ctx_len)
    x_src = (x.reshape(n_batch * seq, d), ctx.reshape(n_batch * ctx_len, d))
    cos_l, sin_l = _rope_tables(seq, tm)

    for i in range(depth):
        last = i == depth - 1
        j = i // 2
        mod = mod_all[i].reshape(8, 1, 6 * d)
        n_tiles = n_lat_tiles if last else n_lat_tiles + 1
        nf = norm_final if last else None
        if i % 2 == 0:
            qk, r, vt, gates = _gla_proj(x_src, t_all, mod, norm_mix[i], gla_w_in[j], gla_w_gate_up[j],
                                         gla_b_gate_up[j], geo)
            o_f, o_b = _gla_scan(qk, vt, gates, n_batch, seq, ctx_len)
            x_src = _post(x_src, (o_f, o_b, r), mod, gla_w_o[j], norm_ffn[i], ffn_w_in[i], ffn_w_out[i],
                          geo, True, nf, n_tiles, o_norm=gla_o_norm[j])
        else:
            q, kx, vx = _swa_proj(x_src, mod, norm_mix[i], swa_w_qkv[j], swa_b_qkv[j], cos_l, sin_l, geo)
            attn = _swa_attention(q, kx, vx, swa_sinks[j], n_batch, seq, ctx_len, not last)
            x_src = _post(x_src, attn, mod, swa_w_o[j], norm_ffn[i], ffn_w_in[i], ffn_w_out[i],
                          geo, False, nf, n_tiles)
    return x_src.reshape(n_batch, seq, d)
```

```python
import functools

import jax
import jax.numpy as jnp
from jax import lax
from jax.experimental import pallas as pl
from jax.experimental.pallas import tpu as pltpu

f32 = jnp.float32
bf16 = jnp.bfloat16

EPS = 1e-6
NEG_INF = -1e30
GRID_W = 64
ROPE_THETA = 10000.0

GLA_HEADS = 4
GLA_GATE_RANK = 16
GLA_GATE_NORM = 16.0
GLA_CHUNK = 64
GLA_BLOCK = 256

SWA_HEADS = 16
SWA_KV_HEADS = 2
SWA_HD = 64
SWA_WINDOW = 128
SWA_BLOCK = 128
ATTN_STACK = 1

LANES = 128
FFN_CHUNK = 256
ROW_TILE = 512
SUB_TILES = 2
VMEM_LIMIT = 56 * 1024 * 1024


def _nt(a, b):
    return lax.dot_general(a, b, (((1,), (1,)), ((), ())), preferred_element_type=f32)


def _nn(a, b):
    return jnp.dot(a, b, preferred_element_type=f32)


def _silu(v):
    return v * (1.0 / (1.0 + jnp.exp(-v)))


def _rms(v, w):
    return v * lax.rsqrt(jnp.mean(v * v, axis=-1, keepdims=True) + EPS) * w


def _const_spec(shape):
    zeros = (0,) * len(shape)
    return pl.BlockSpec(shape, lambda *_: zeros, pipeline_mode=pl.Buffered(1))


def _layer_spec(stacked, layer):
    zeros = (0,) * (stacked.ndim - 1)
    return pl.BlockSpec((None,) + stacked.shape[1:], lambda *_: (layer,) + zeros, pipeline_mode=pl.Buffered(1))


def _ada_kernel(cb_ref, w_ref, b_ref, out_ref, *, n_rows, d_model, tn):
    ncol = tn // LANES

    def body(i, accs):
        d0 = pl.multiple_of(i * 8, 8)
        w8 = w_ref[0, pl.ds(d0, 8), :]
        new = []
        for r in range(n_rows):
            s8 = _silu(cb_ref[r, pl.ds(d0, 8), :])
            new.append(tuple(accs[r][j] + w8[:, j * LANES:(j + 1) * LANES] * s8 for j in range(ncol)))
        return tuple(new)

    init = tuple(tuple(jnp.zeros((8, LANES), f32) for _ in range(ncol)) for _ in range(n_rows))
    accs = lax.fori_loop(0, d_model // 8, body, init, unroll=8)
    rows = [jnp.concatenate([jnp.sum(a, axis=0, keepdims=True) for a in accs[r]], axis=1)
            for r in range(n_rows)]
    rows += [jnp.zeros((1, tn), f32)] * (8 - n_rows)
    out_ref[0] = jnp.concatenate(rows, axis=0) + b_ref[0]


def _ada_mod(cvecs, w_ada, b_ada):
    n_rows, d_model = cvecs.shape
    depth, _, e = w_ada.shape
    tn = 768
    cb = jnp.broadcast_to(cvecs[:, :, None], (n_rows, d_model, LANES))
    return pl.pallas_call(
        functools.partial(_ada_kernel, n_rows=n_rows, d_model=d_model, tn=tn),
        grid=(depth, e // tn),
        in_specs=[
            pl.BlockSpec((n_rows, d_model, LANES), lambda l, j: (0, 0, 0)),
            pl.BlockSpec((1, d_model, tn), lambda l, j: (l, 0, j)),
            pl.BlockSpec((1, 1, tn), lambda l, j: (l, 0, j)),
        ],
        out_specs=pl.BlockSpec((1, 8, tn), lambda l, j: (l, 0, j)),
        out_shape=jax.ShapeDtypeStruct((depth, 8, e), f32),
        compiler_params=pltpu.CompilerParams(dimension_semantics=("arbitrary", "arbitrary")),
        name="ada_mod",
    )(cb, w_ada, b_ada.reshape(depth, 1, e))


def _mod_slices(mod_ref, d, first):
    return tuple(mod_ref[0, :, (first + k) * d:(first + k + 1) * d] for k in range(3))


def _x_specs(x_src, n_lat_tiles):
    tm = ROW_TILE
    if isinstance(x_src, tuple):
        lat, ctx = x_src
        d = lat.shape[1]
        assert ctx.shape == (tm, d)
        return [lat, ctx], [pl.BlockSpec((tm, d), lambda i: (jnp.minimum(i, n_lat_tiles - 1), 0)),
                            pl.BlockSpec((tm, d), lambda i: (0, 0))]
    return [x_src], [pl.BlockSpec((tm, x_src.shape[1]), lambda i: (i, 0))]


def _load_x(x_refs, n_lat_tiles, rs=slice(None)):
    if len(x_refs) == 2:
        return jnp.where(pl.program_id(0) < n_lat_tiles, x_refs[0][rs, :], x_refs[1][rs, :])
    return x_refs[0][rs, :]


def _sub_tiles(rows):
    step = rows // SUB_TILES
    return [slice(s * step, (s + 1) * step) for s in range(SUB_TILES)]


def _group_map(n_lat_tiles, tiles_per_batch, n_batch):
    def index_map(i):
        return (jnp.where(i < n_lat_tiles, i // tiles_per_batch, n_batch), 0, 0)
    return index_map


def _gla_proj_kernel(*refs, d, dk, dv, heads, n_x, n_lat_tiles):
    x_refs, (mod_ref, nw_ref, win_ref, wvt_ref, wup_ref, bup_ref, qk_ref, r_ref, vt_ref, g_ref) = (
        refs[:n_x], refs[n_x:])
    sh, sc, _ = _mod_slices(mod_ref, d, 0)
    hk = dk // heads
    subs = _sub_tiles(r_ref.shape[0])
    hs = [(_rms(_load_x(x_refs, n_lat_tiles, rs), nw_ref[...]) * (1.0 + sc) + sh).astype(bf16) for rs in subs]
    for rs, h in zip(subs, hs):
        low = _nn(h, win_ref[:, 2 * dk + 2 * dv:]).astype(bf16)
        z = _nn(low, wup_ref[...]) + bup_ref[...]
        lg = (jnp.minimum(z, 0.0) - jnp.log(1.0 + jnp.exp(-jnp.abs(z)))) * (1.0 / GLA_GATE_NORM)
        for hh in range(2 * heads):
            g_ref[hh, rs, :] = lg[:, hh * hk:(hh + 1) * hk]
        r_ref[rs, :] = _silu(_nn(h, win_ref[:, 2 * dk + dv:2 * dk + 2 * dv])).astype(bf16)
        qk = _nn(h, win_ref[:, :2 * dk])
        q = qk[:, :dk] * (hk ** -0.5)
        for hh in range(heads):
            qk_ref[hh, rs, :] = q[:, hh * hk:(hh + 1) * hk].astype(bf16)
            qk_ref[heads + hh, rs, :] = qk[:, dk + hh * hk:dk + (hh + 1) * hk].astype(bf16)
        vt_ref[:, rs] = _nt(wvt_ref[...], h).astype(bf16)


def _gla_proj(x_src, t_all, mod, norm_w, w_in, w_in16, layer, w_gate_up, b_gate_up, geo):
    d = w_in.shape[0]
    dk, dv, heads = d // 2, d, GLA_HEADS
    hk = dk // heads
    rank = GLA_GATE_RANK
    tm = ROW_TILE
    wvt = w_in[:, 2 * dk:2 * dk + dv].T.astype(bf16)
    wup = jnp.zeros((2 * rank, 2 * dk), f32)
    wup = wup.at[:rank, :dk].set(w_gate_up[0]).at[rank:, dk:].set(w_gate_up[1]).astype(bf16)
    bup = b_gate_up.reshape(1, 2 * dk)
    x_args, x_specs = _x_specs(x_src, geo[0])
    kern = functools.partial(_gla_proj_kernel, d=d, dk=dk, dv=dv, heads=heads, n_x=len(x_args),
                             n_lat_tiles=geo[0])
    return pl.pallas_call(
        kern,
        grid=(t_all // tm,),
        in_specs=x_specs + [
            pl.BlockSpec((1, 1, 6 * d), _group_map(*geo)),
            _const_spec((1, d)),
            _layer_spec(w_in16, layer),
            _const_spec((dv, d)),
            _const_spec((2 * rank, 2 * dk)),
            _const_spec((1, 2 * dk)),
        ],
        out_specs=[
            pl.BlockSpec((2 * heads, tm, hk), lambda i: (0, i, 0)),
            pl.BlockSpec((tm, dv), lambda i: (i, 0)),
            pl.BlockSpec((dv, tm), lambda i: (0, i)),
            pl.BlockSpec((2 * heads, tm, hk), lambda i: (0, i, 0)),
        ],
        out_shape=[
            jax.ShapeDtypeStruct((2 * heads, t_all, hk), bf16),
            jax.ShapeDtypeStruct((t_all, dv), bf16),
            jax.ShapeDtypeStruct((dv, t_all), bf16),
            jax.ShapeDtypeStruct((2 * heads, t_all, hk), f32),
        ],
        compiler_params=pltpu.CompilerParams(dimension_semantics=("arbitrary",),
                                             vmem_limit_bytes=VMEM_LIMIT),
        name="gla_proj",
    )(*x_args, mod, norm_w.reshape(1, d), w_in16, wvt, wup, bup)


def _gla_block(q, k, vt, g, s_ref, o_ref, reverse):
    tb, hk = q.shape
    c = GLA_CHUNK
    half = tb // 2
    assert tb == 4 * c
    row = lax.broadcasted_iota(jnp.int32, (tb, tb), 0)
    col = lax.broadcasted_iota(jnp.int32, (tb, tb), 1)
    causal = ((row // c) == (col // c)) & ((col >= row) if reverse else (col <= row))
    cum = jnp.where(causal, 1.0, 0.0).astype(bf16)
    g_hi = g.astype(bf16)
    g_lo = (g - g_hi.astype(f32)).astype(bf16)
    b2 = _nn(cum, jnp.concatenate([g_hi, g_lo], axis=1))
    yield
    b = b2[:, :hk] + b2[:, hk:]
    order = (3, 2, 1, 0) if reverse else (0, 1, 2, 3)
    pos = {ch: s for s, ch in enumerate(order)}
    tot = []
    for ch in range(4):
        last = ch * c if reverse else ch * c + c - 1
        tot.append(b[last:last + 1, :])
    zero = jnp.zeros_like(tot[0])
    before = [sum((tot[order[s2]] for s2 in range(pos[ch])), zero) for ch in range(4)]
    after = [sum((tot[order[s2]] for s2 in range(pos[ch] + 1, 4)), zero) for ch in range(4)]

    def rows(vals):
        return jnp.concatenate([jnp.broadcast_to(v, (c, hk)) for v in vals], axis=0)

    b_tot = rows(tot)
    kf = k.astype(f32)
    qd = q.astype(f32) * jnp.exp(b)
    ki = kf * jnp.exp(-b)
    ke = kf * jnp.exp(b_tot - b)
    qd16, ki16, ke16 = qd.astype(bf16), ki.astype(bf16), ke.astype(bf16)
    yield

    lrow = lax.broadcasted_iota(jnp.int32, (half, half), 0)
    lcol = lax.broadcasted_iota(jnp.int32, (half, half), 1)
    m_same = ((lrow // c) == (lcol // c)) & ((lcol >= lrow) if reverse else (lcol <= lrow))
    m_next = ((lrow < c) & (lcol >= c)) if reverse else ((lrow >= c) & (lcol < c))
    xs = [_nt(qd16[lo:lo + half], jnp.concatenate([ki16[lo:lo + half], ke16[lo:lo + half]], axis=0))
          for lo in (0, half)]

    late, early = (0, half) if reverse else (half, 0)
    q_far = order[3] * c - late
    k_far = order[0] * c - early
    q_scale = [zero, zero]
    q_scale[q_far // c] = tot[order[2]]
    k_scale = [zero, zero]
    k_scale[k_far // c] = tot[order[1]]
    q2 = (qd[late:late + half] * jnp.exp(rows(q_scale))).astype(bf16)
    k2 = (ke[early:early + half] * jnp.exp(rows(k_scale))).astype(bf16)
    cross_f32 = _nt(q2, k2)
    yield
    cross = cross_f32.astype(bf16)

    diag = [jnp.where(m_same, x[:, :half], jnp.where(m_next, x[:, half:], 0.0)).astype(bf16) for x in xs]
    if reverse:
        p_early = diag[1]
        p_late = jnp.concatenate([diag[0], cross], axis=1)
    else:
        p_early = diag[0]
        p_late = jnp.concatenate([cross, diag[1]], axis=1)
    s = s_ref[...]
    inter = _nt((qd * jnp.exp(rows(before))).astype(bf16), s.astype(bf16))
    pv_early = _nt(p_early, vt[:, early:early + half])
    pv_late = _nt(p_late, vt)
    k_end = (ke * jnp.exp(rows(after))).astype(bf16)
    upd = _nn(vt, k_end)
    yield
    o_ref[early:early + half, :] = pv_early + inter[early:early + half]
    o_ref[late:late + half, :] = pv_late + inter[late:late + half]
    s_ref[...] = s * jnp.exp(before[order[3]] + tot[order[3]]) + upd


def _gla_scan_kernel(qkf_ref, vtf_ref, gf_ref, qkb_ref, vtb_ref, gb_ref, of_ref, ob_ref, s_ref, *, heads):
    @pl.when(pl.program_id(1) == 0)
    def _():
        s_ref[...] = jnp.zeros_like(s_ref)

    hv = vtf_ref.shape[0] // heads
    chains = []
    for hh in range(heads):
        cols = slice(hh * hv, (hh + 1) * hv)
        chains.append(_gla_block(qkf_ref[hh], qkf_ref[heads + hh], vtf_ref[cols, :], gf_ref[hh],
                                 s_ref.at[0, hh], of_ref.at[:, cols], False))
        chains.append(_gla_block(qkb_ref[hh], qkb_ref[heads + hh], vtb_ref[cols, :], gb_ref[hh],
                                 s_ref.at[1, hh], ob_ref.at[:, cols], True))
    while chains:
        alive = []
        for chain in chains:
            try:
                next(chain)
                alive.append(chain)
            except StopIteration:
                pass
        chains = alive


def _gla_scan(qk, vt, gates, n_batch, seq, ctx_len):
    heads = GLA_HEADS
    _, t_all, hk = qk.shape
    dv = vt.shape[0]
    hv = dv // heads
    tb = GLA_BLOCK
    assert ctx_len == tb and seq % tb == 0
    nlat = seq // tb
    ctx0 = n_batch * nlat

    def fwd_row(b, t):
        return jnp.where(t == 0, ctx0 + b, b * nlat + t - 1)

    def bwd_row(b, t):
        return jnp.where(t == 0, ctx0 + b, b * nlat + nlat - t)

    def specs(row_fn, direction):
        return [
            pl.BlockSpec((2 * heads, tb, hk), lambda b, t: (0, row_fn(b, t), 0)),
            pl.BlockSpec((dv, tb), lambda b, t: (0, row_fn(b, t))),
            pl.BlockSpec((heads, tb, hk), lambda b, t: (direction, row_fn(b, t), 0)),
        ]

    return pl.pallas_call(
        functools.partial(_gla_scan_kernel, heads=heads),
        grid=(n_batch, nlat + 1),
        in_specs=specs(fwd_row, 0) + specs(bwd_row, 1),
        out_specs=[
            pl.BlockSpec((tb, dv), lambda b, t: (fwd_row(b, t), 0)),
            pl.BlockSpec((tb, dv), lambda b, t: (bwd_row(b, t), 0)),
        ],
        out_shape=[jax.ShapeDtypeStruct((t_all, dv), f32)] * 2,
        scratch_shapes=[pltpu.VMEM((2, heads, hv, hk), f32)],
        compiler_params=pltpu.CompilerParams(dimension_semantics=("arbitrary", "arbitrary"),
                                             vmem_limit_bytes=VMEM_LIMIT),
        name="gla_scan",
    )(qk, vt, gates, qk, vt, gates)


def _swa_proj_kernel(x_ref, mod_ref, nw_ref, w_ref, b_ref, cos_ref, sin_ref,
                     q_ref, kx_ref, vx_ref, *, d, nq):
    sh, sc, _ = _mod_slices(mod_ref, d, 0)
    subs = _sub_tiles(x_ref.shape[0])
    hs = [(_rms(x_ref[rs, :], nw_ref[...]) * (1.0 + sc) + sh).astype(bf16) for rs in subs]
    half = SWA_HD // 2
    for rs, h in zip(subs, hs):
        qkv = _nn(h, w_ref[...]) + b_ref[...]
        cos = cos_ref[rs, :]
        sin = sin_ref[rs, :]
        lane = lax.broadcasted_iota(jnp.int32, cos.shape, 1)
        first_half = (lane % SWA_HD) < half
        low_head = lane < SWA_HD

        def rope(v):
            partner = jnp.where(first_half, pltpu.roll(v, LANES - half, 1), pltpu.roll(v, half, 1))
            return v * cos + partner * sin

        for j in range(nq // LANES):
            q_ref[rs, j * LANES:(j + 1) * LANES] = (
                rope(qkv[:, j * LANES:(j + 1) * LANES]) * (SWA_HD ** -0.5)).astype(bf16)
        k = rope(qkv[:, nq:nq + LANES])
        v = qkv[:, nq + LANES:nq + 2 * LANES]
        for src_val, dst in ((k, kx_ref), (v, vx_ref)):
            swapped = pltpu.roll(src_val, SWA_HD, 1)
            dst[0, rs, :] = jnp.where(low_head, src_val, 0.0).astype(bf16)
            dst[1, rs, :] = jnp.where(low_head, 0.0, swapped).astype(bf16)
            dst[2, rs, :] = jnp.where(low_head, swapped, 0.0).astype(bf16)
            dst[3, rs, :] = jnp.where(low_head, 0.0, src_val).astype(bf16)


def _rope_tables(seq, extra_rows):
    rows = seq // GRID_W
    row = jnp.broadcast_to(jnp.arange(rows)[:, None], (rows, GRID_W)).reshape(-1).astype(f32)
    col = jnp.broadcast_to(jnp.arange(GRID_W)[None, :], (rows, GRID_W)).reshape(-1).astype(f32)
    n_freq = SWA_HD // 4
    inv = ROPE_THETA ** (-jnp.arange(n_freq, dtype=f32) / n_freq)
    ang = jnp.concatenate([row[:, None] * inv, col[:, None] * inv], axis=-1)
    cos, sin = jnp.cos(ang), jnp.sin(ang)
    reps = LANES // SWA_HD
    cos_l = jnp.tile(jnp.concatenate([cos, cos], axis=-1), (1, reps))
    sin_l = jnp.tile(jnp.concatenate([-sin, sin], axis=-1), (1, reps))
    cos_l = jnp.concatenate([cos_l, jnp.ones((extra_rows, LANES), f32)], axis=0)
    sin_l = jnp.concatenate([sin_l, jnp.zeros((extra_rows, LANES), f32)], axis=0)
    return cos_l, sin_l


def _swa_proj(x_all, mod, norm_w, w_qkv16, layer, b_qkv, cos_l, sin_l, geo):
    t_all, d = x_all.shape
    nq = SWA_HEADS * SWA_HD
    nqkv = w_qkv16.shape[2]
    tm = ROW_TILE
    n_lat_tiles, tiles_per_batch, _ = geo

    def pos_map(i):
        return (jnp.where(i < n_lat_tiles, i % tiles_per_batch, tiles_per_batch), 0)

    return pl.pallas_call(
        functools.partial(_swa_proj_kernel, d=d, nq=nq),
        grid=(t_all // tm,),
        in_specs=[
            pl.BlockSpec((tm, d), lambda i: (i, 0)),
            pl.BlockSpec((1, 1, 6 * d), _group_map(*geo)),
            _const_spec((1, d)),
            _layer_spec(w_qkv16, layer),
            _const_spec((1, nqkv)),
            pl.BlockSpec((tm, LANES), pos_map),
            pl.BlockSpec((tm, LANES), pos_map),
        ],
        out_specs=[
            pl.BlockSpec((tm, nq), lambda i: (i, 0)),
            pl.BlockSpec((4, tm, LANES), lambda i: (0, i, 0)),
            pl.BlockSpec((4, tm, LANES), lambda i: (0, i, 0)),
        ],
        out_shape=[
            jax.ShapeDtypeStruct((t_all, nq), bf16),
            jax.ShapeDtypeStruct((4, t_all, LANES), bf16),
            jax.ShapeDtypeStruct((4, t_all, LANES), bf16),
        ],
        compiler_params=pltpu.CompilerParams(dimension_semantics=("arbitrary",),
                                             vmem_limit_bytes=VMEM_LIMIT),
        name="swa_proj",
    )(x_all, mod, norm_w.reshape(1, d), w_qkv16, b_qkv.reshape(1, nqkv), cos_l, sin_l)


def _attend(q_ref, o_ref, sink_ref, k_pieces, v_pieces, mask, n_masked):
    tq = q_ref.shape[0]
    pairs = q_ref.shape[1] // LANES
    group = pairs // SWA_KV_HEADS
    stack = ATTN_STACK
    lane = lax.broadcasted_iota(jnp.int32, (tq, LANES), 1)
    low_head = lane < SWA_HD
    k_cat, v_cat = [], []
    for g in range(SWA_KV_HEADS):
        k_cat.append(jnp.concatenate(k_pieces[g][0] + k_pieces[g][1], axis=0))
        nk = k_cat[g].shape[0] // 2
        lane2 = lax.broadcasted_iota(jnp.int32, (nk, LANES), 1)
        ones_lo = jnp.where(lane2 < SWA_HD, 1.0, 0.0).astype(bf16)
        ones_hi = jnp.where(lane2 < SWA_HD, 0.0, 1.0).astype(bf16)
        v_cat.append(jnp.concatenate([
            jnp.concatenate([jnp.concatenate(v_pieces[g][0], axis=0), ones_lo], axis=1),
            jnp.concatenate([jnp.concatenate(v_pieces[g][1], axis=0), ones_hi], axis=1)], axis=0))
    units = [(u * stack) // group for u in range(pairs // stack)]

    def scores(u):
        q_stack = jnp.concatenate(
            [q_ref[:, (u * stack + jj) * LANES:(u * stack + jj + 1) * LANES] for jj in range(stack)], axis=0)
        return _nt(q_stack, k_cat[units[u]])

    s_next = scores(0)
    for u, g in enumerate(units):
        s_all = s_next
        if u + 1 < len(units):
            s_next = scores(u + 1)
        prob_rows, sink_rows = [], []
        for jj in range(stack):
            j = u * stack + jj
            probs, sink_terms = [], []
            for p in range(2):
                sp = s_all[jj * tq:(jj + 1) * tq, p * nk:(p + 1) * nk]
                if mask is not None:
                    sp = jnp.concatenate(
                        [jnp.where(mask, sp[:, :n_masked], NEG_INF), sp[:, n_masked:]], axis=1)
                sink = sink_ref[2 * j + p]
                m = jnp.maximum(jnp.max(sp, axis=-1, keepdims=True), sink)
                probs.append(jnp.exp(sp - m).astype(bf16))
                sink_terms.append(jnp.exp(sink - m))
            prob_rows.append(jnp.concatenate(probs, axis=1))
            sink_rows.append(jnp.where(low_head, sink_terms[0], sink_terms[1]))
        out_all = _nn(jnp.concatenate(prob_rows, axis=0), v_cat[g])
        for jj in range(stack):
            j = u * stack + jj
            out = out_all[jj * tq:(jj + 1) * tq]
            den = out[:, LANES:] + sink_rows[jj]
            o_ref[:, j * LANES:(j + 1) * LANES] = (out[:, :LANES] / den).astype(bf16)


def _swa_kernel(sink_ref, q_ref, kp_ref, ko_ref, kn_ref, kc_ref, vp_ref, vo_ref, vn_ref, vc_ref,
                o_ref, *, seq, n_lat_steps, need_ctx):
    i = pl.program_id(1)
    blk = SWA_BLOCK

    def pieces(refs_of_half):
        return [[[load(2 * g + p) for load in refs_of_half] for p in range(2)] for g in range(SWA_KV_HEADS)]

    def rows(ref, lo):
        return lambda idx: ref[idx, lo:lo + blk, :]

    @pl.when(i < n_lat_steps)
    def _():
        for half in range(2):
            first = 2 * i + half
            qi = first * blk + lax.broadcasted_iota(jnp.int32, (blk, 3 * blk), 0)
            kj = (first - 1) * blk + lax.broadcasted_iota(jnp.int32, (blk, 3 * blk), 1)
            mask = (jnp.abs(kj - qi) <= SWA_WINDOW) & (kj >= 0) & (kj < seq)
            if half == 0:
                k_loads = (rows(kp_ref, 0), rows(ko_ref, 0), rows(ko_ref, blk), lambda idx: kc_ref[idx])
                v_loads = (rows(vp_ref, 0), rows(vo_ref, 0), rows(vo_ref, blk), lambda idx: vc_ref[idx])
            else:
                k_loads = (rows(ko_ref, 0), rows(ko_ref, blk), rows(kn_ref, 0), lambda idx: kc_ref[idx])
                v_loads = (rows(vo_ref, 0), rows(vo_ref, blk), rows(vn_ref, 0), lambda idx: vc_ref[idx])
            rs = slice(half * blk, (half + 1) * blk)
            _attend(q_ref.at[rs, :], o_ref.at[rs, :], sink_ref, pieces(k_loads), pieces(v_loads), mask, 3 * blk)

    if need_ctx:
        @pl.when(i >= n_lat_steps)
        def _():
            for half in range(2):
                rs = slice(half * blk, (half + 1) * blk)
                _attend(q_ref.at[rs, :], o_ref.at[rs, :], sink_ref, pieces((lambda idx: kc_ref[idx],)),
                        pieces((lambda idx: vc_ref[idx],)), None, 0)


def _swa_attention(q, kx, vx, sinks, n_batch, seq, ctx_len, need_ctx):
    t_all, nq = q.shape
    blk = SWA_BLOCK
    step = 2 * blk
    assert seq % step == 0 and ctx_len % step == 0
    nb = seq // blk
    n_lat_steps = seq // step
    n_ctx_steps = ctx_len // step
    out_rows = t_all if need_ctx else n_batch * seq
    smem = pl.BlockSpec(memory_space=pltpu.SMEM)

    def edge(off):
        return pl.BlockSpec((4, blk, LANES),
                            lambda b, i: (0, b * nb + jnp.clip(2 * i + off, 0, nb - 1), 0))

    def own():
        return pl.BlockSpec((4, step, LANES),
                            lambda b, i: (0, b * n_lat_steps + jnp.minimum(i, n_lat_steps - 1), 0))

    def ctx_spec():
        return pl.BlockSpec((4, ctx_len, LANES), lambda b, i: (0, (n_batch * seq) // ctx_len + b, 0))

    def q_row(b, i):
        return (jnp.where(i < n_lat_steps, b * n_lat_steps + i,
                          n_batch * n_lat_steps + b * n_ctx_steps + i - n_lat_steps), 0)

    return pl.pallas_call(
        functools.partial(_swa_kernel, seq=seq, n_lat_steps=n_lat_steps, need_ctx=need_ctx),
        grid=(n_batch, n_lat_steps + n_ctx_steps if need_ctx else n_lat_steps),
        in_specs=[smem, pl.BlockSpec((step, nq), q_row),
                  edge(-1), own(), edge(2), ctx_spec(), edge(-1), own(), edge(2), ctx_spec()],
        out_specs=pl.BlockSpec((step, nq), q_row),
        out_shape=jax.ShapeDtypeStruct((out_rows, nq), bf16),
        compiler_params=pltpu.CompilerParams(dimension_semantics=("arbitrary", "arbitrary"),
                                             vmem_limit_bytes=VMEM_LIMIT),
        name="swa_attn",
    )(sinks, q, kx, kx, kx, kx, vx, vx, vx, vx)


def _post_kernel(*refs, d, gla, final, hidden, n_x, n_lat_tiles):
    x_refs, refs = refs[:n_x], refs[n_x:]
    if gla:
        of_ref, ob_ref, r_ref, on_ref = refs[:4]
        rest = refs[4:]
    else:
        a_ref = refs[0]
        rest = refs[1:]
    mod_ref, wo_ref, nw_ref, win_ref, wout_ref = rest[:5]
    rest = rest[5:]
    if final:
        nf_ref, out_ref, acc_ref = rest
    else:
        out_ref, acc_ref = rest

    if gla:
        o = of_ref[...] + ob_ref[...]
        hv = d // GLA_HEADS
        parts = []
        for hh in range(GLA_HEADS):
            seg = o[:, hh * hv:(hh + 1) * hv]
            parts.append(_rms(seg, on_ref[:, hh * hv:(hh + 1) * hv]))
        a = (jnp.concatenate(parts, axis=1) * r_ref[...].astype(f32)).astype(bf16)
    else:
        a = a_ref[...]
    _, _, g1 = _mod_slices(mod_ref, d, 0)
    sh2, sc2, g2 = _mod_slices(mod_ref, d, 3)
    x1 = _load_x(x_refs, n_lat_tiles) + g1 * _nn(a, wo_ref[...])
    hn = (_rms(x1, nw_ref[...]) * (1.0 + sc2) + sh2).astype(bf16)
    acc_ref[...] = jnp.zeros_like(acc_ref)

    def body(ci, carry):
        off = pl.multiple_of(ci * FFN_CHUNK, FFN_CHUNK)
        gate = _nn(hn, win_ref[:, pl.ds(off, FFN_CHUNK)])
        up = _nn(hn, win_ref[:, pl.ds(hidden + off, FFN_CHUNK)])
        act = (_silu(gate) * up).astype(bf16)
        acc_ref[...] += _nn(act, wout_ref[pl.ds(off, FFN_CHUNK), :])
        return carry

    lax.fori_loop(0, hidden // FFN_CHUNK, body, 0, unroll=True)
    x2 = x1 + g2 * acc_ref[...]
    if final:
        x2 = _rms(x2, nf_ref[...])
    out_ref[...] = x2


def _post(x_src, mixer_in, mod, w_o16, mix_layer, norm_w, w_in16, w_out16, layer, geo, gla, norm_final,
          n_tiles, o_norm=None):
    _, hidden, d = w_out16.shape
    assert hidden % FFN_CHUNK == 0
    tm = ROW_TILE
    final = norm_final is not None
    row = lambda i: (i, 0)
    args, in_specs = _x_specs(x_src, geo[0])
    n_x = len(args)
    if gla:
        o_f, o_b, r = mixer_in
        args += [o_f, o_b, r, jnp.tile(o_norm, GLA_HEADS).reshape(1, d)]
        in_specs += [pl.BlockSpec((tm, d), row)] * 3 + [_const_spec((1, d))]
    else:
        args += [mixer_in]
        in_specs += [pl.BlockSpec((tm, d), row)]
    args += [mod, w_o16, norm_w.reshape(1, d), w_in16, w_out16]
    in_specs += [pl.BlockSpec((1, 1, 6 * d), _group_map(*geo)), _layer_spec(w_o16, mix_layer), _const_spec((1, d)),
                 _layer_spec(w_in16, layer), _layer_spec(w_out16, layer)]
    if final:
        args += [norm_final.reshape(1, d)]
        in_specs += [_const_spec((1, d))]
    return pl.pallas_call(
        functools.partial(_post_kernel, d=d, gla=gla, final=final, hidden=hidden, n_x=n_x,
                          n_lat_tiles=geo[0]),
        grid=(n_tiles,),
        in_specs=in_specs,
        out_specs=pl.BlockSpec((tm, d), row),
        out_shape=jax.ShapeDtypeStruct((n_tiles * tm, d), f32),
        scratch_shapes=[pltpu.VMEM((tm, d), f32)],
        compiler_params=pltpu.CompilerParams(dimension_semantics=("arbitrary",),
                                             vmem_limit_bytes=VMEM_LIMIT),
        name="post_gla" if gla else "post_swa",
    )(*args)


def kernel(x, c, ctx, c_ctx, w_ada, b_ada, norm_mix, norm_ffn, gla_w_in, gla_w_gate_up, gla_b_gate_up,
           gla_w_o, gla_o_norm, swa_w_qkv, swa_b_qkv, swa_w_o, swa_sinks, ffn_w_in, ffn_w_out,
           norm_final):
    n_batch, seq, d = x.shape
    ctx_len = ctx.shape[1]
    depth = w_ada.shape[0]
    tm = ROW_TILE
    assert seq % tm == 0 and (n_batch * ctx_len) == tm
    n_lat_tiles = n_batch * seq // tm
    geo = (n_lat_tiles, seq // tm, n_batch)

    mod_all = _ada_mod(jnp.concatenate([c, c_ctx[None, :]], axis=0), w_ada, b_ada)
    t_all = n_batch * (seq + ctx_len)
    x_src = (x.reshape(n_batch * seq, d), ctx.reshape(n_batch * ctx_len, d))
    cos_l, sin_l = _rope_tables(seq, tm)
    gla_w_in16, gla_w_o16 = gla_w_in.astype(bf16), gla_w_o.astype(bf16)
    swa_w_qkv16, swa_w_o16 = swa_w_qkv.astype(bf16), swa_w_o.astype(bf16)
    ffn_w_in16, ffn_w_out16 = ffn_w_in.astype(bf16), ffn_w_out.astype(bf16)

    for i in range(depth):
        last = i == depth - 1
        j = i // 2
        mod = mod_all[i].reshape(8, 1, 6 * d)
        n_tiles = n_lat_tiles if last else n_lat_tiles + 1
        nf = norm_final if last else None
        if i % 2 == 0:
            qk, r, vt, gates = _gla_proj(x_src, t_all, mod, norm_mix[i], gla_w_in[j], gla_w_in16, j,
                                         gla_w_gate_up[j], gla_b_gate_up[j], geo)
            o_f, o_b = _gla_scan(qk, vt, gates, n_batch, seq, ctx_len)
            x_src = _post(x_src, (o_f, o_b, r), mod, gla_w_o16, j, norm_ffn[i], ffn_w_in16, ffn_w_out16, i,
                          geo, True, nf, n_tiles, o_norm=gla_o_norm[j])
        else:
            q, kx, vx = _swa_proj(x_src, mod, norm_mix[i], swa_w_qkv16, j, swa_b_qkv[j], cos_l, sin_l, geo)
            attn = _swa_attention(q, kx, vx, swa_sinks[j], n_batch, seq, ctx_len, not last)
            x_src = _post(x_src, attn, mod, swa_w_o16, j, norm_ffn[i], ffn_w_in16, ffn_w_out16, i,
                          geo, False, nf, n_tiles)
    return x_src.reshape(n_batch, seq, d)
```

```python
import functools

import jax
import jax.numpy as jnp
from jax import lax
from jax.experimental import pallas as pl
from jax.experimental.pallas import tpu as pltpu

f32 = jnp.float32
bf16 = jnp.bfloat16

EPS = 1e-6
NEG_INF = -1e30
GRID_W = 64
ROPE_THETA = 10000.0

GLA_HEADS = 4
GLA_GATE_RANK = 16
GLA_GATE_NORM = 16.0
GLA_CHUNK = 64
GLA_BLOCK = 256
SCAN_SKEW_GROUP = 4

SWA_HEADS = 16
SWA_KV_HEADS = 2
SWA_HD = 64
SWA_WINDOW = 128
SWA_BLOCK = 128
ATTN_STACK = 1

ATTN_AHEAD = 2

LANES = 128
FFN_CHUNK = 256
ROW_TILE = 512
SUB_TILES = 2
VMEM_LIMIT = 56 * 1024 * 1024


def _nt(a, b):
    return lax.dot_general(a, b, (((1,), (1,)), ((), ())), preferred_element_type=f32)


def _nn(a, b):
    return jnp.dot(a, b, preferred_element_type=f32)


def _silu(v):
    return v * (1.0 / (1.0 + jnp.exp(-v)))


def _rms(v, w):
    return v * lax.rsqrt(jnp.mean(v * v, axis=-1, keepdims=True) + EPS) * w


def _const_spec(shape):
    zeros = (0,) * len(shape)
    return pl.BlockSpec(shape, lambda *_: zeros, pipeline_mode=pl.Buffered(1))


def _ada_kernel(cb_ref, w_ref, b_ref, out_ref, *, n_rows, d_model, tn):
    ncol = tn // LANES

    def body(i, accs):
        d0 = pl.multiple_of(i * 8, 8)
        w8 = w_ref[0, pl.ds(d0, 8), :]
        new = []
        for r in range(n_rows):
            s8 = _silu(cb_ref[r, pl.ds(d0, 8), :])
            new.append(tuple(accs[r][j] + w8[:, j * LANES:(j + 1) * LANES] * s8 for j in range(ncol)))
        return tuple(new)

    init = tuple(tuple(jnp.zeros((8, LANES), f32) for _ in range(ncol)) for _ in range(n_rows))
    accs = lax.fori_loop(0, d_model // 8, body, init, unroll=8)
    rows = [jnp.concatenate([jnp.sum(a, axis=0, keepdims=True) for a in accs[r]], axis=1)
            for r in range(n_rows)]
    rows += [jnp.zeros((1, tn), f32)] * (8 - n_rows)
    out_ref[0] = jnp.concatenate(rows, axis=0) + b_ref[0]


def _ada_mod(cvecs, w_ada, b_ada):
    n_rows, d_model = cvecs.shape
    depth, _, e = w_ada.shape
    tn = 1536
    cb = jnp.broadcast_to(cvecs[:, :, None], (n_rows, d_model, LANES))
    return pl.pallas_call(
        functools.partial(_ada_kernel, n_rows=n_rows, d_model=d_model, tn=tn),
        grid=(depth, e // tn),
        in_specs=[
            pl.BlockSpec((n_rows, d_model, LANES), lambda l, j: (0, 0, 0)),
            pl.BlockSpec((1, d_model, tn), lambda l, j: (l, 0, j)),
            pl.BlockSpec((1, 1, tn), lambda l, j: (l, 0, j)),
        ],
        out_specs=pl.BlockSpec((1, 8, tn), lambda l, j: (l, 0, j)),
        out_shape=jax.ShapeDtypeStruct((depth, 8, e), f32),
        compiler_params=pltpu.CompilerParams(dimension_semantics=("arbitrary", "arbitrary")),
        name="ada_mod",
    )(cb, w_ada, b_ada.reshape(depth, 1, e))


def _mod_slices(mod_ref, d, first):
    return tuple(mod_ref[0, :, (first + k) * d:(first + k + 1) * d] for k in range(3))


def _x_specs(x_src, n_lat_tiles):
    tm = ROW_TILE
    if isinstance(x_src, tuple):
        lat, ctx = x_src
        d = lat.shape[1]
        assert ctx.shape == (tm, d)
        return [lat, ctx], [pl.BlockSpec((tm, d), lambda i: (jnp.minimum(i, n_lat_tiles - 1), 0)),
                            pl.BlockSpec((tm, d), lambda i: (0, 0))]
    return [x_src], [pl.BlockSpec((tm, x_src.shape[1]), lambda i: (i, 0))]


def _load_x(x_refs, n_lat_tiles, rs=slice(None)):
    if len(x_refs) == 2:
        return jnp.where(pl.program_id(0) < n_lat_tiles, x_refs[0][rs, :], x_refs[1][rs, :])
    return x_refs[0][rs, :]


def _sub_tiles(rows):
    step = rows // SUB_TILES
    return [slice(s * step, (s + 1) * step) for s in range(SUB_TILES)]


def _group_map(n_lat_tiles, tiles_per_batch, n_batch):
    def index_map(i):
        return (jnp.where(i < n_lat_tiles, i // tiles_per_batch, n_batch), 0, 0)
    return index_map


def _gla_proj_kernel(*refs, d, dk, dv, heads, n_x, n_lat_tiles):
    x_refs, (mod_ref, nw_ref, win_ref, wvt_ref, wup_ref, bup_ref, qk_ref, r_ref, vt_ref, g_ref) = (
        refs[:n_x], refs[n_x:])
    sh, sc, _ = _mod_slices(mod_ref, d, 0)
    hk = dk // heads
    subs = _sub_tiles(r_ref.shape[0])
    hs = [(_rms(_load_x(x_refs, n_lat_tiles, rs), nw_ref[...]) * (1.0 + sc) + sh).astype(bf16) for rs in subs]
    for rs, h in zip(subs, hs):
        low = _nn(h, win_ref[:, 2 * dk + 2 * dv:]).astype(bf16)
        z = _nn(low, wup_ref[...]) + bup_ref[...]
        lg = (jnp.minimum(z, 0.0) - jnp.log(1.0 + jnp.exp(-jnp.abs(z)))) * (1.0 / GLA_GATE_NORM)
        for hh in range(2 * heads):
            g_ref[hh, rs, :] = lg[:, hh * hk:(hh + 1) * hk]
        r_ref[rs, :] = _silu(_nn(h, win_ref[:, 2 * dk + dv:2 * dk + 2 * dv])).astype(bf16)
        qk = _nn(h, win_ref[:, :2 * dk])
        q = qk[:, :dk] * (hk ** -0.5)
        for hh in range(heads):
            qk_ref[hh, rs, :] = q[:, hh * hk:(hh + 1) * hk].astype(bf16)
            qk_ref[heads + hh, rs, :] = qk[:, dk + hh * hk:dk + (hh + 1) * hk].astype(bf16)
        vt_ref[:, rs] = _nt(wvt_ref[...], h).astype(bf16)


def _gla_proj(x_src, t_all, mod, norm_w, w_in, w_in16, w_gate_up, b_gate_up, geo):
    d = w_in.shape[0]
    dk, dv, heads = d // 2, d, GLA_HEADS
    hk = dk // heads
    rank = GLA_GATE_RANK
    tm = ROW_TILE
    wvt = w_in[:, 2 * dk:2 * dk + dv].T.astype(bf16)
    wup = jnp.zeros((2 * rank, 2 * dk), f32)
    wup = wup.at[:rank, :dk].set(w_gate_up[0]).at[rank:, dk:].set(w_gate_up[1]).astype(bf16)
    bup = b_gate_up.reshape(1, 2 * dk)
    x_args, x_specs = _x_specs(x_src, geo[0])
    kern = functools.partial(_gla_proj_kernel, d=d, dk=dk, dv=dv, heads=heads, n_x=len(x_args),
                             n_lat_tiles=geo[0])
    return pl.pallas_call(
        kern,
        grid=(t_all // tm,),
        in_specs=x_specs + [
            pl.BlockSpec((1, 1, 6 * d), _group_map(*geo)),
            _const_spec((1, d)),
            _const_spec(w_in16.shape),
            _const_spec((dv, d)),
            _const_spec((2 * rank, 2 * dk)),
            _const_spec((1, 2 * dk)),
        ],
        out_specs=[
            pl.BlockSpec((2 * heads, tm, hk), lambda i: (0, i, 0)),
            pl.BlockSpec((tm, dv), lambda i: (i, 0)),
            pl.BlockSpec((dv, tm), lambda i: (0, i)),
            pl.BlockSpec((2 * heads, tm, hk), lambda i: (0, i, 0)),
        ],
        out_shape=[
            jax.ShapeDtypeStruct((2 * heads, t_all, hk), bf16),
            jax.ShapeDtypeStruct((t_all, dv), bf16),
            jax.ShapeDtypeStruct((dv, t_all), bf16),
            jax.ShapeDtypeStruct((2 * heads, t_all, hk), f32),
        ],
        compiler_params=pltpu.CompilerParams(dimension_semantics=("arbitrary",),
                                             vmem_limit_bytes=VMEM_LIMIT),
        name="gla_proj",
    )(*x_args, mod, norm_w.reshape(1, d), w_in16, wvt, wup, bup)


def _gla_block(q, k, vt, g, s_ref, o_ref, reverse):
    tb, hk = q.shape
    c = GLA_CHUNK
    half = tb // 2
    assert tb == 4 * c
    row = lax.broadcasted_iota(jnp.int32, (tb, tb), 0)
    col = lax.broadcasted_iota(jnp.int32, (tb, tb), 1)
    causal = ((row // c) == (col // c)) & ((col >= row) if reverse else (col <= row))
    cum = jnp.where(causal, 1.0, 0.0).astype(bf16)
    g_hi = g.astype(bf16)
    g_lo = (g - g_hi.astype(f32)).astype(bf16)
    b2 = _nn(cum, jnp.concatenate([g_hi, g_lo], axis=1))
    yield
    b = b2[:, :hk] + b2[:, hk:]
    order = (3, 2, 1, 0) if reverse else (0, 1, 2, 3)
    pos = {ch: s for s, ch in enumerate(order)}
    tot = []
    for ch in range(4):
        last = ch * c if reverse else ch * c + c - 1
        tot.append(b[last:last + 1, :])
    zero = jnp.zeros_like(tot[0])
    before = [sum((tot[order[s2]] for s2 in range(pos[ch])), zero) for ch in range(4)]
    after = [sum((tot[order[s2]] for s2 in range(pos[ch] + 1, 4)), zero) for ch in range(4)]

    def rows(vals):
        return jnp.concatenate([jnp.broadcast_to(v, (c, hk)) for v in vals], axis=0)

    b_tot = rows(tot)
    kf = k.astype(f32)
    qd = q.astype(f32) * jnp.exp(b)
    ki = kf * jnp.exp(-b)
    ke = kf * jnp.exp(b_tot - b)
    qd16, ki16, ke16 = qd.astype(bf16), ki.astype(bf16), ke.astype(bf16)
    yield

    lrow = lax.broadcasted_iota(jnp.int32, (half, half), 0)
    lcol = lax.broadcasted_iota(jnp.int32, (half, half), 1)
    m_same = ((lrow // c) == (lcol // c)) & ((lcol >= lrow) if reverse else (lcol <= lrow))
    m_next = ((lrow < c) & (lcol >= c)) if reverse else ((lrow >= c) & (lcol < c))
    xs = [_nt(qd16[lo:lo + half], jnp.concatenate([ki16[lo:lo + half], ke16[lo:lo + half]], axis=0))
          for lo in (0, half)]

    late, early = (0, half) if reverse else (half, 0)
    q_far = order[3] * c - late
    k_far = order[0] * c - early
    q_scale = [zero, zero]
    q_scale[q_far // c] = tot[order[2]]
    k_scale = [zero, zero]
    k_scale[k_far // c] = tot[order[1]]
    q2 = (qd[late:late + half] * jnp.exp(rows(q_scale))).astype(bf16)
    k2 = (ke[early:early + half] * jnp.exp(rows(k_scale))).astype(bf16)
    cross_f32 = _nt(q2, k2)
    yield
    cross = cross_f32.astype(bf16)

    diag = [jnp.where(m_same, x[:, :half], jnp.where(m_next, x[:, half:], 0.0)).astype(bf16) for x in xs]
    if reverse:
        p_early = diag[1]
        p_late = jnp.concatenate([diag[0], cross], axis=1)
    else:
        p_early = diag[0]
        p_late = jnp.concatenate([cross, diag[1]], axis=1)
    s = s_ref[...]
    inter = _nt((qd * jnp.exp(rows(before))).astype(bf16), s.astype(bf16))
    pv_early = _nt(p_early, vt[:, early:early + half])
    pv_late = _nt(p_late, vt)
    k_end = (ke * jnp.exp(rows(after))).astype(bf16)
    upd = _nn(vt, k_end)
    yield
    o_ref[early:early + half, :] = pv_early + inter[early:early + half]
    o_ref[late:late + half, :] = pv_late + inter[late:late + half]
    s_ref[...] = s * jnp.exp(before[order[3]] + tot[order[3]]) + upd


def _gla_scan_kernel(qkf_ref, vtf_ref, gf_ref, qkb_ref, vtb_ref, gb_ref, of_ref, ob_ref, s_ref, *, heads):
    @pl.when(pl.program_id(1) == 0)
    def _():
        s_ref[...] = jnp.zeros_like(s_ref)

    hv = vtf_ref.shape[0] // heads
    chains = []
    for hh in range(heads):
        cols = slice(hh * hv, (hh + 1) * hv)
        chains.append(_gla_block(qkf_ref[hh], qkf_ref[heads + hh], vtf_ref[cols, :], gf_ref[hh],
                                 s_ref.at[0, hh], of_ref.at[:, cols], False))
        chains.append(_gla_block(qkb_ref[hh], qkb_ref[heads + hh], vtb_ref[cols, :], gb_ref[hh],
                                 s_ref.at[1, hh], ob_ref.at[:, cols], True))
    groups = [chains[k:k + SCAN_SKEW_GROUP] for k in range(0, len(chains), SCAN_SKEW_GROUP)]
    tick = 0
    while any(groups):
        for k, group in enumerate(groups):
            if tick < k:
                continue
            alive = []
            for chain in group:
                try:
                    next(chain)
                    alive.append(chain)
                except StopIteration:
                    pass
            groups[k] = alive
        tick += 1


def _gla_scan(qk, vt, gates, n_batch, seq, ctx_len):
    heads = GLA_HEADS
    _, t_all, hk = qk.shape
    dv = vt.shape[0]
    hv = dv // heads
    tb = GLA_BLOCK
    assert ctx_len == tb and seq % tb == 0
    nlat = seq // tb
    ctx0 = n_batch * nlat

    def fwd_row(b, t):
        return jnp.where(t == 0, ctx0 + b, b * nlat + t - 1)

    def bwd_row(b, t):
        return jnp.where(t == 0, ctx0 + b, b * nlat + nlat - t)

    def specs(row_fn, direction):
        return [
            pl.BlockSpec((2 * heads, tb, hk), lambda b, t: (0, row_fn(b, t), 0)),
            pl.BlockSpec((dv, tb), lambda b, t: (0, row_fn(b, t))),
            pl.BlockSpec((heads, tb, hk), lambda b, t: (direction, row_fn(b, t), 0)),
        ]

    return pl.pallas_call(
        functools.partial(_gla_scan_kernel, heads=heads),
        grid=(n_batch, nlat + 1),
        in_specs=specs(fwd_row, 0) + specs(bwd_row, 1),
        out_specs=[
            pl.BlockSpec((tb, dv), lambda b, t: (fwd_row(b, t), 0)),
            pl.BlockSpec((tb, dv), lambda b, t: (bwd_row(b, t), 0)),
        ],
        out_shape=[jax.ShapeDtypeStruct((t_all, dv), f32)] * 2,
        scratch_shapes=[pltpu.VMEM((2, heads, hv, hk), f32)],
        compiler_params=pltpu.CompilerParams(dimension_semantics=("arbitrary", "arbitrary"),
                                             vmem_limit_bytes=VMEM_LIMIT),
        name="gla_scan",
    )(qk, vt, gates, qk, vt, gates)


def _swa_proj_kernel(x_ref, mod_ref, nw_ref, w_ref, b_ref, cos_ref, sin_ref,
                     q_ref, kx_ref, vx_ref, *, d, nq):
    sh, sc, _ = _mod_slices(mod_ref, d, 0)
    subs = _sub_tiles(x_ref.shape[0])
    hs = [(_rms(x_ref[rs, :], nw_ref[...]) * (1.0 + sc) + sh).astype(bf16) for rs in subs]
    half = SWA_HD // 2
    for rs, h in zip(subs, hs):
        qkv = _nn(h, w_ref[...]) + b_ref[...]
        cos = cos_ref[rs, :]
        sin = sin_ref[rs, :]
        lane = lax.broadcasted_iota(jnp.int32, cos.shape, 1)
        first_half = (lane % SWA_HD) < half
        low_head = lane < SWA_HD

        def rope(v):
            partner = jnp.where(first_half, pltpu.roll(v, LANES - half, 1), pltpu.roll(v, half, 1))
            return v * cos + partner * sin

        for j in range(nq // LANES):
            q_ref[rs, j * LANES:(j + 1) * LANES] = (
                rope(qkv[:, j * LANES:(j + 1) * LANES]) * (SWA_HD ** -0.5)).astype(bf16)
        k = rope(qkv[:, nq:nq + LANES])
        v = qkv[:, nq + LANES:nq + 2 * LANES]
        for src_val, dst in ((k, kx_ref), (v, vx_ref)):
            swapped = pltpu.roll(src_val, SWA_HD, 1)
            dst[0, rs, :] = jnp.where(low_head, src_val, 0.0).astype(bf16)
            dst[1, rs, :] = jnp.where(low_head, 0.0, swapped).astype(bf16)
            dst[2, rs, :] = jnp.where(low_head, swapped, 0.0).astype(bf16)
            dst[3, rs, :] = jnp.where(low_head, 0.0, src_val).astype(bf16)


def _rope_tables(seq, extra_rows):
    rows = seq // GRID_W
    row = jnp.broadcast_to(jnp.arange(rows)[:, None], (rows, GRID_W)).reshape(-1).astype(f32)
    col = jnp.broadcast_to(jnp.arange(GRID_W)[None, :], (rows, GRID_W)).reshape(-1).astype(f32)
    n_freq = SWA_HD // 4
    inv = ROPE_THETA ** (-jnp.arange(n_freq, dtype=f32) / n_freq)
    ang = jnp.concatenate([row[:, None] * inv, col[:, None] * inv], axis=-1)
    cos, sin = jnp.cos(ang), jnp.sin(ang)
    reps = LANES // SWA_HD
    cos_l = jnp.tile(jnp.concatenate([cos, cos], axis=-1), (1, reps))
    sin_l = jnp.tile(jnp.concatenate([-sin, sin], axis=-1), (1, reps))
    cos_l = jnp.concatenate([cos_l, jnp.ones((extra_rows, LANES), f32)], axis=0)
    sin_l = jnp.concatenate([sin_l, jnp.zeros((extra_rows, LANES), f32)], axis=0)
    return cos_l, sin_l


def _swa_proj(x_all, mod, norm_w, w_qkv16, b_qkv, cos_l, sin_l, geo):
    t_all, d = x_all.shape
    nq = SWA_HEADS * SWA_HD
    nqkv = w_qkv16.shape[1]
    tm = ROW_TILE
    n_lat_tiles, tiles_per_batch, _ = geo

    def pos_map(i):
        return (jnp.where(i < n_lat_tiles, i % tiles_per_batch, tiles_per_batch), 0)

    return pl.pallas_call(
        functools.partial(_swa_proj_kernel, d=d, nq=nq),
        grid=(t_all // tm,),
        in_specs=[
            pl.BlockSpec((tm, d), lambda i: (i, 0)),
            pl.BlockSpec((1, 1, 6 * d), _group_map(*geo)),
            _const_spec((1, d)),
            _const_spec(w_qkv16.shape),
            _const_spec((1, nqkv)),
            pl.BlockSpec((tm, LANES), pos_map),
            pl.BlockSpec((tm, LANES), pos_map),
        ],
        out_specs=[
            pl.BlockSpec((tm, nq), lambda i: (i, 0)),
            pl.BlockSpec((4, tm, LANES), lambda i: (0, i, 0)),
            pl.BlockSpec((4, tm, LANES), lambda i: (0, i, 0)),
        ],
        out_shape=[
            jax.ShapeDtypeStruct((t_all, nq), bf16),
            jax.ShapeDtypeStruct((4, t_all, LANES), bf16),
            jax.ShapeDtypeStruct((4, t_all, LANES), bf16),
        ],
        compiler_params=pltpu.CompilerParams(dimension_semantics=("arbitrary",),
                                             vmem_limit_bytes=VMEM_LIMIT),
        name="swa_proj",
    )(x_all, mod, norm_w.reshape(1, d), w_qkv16, b_qkv.reshape(1, nqkv), cos_l, sin_l)


def _attend(q_ref, o_ref, sink_ref, k_pieces, v_pieces, mask, n_masked):
    tq = q_ref.shape[0]
    pairs = q_ref.shape[1] // LANES
    group = pairs // SWA_KV_HEADS
    stack = ATTN_STACK
    lane = lax.broadcasted_iota(jnp.int32, (tq, LANES), 1)
    low_head = lane < SWA_HD
    k_cat, v_cat = [], []
    for g in range(SWA_KV_HEADS):
        k_cat.append(jnp.concatenate(k_pieces[g][0] + k_pieces[g][1], axis=0))
        nk = k_cat[g].shape[0] // 2
        lane2 = lax.broadcasted_iota(jnp.int32, (nk, LANES), 1)
        ones_lo = jnp.where(lane2 < SWA_HD, 1.0, 0.0).astype(bf16)
        ones_hi = jnp.where(lane2 < SWA_HD, 0.0, 1.0).astype(bf16)
        v_cat.append(jnp.concatenate([
            jnp.concatenate([jnp.concatenate(v_pieces[g][0], axis=0), ones_lo], axis=1),
            jnp.concatenate([jnp.concatenate(v_pieces[g][1], axis=0), ones_hi], axis=1)], axis=0))
    units = [(u * stack) // group for u in range(pairs // stack)]

    def scores(u):
        q_stack = jnp.concatenate(
            [q_ref[:, (u * stack + jj) * LANES:(u * stack + jj + 1) * LANES] for jj in range(stack)], axis=0)
        return _nt(q_stack, k_cat[units[u]])

    ahead = [scores(u) for u in range(min(ATTN_AHEAD, len(units)))]
    for u, g in enumerate(units):
        s_all = ahead.pop(0)
        if u + ATTN_AHEAD < len(units):
            ahead.append(scores(u + ATTN_AHEAD))
        prob_rows, sink_rows = [], []
        for jj in range(stack):
            j = u * stack + jj
            probs, sink_terms = [], []
            for p in range(2):
                sp = s_all[jj * tq:(jj + 1) * tq, p * nk:(p + 1) * nk]
                if mask is not None:
                    sp = jnp.concatenate(
                        [jnp.where(mask, sp[:, :n_masked], NEG_INF), sp[:, n_masked:]], axis=1)
                sink = sink_ref[2 * j + p]
                m = jnp.maximum(jnp.max(sp, axis=-1, keepdims=True), sink)
                probs.append(jnp.exp(sp - m).astype(bf16))
                sink_terms.append(jnp.exp(sink - m))
            prob_rows.append(jnp.concatenate(probs, axis=1))
            sink_rows.append(jnp.where(low_head, sink_terms[0], sink_terms[1]))
        out_all = _nn(jnp.concatenate(prob_rows, axis=0), v_cat[g])
        for jj in range(stack):
            j = u * stack + jj
            out = out_all[jj * tq:(jj + 1) * tq]
            den = out[:, LANES:] + sink_rows[jj]
            o_ref[:, j * LANES:(j + 1) * LANES] = (out[:, :LANES] / den).astype(bf16)
        yield


def _trace_in_order(chains):
    for chain in chains:
        for _ in chain:
            pass


def _swa_kernel(sink_ref, q_ref, kp_ref, ko_ref, kn_ref, kc_ref, vp_ref, vo_ref, vn_ref, vc_ref,
                o_ref, *, seq, n_lat_steps, need_ctx):
    i = pl.program_id(1)
    blk = SWA_BLOCK

    def pieces(refs_of_half):
        return [[[load(2 * g + p) for load in refs_of_half] for p in range(2)] for g in range(SWA_KV_HEADS)]

    def rows(ref, lo):
        return lambda idx: ref[idx, lo:lo + blk, :]

    @pl.when(i < n_lat_steps)
    def _():
        halves = []
        for half in range(2):
            first = 2 * i + half
            qi = first * blk + lax.broadcasted_iota(jnp.int32, (blk, 3 * blk), 0)
            kj = (first - 1) * blk + lax.broadcasted_iota(jnp.int32, (blk, 3 * blk), 1)
            mask = (jnp.abs(kj - qi) <= SWA_WINDOW) & (kj >= 0) & (kj < seq)
            if half == 0:
                k_loads = (rows(kp_ref, 0), rows(ko_ref, 0), rows(ko_ref, blk), lambda idx: kc_ref[idx])
                v_loads = (rows(vp_ref, 0), rows(vo_ref, 0), rows(vo_ref, blk), lambda idx: vc_ref[idx])
            else:
                k_loads = (rows(ko_ref, 0), rows(ko_ref, blk), rows(kn_ref, 0), lambda idx: kc_ref[idx])
                v_loads = (rows(vo_ref, 0), rows(vo_ref, blk), rows(vn_ref, 0), lambda idx: vc_ref[idx])
            rs = slice(half * blk, (half + 1) * blk)
            halves.append(_attend(q_ref.at[rs, :], o_ref.at[rs, :], sink_ref, pieces(k_loads), pieces(v_loads),
                                  mask, 3 * blk))
        _trace_in_order(halves)

    if need_ctx:
        @pl.when(i >= n_lat_steps)
        def _():
            halves = []
            for half in range(2):
                rs = slice(half * blk, (half + 1) * blk)
                halves.append(_attend(q_ref.at[rs, :], o_ref.at[rs, :], sink_ref, pieces((lambda idx: kc_ref[idx],)),
                                      pieces((lambda idx: vc_ref[idx],)), None, 0))
            _trace_in_order(halves)


def _swa_attention(q, kx, vx, sinks, n_batch, seq, ctx_len, need_ctx):
    t_all, nq = q.shape
    blk = SWA_BLOCK
    step = 2 * blk
    assert seq % step == 0 and ctx_len % step == 0
    nb = seq // blk
    n_lat_steps = seq // step
    n_ctx_steps = ctx_len // step
    out_rows = t_all if need_ctx else n_batch * seq
    smem = pl.BlockSpec(memory_space=pltpu.SMEM)

    def edge(off):
        return pl.BlockSpec((4, blk, LANES),
                            lambda b, i: (0, b * nb + jnp.clip(2 * i + off, 0, nb - 1), 0))

    def own():
        return pl.BlockSpec((4, step, LANES),
                            lambda b, i: (0, b * n_lat_steps + jnp.minimum(i, n_lat_steps - 1), 0))

    def ctx_spec():
        return pl.BlockSpec((4, ctx_len, LANES), lambda b, i: (0, (n_batch * seq) // ctx_len + b, 0))

    def q_row(b, i):
        return (jnp.where(i < n_lat_steps, b * n_lat_steps + i,
                          n_batch * n_lat_steps + b * n_ctx_steps + i - n_lat_steps), 0)

    return pl.pallas_call(
        functools.partial(_swa_kernel, seq=seq, n_lat_steps=n_lat_steps, need_ctx=need_ctx),
        grid=(n_batch, n_lat_steps + n_ctx_steps if need_ctx else n_lat_steps),
        in_specs=[smem, pl.BlockSpec((step, nq), q_row),
                  edge(-1), own(), edge(2), ctx_spec(), edge(-1), own(), edge(2), ctx_spec()],
        out_specs=pl.BlockSpec((step, nq), q_row),
        out_shape=jax.ShapeDtypeStruct((out_rows, nq), bf16),
        compiler_params=pltpu.CompilerParams(dimension_semantics=("arbitrary", "arbitrary"),
                                             vmem_limit_bytes=VMEM_LIMIT),
        name="swa_attn",
    )(sinks, q, kx, kx, kx, kx, vx, vx, vx, vx)


def _post_kernel(*refs, d, gla, final, hidden, n_x, n_lat_tiles, n_cast):
    x_refs, refs = refs[:n_x], refs[n_x:]
    if gla:
        of_ref, ob_ref, r_ref, on_ref = refs[:4]
        rest = refs[4:]
    else:
        a_ref = refs[0]
        rest = refs[1:]
    mod_ref, wo_ref, nw_ref, win_ref, wout_ref = rest[:5]
    rest = rest[5:]
    if final:
        nf_ref, rest = rest[0], rest[1:]
    cast_in, out_ref, cast_out, acc_ref = rest[:n_cast], rest[n_cast], rest[n_cast + 1:-1], rest[-1]
    for src_ref, dst_ref in zip(cast_in, cast_out):
        dst_ref[...] = src_ref[...].astype(bf16)

    if gla:
        o = of_ref[...] + ob_ref[...]
        hv = d // GLA_HEADS
        parts = []
        for hh in range(GLA_HEADS):
            seg = o[:, hh * hv:(hh + 1) * hv]
            parts.append(_rms(seg, on_ref[:, hh * hv:(hh + 1) * hv]))
        a = (jnp.concatenate(parts, axis=1) * r_ref[...].astype(f32)).astype(bf16)
    else:
        a = a_ref[...]
    _, _, g1 = _mod_slices(mod_ref, d, 0)
    sh2, sc2, g2 = _mod_slices(mod_ref, d, 3)
    x1 = _load_x(x_refs, n_lat_tiles) + g1 * _nn(a, wo_ref[...])
    hn = (_rms(x1, nw_ref[...]) * (1.0 + sc2) + sh2).astype(bf16)
    acc_ref[...] = jnp.zeros_like(acc_ref)

    def body(ci, carry):
        off = pl.multiple_of(ci * FFN_CHUNK, FFN_CHUNK)
        gate = _nn(hn, win_ref[:, pl.ds(off, FFN_CHUNK)])
        up = _nn(hn, win_ref[:, pl.ds(hidden + off, FFN_CHUNK)])
        act = (_silu(gate) * up).astype(bf16)
        acc_ref[...] += _nn(act, wout_ref[pl.ds(off, FFN_CHUNK), :])
        return carry

    lax.fori_loop(0, hidden // FFN_CHUNK, body, 0, unroll=True)
    x2 = x1 + g2 * acc_ref[...]
    if final:
        x2 = _rms(x2, nf_ref[...])
    out_ref[...] = x2


CAST_BLOCKS = 16


def _post(x_src, mixer_in, mod, w_o16, norm_w, w_in16, w_out16, geo, gla, norm_final, n_tiles, o_norm=None,
          cast_jobs=()):
    hidden, d = w_out16.shape
    assert hidden % FFN_CHUNK == 0
    tm = ROW_TILE
    final = norm_final is not None
    row = lambda i: (i, 0)
    args, in_specs = _x_specs(x_src, geo[0])
    n_x = len(args)
    if gla:
        o_f, o_b, r = mixer_in
        args += [o_f, o_b, r, jnp.tile(o_norm, GLA_HEADS).reshape(1, d)]
        in_specs += [pl.BlockSpec((tm, d), row)] * 3 + [_const_spec((1, d))]
    else:
        args += [mixer_in]
        in_specs += [pl.BlockSpec((tm, d), row)]
    args += [mod, w_o16, norm_w.reshape(1, d), w_in16, w_out16]
    in_specs += [pl.BlockSpec((1, 1, 6 * d), _group_map(*geo)), _const_spec(w_o16.shape), _const_spec((1, d)),
                 _const_spec(w_in16.shape), _const_spec(w_out16.shape)]
    if final:
        args += [norm_final.reshape(1, d)]
        in_specs += [_const_spec((1, d))]
    out_specs = [pl.BlockSpec((tm, d), row)]
    out_shape = [jax.ShapeDtypeStruct((n_tiles * tm, d), f32)]
    n_blocks = CAST_BLOCKS
    while n_blocks > n_tiles:
        n_blocks //= 2
    last_block = n_blocks - 1
    for stacked, layer in cast_jobs:
        _, rows, cols = stacked.shape
        assert rows % (16 * n_blocks) == 0
        br = rows // n_blocks
        args.append(stacked)
        in_specs.append(pl.BlockSpec((None, br, cols),
                                     lambda i, layer=layer: (layer, jnp.minimum(i, last_block), 0)))
        out_specs.append(pl.BlockSpec((br, cols), lambda i: (jnp.minimum(i, last_block), 0)))
        out_shape.append(jax.ShapeDtypeStruct((rows, cols), bf16))
    outs = pl.pallas_call(
        functools.partial(_post_kernel, d=d, gla=gla, final=final, hidden=hidden, n_x=n_x,
                          n_lat_tiles=geo[0], n_cast=len(cast_jobs)),
        grid=(n_tiles,),
        in_specs=in_specs,
        out_specs=out_specs,
        out_shape=out_shape,
        scratch_shapes=[pltpu.VMEM((tm, d), f32)],
        compiler_params=pltpu.CompilerParams(dimension_semantics=("arbitrary",),
                                             vmem_limit_bytes=VMEM_LIMIT),
        name="post_gla" if gla else "post_swa",
    )(*args)
    return outs[0], list(outs[1:])


def kernel(x, c, ctx, c_ctx, w_ada, b_ada, norm_mix, norm_ffn, gla_w_in, gla_w_gate_up, gla_b_gate_up,
           gla_w_o, gla_o_norm, swa_w_qkv, swa_b_qkv, swa_w_o, swa_sinks, ffn_w_in, ffn_w_out,
           norm_final):
    n_batch, seq, d = x.shape
    ctx_len = ctx.shape[1]
    depth = w_ada.shape[0]
    tm = ROW_TILE
    assert seq % tm == 0 and (n_batch * ctx_len) == tm
    n_lat_tiles = n_batch * seq // tm
    geo = (n_lat_tiles, seq // tm, n_batch)

    mod_all = _ada_mod(jnp.concatenate([c, c_ctx[None, :]], axis=0), w_ada, b_ada)
    t_all = n_batch * (seq + ctx_len)
    x_src = (x.reshape(n_batch * seq, d), ctx.reshape(n_batch * ctx_len, d))
    cos_l, sin_l = _rope_tables(seq, tm)

    def mixer_jobs(layer):
        jj = layer // 2
        mix = [(gla_w_in, jj), (gla_w_o, jj)] if layer % 2 == 0 else [(swa_w_qkv, jj), (swa_w_o, jj)]
        return mix + [(ffn_w_in, layer), (ffn_w_out, layer)]

    w16 = [stacked[layer].astype(bf16) for stacked, layer in mixer_jobs(0)]
    for i in range(depth):
        last = i == depth - 1
        j = i // 2
        mod = mod_all[i].reshape(8, 1, 6 * d)
        n_tiles = n_lat_tiles if last else n_lat_tiles + 1
        nf = norm_final if last else None
        jobs = () if last else mixer_jobs(i + 1)
        w_mix16, w_o16, w_ffn_in16, w_ffn_out16 = w16
        if i % 2 == 0:
            qk, r, vt, gates = _gla_proj(x_src, t_all, mod, norm_mix[i], gla_w_in[j], w_mix16,
                                         gla_w_gate_up[j], gla_b_gate_up[j], geo)
            o_f, o_b = _gla_scan(qk, vt, gates, n_batch, seq, ctx_len)
            x_src, w16 = _post(x_src, (o_f, o_b, r), mod, w_o16, norm_ffn[i], w_ffn_in16, w_ffn_out16,
                               geo, True, nf, n_tiles, o_norm=gla_o_norm[j], cast_jobs=jobs)
        else:
            q, kx, vx = _swa_proj(x_src, mod, norm_mix[i], w_mix16, swa_b_qkv[j], cos_l, sin_l, geo)
            attn = _swa_attention(q, kx, vx, swa_sinks[j], n_batch, seq, ctx_len, not last)
            x_src, w16 = _post(x_src, attn, mod, w_o16, norm_ffn[i], w_ffn_in16, w_ffn_out16,
                               geo, False, nf, n_tiles, cast_jobs=jobs)
    return x_src.reshape(n_batch, seq, d)
```

```python
import functools

import jax
import jax.numpy as jnp
from jax import lax
from jax.experimental import pallas as pl
from jax.experimental.pallas import tpu as pltpu

f32 = jnp.float32
bf16 = jnp.bfloat16

EPS = 1e-6
NEG_INF = -1e30
GRID_W = 64
ROPE_THETA = 10000.0

GLA_HEADS = 4
GLA_GATE_RANK = 16
GLA_GATE_NORM = 16.0
GLA_CHUNK = 64
GLA_BLOCK = 256
SCAN_SKEW_GROUP = 4

SWA_HEADS = 16
SWA_KV_HEADS = 2
SWA_HD = 64
SWA_WINDOW = 128
SWA_BLOCK = 128
ATTN_STACK = 1

ATTN_AHEAD = 2

LANES = 128
FFN_CHUNK = 256
ROW_TILE = 512
SUB_TILES = 2
VMEM_LIMIT = 56 * 1024 * 1024


def _nt(a, b):
    return lax.dot_general(a, b, (((1,), (1,)), ((), ())), preferred_element_type=f32)


def _nn(a, b):
    return jnp.dot(a, b, preferred_element_type=f32)


def _silu(v):
    return v * (1.0 / (1.0 + jnp.exp(-v)))


def _rms(v, w):
    return v * lax.rsqrt(jnp.mean(v * v, axis=-1, keepdims=True) + EPS) * w


def _const_spec(shape):
    zeros = (0,) * len(shape)
    return pl.BlockSpec(shape, lambda *_: zeros, pipeline_mode=pl.Buffered(1))


def _ada_kernel(cb_ref, w_ref, b_ref, out_ref, *, n_rows, d_model, tn):
    ncol = tn // LANES

    def body(i, accs):
        d0 = pl.multiple_of(i * 8, 8)
        w8 = w_ref[0, pl.ds(d0, 8), :]
        new = []
        for r in range(n_rows):
            s8 = _silu(cb_ref[r, pl.ds(d0, 8), :])
            new.append(tuple(accs[r][j] + w8[:, j * LANES:(j + 1) * LANES] * s8 for j in range(ncol)))
        return tuple(new)

    init = tuple(tuple(jnp.zeros((8, LANES), f32) for _ in range(ncol)) for _ in range(n_rows))
    accs = lax.fori_loop(0, d_model // 8, body, init, unroll=8)
    rows = [jnp.concatenate([jnp.sum(a, axis=0, keepdims=True) for a in accs[r]], axis=1)
            for r in range(n_rows)]
    rows += [jnp.zeros((1, tn), f32)] * (8 - n_rows)
    out_ref[0] = jnp.concatenate(rows, axis=0) + b_ref[0]


def _ada_mod(cvecs, w_ada, b_ada):
    n_rows, d_model = cvecs.shape
    depth, _, e = w_ada.shape
    tn = 1536
    cb = jnp.broadcast_to(cvecs[:, :, None], (n_rows, d_model, LANES))
    return pl.pallas_call(
        functools.partial(_ada_kernel, n_rows=n_rows, d_model=d_model, tn=tn),
        grid=(depth, e // tn),
        in_specs=[
            pl.BlockSpec((n_rows, d_model, LANES), lambda l, j: (0, 0, 0)),
            pl.BlockSpec((1, d_model, tn), lambda l, j: (l, 0, j)),
            pl.BlockSpec((1, 1, tn), lambda l, j: (l, 0, j)),
        ],
        out_specs=pl.BlockSpec((1, 8, tn), lambda l, j: (l, 0, j)),
        out_shape=jax.ShapeDtypeStruct((depth, 8, e), f32),
        compiler_params=pltpu.CompilerParams(dimension_semantics=("arbitrary", "arbitrary")),
        name="ada_mod",
    )(cb, w_ada, b_ada.reshape(depth, 1, e))


def _mod_slices(mod_ref, d, first):
    return tuple(mod_ref[0, :, (first + k) * d:(first + k + 1) * d] for k in range(3))


def _x_specs(x_src, n_lat_tiles):
    tm = ROW_TILE
    if isinstance(x_src, tuple):
        lat, ctx = x_src
        d = lat.shape[1]
        assert ctx.shape == (tm, d)
        return [lat, ctx], [pl.BlockSpec((tm, d), lambda i: (jnp.minimum(i, n_lat_tiles - 1), 0)),
                            pl.BlockSpec((tm, d), lambda i: (0, 0))]
    return [x_src], [pl.BlockSpec((tm, x_src.shape[1]), lambda i: (i, 0))]


def _load_x(x_refs, n_lat_tiles, rs=slice(None)):
    if len(x_refs) == 2:
        return jnp.where(pl.program_id(0) < n_lat_tiles, x_refs[0][rs, :], x_refs[1][rs, :])
    return x_refs[0][rs, :]


def _sub_tiles(rows):
    step = rows // SUB_TILES
    return [slice(s * step, (s + 1) * step) for s in range(SUB_TILES)]


def _group_map(n_lat_tiles, tiles_per_batch, n_batch):
    def index_map(i):
        return (jnp.where(i < n_lat_tiles, i // tiles_per_batch, n_batch), 0, 0)
    return index_map


def _gla_proj_kernel(*refs, d, dk, dv, heads, n_x, n_lat_tiles):
    x_refs, (mod_ref, nw_ref, win_ref, qk_ref, r_ref, vt_ref, low_ref, wvt_ref) = refs[:n_x], refs[n_x:]

    @pl.when(pl.program_id(0) == 0)
    def _():
        wvt_ref[...] = win_ref[:, 2 * dk:2 * dk + dv].astype(f32).T.astype(bf16)

    sh, sc, _ = _mod_slices(mod_ref, d, 0)
    hk = dk // heads
    subs = _sub_tiles(r_ref.shape[0])
    hs = [(_rms(_load_x(x_refs, n_lat_tiles, rs), nw_ref[...]) * (1.0 + sc) + sh).astype(bf16) for rs in subs]
    for rs, h in zip(subs, hs):
        low_ref[rs, :] = _nn(h, win_ref[:, 2 * dk + 2 * dv:]).astype(bf16)
        r_ref[rs, :] = _silu(_nn(h, win_ref[:, 2 * dk + dv:2 * dk + 2 * dv])).astype(bf16)
        qk = _nn(h, win_ref[:, :2 * dk])
        q = qk[:, :dk] * (hk ** -0.5)
        for hh in range(heads):
            qk_ref[hh, rs, :] = q[:, hh * hk:(hh + 1) * hk].astype(bf16)
            qk_ref[heads + hh, rs, :] = qk[:, dk + hh * hk:dk + (hh + 1) * hk].astype(bf16)
        vt_ref[:, rs] = _nt(wvt_ref[...], h).astype(bf16)


def _gla_proj(x_src, t_all, mod, norm_w, w_in16, geo):
    d = w_in16.shape[0]
    dk, dv, heads = d // 2, d, GLA_HEADS
    hk = dk // heads
    tm = ROW_TILE
    x_args, x_specs = _x_specs(x_src, geo[0])
    kern = functools.partial(_gla_proj_kernel, d=d, dk=dk, dv=dv, heads=heads, n_x=len(x_args),
                             n_lat_tiles=geo[0])
    return pl.pallas_call(
        kern,
        grid=(t_all // tm,),
        in_specs=x_specs + [
            pl.BlockSpec((1, 1, 6 * d), _group_map(*geo)),
            _const_spec((1, d)),
            _const_spec(w_in16.shape),
        ],
        out_specs=[
            pl.BlockSpec((2 * heads, tm, hk), lambda i: (0, i, 0)),
            pl.BlockSpec((tm, dv), lambda i: (i, 0)),
            pl.BlockSpec((dv, tm), lambda i: (0, i)),
            pl.BlockSpec((tm, 2 * GLA_GATE_RANK), lambda i: (i, 0)),
        ],
        out_shape=[
            jax.ShapeDtypeStruct((2 * heads, t_all, hk), bf16),
            jax.ShapeDtypeStruct((t_all, dv), bf16),
            jax.ShapeDtypeStruct((dv, t_all), bf16),
            jax.ShapeDtypeStruct((t_all, 2 * GLA_GATE_RANK), bf16),
        ],
        scratch_shapes=[pltpu.VMEM((dv, d), bf16)],
        compiler_params=pltpu.CompilerParams(dimension_semantics=("arbitrary",),
                                             vmem_limit_bytes=VMEM_LIMIT),
        name="gla_proj",
    )(*x_args, mod, norm_w.reshape(1, d), w_in16)


def _gla_block(q, k, vt, g, s_ref, o_ref, reverse):
    tb, hk = q.shape
    c = GLA_CHUNK
    half = tb // 2
    assert tb == 4 * c
    row = lax.broadcasted_iota(jnp.int32, (tb, tb), 0)
    col = lax.broadcasted_iota(jnp.int32, (tb, tb), 1)
    causal = ((row // c) == (col // c)) & ((col >= row) if reverse else (col <= row))
    cum = jnp.where(causal, 1.0, 0.0).astype(bf16)
    g_hi = g.astype(bf16)
    g_lo = (g - g_hi.astype(f32)).astype(bf16)
    b2 = _nn(cum, jnp.concatenate([g_hi, g_lo], axis=1))
    yield
    b = b2[:, :hk] + b2[:, hk:]
    order = (3, 2, 1, 0) if reverse else (0, 1, 2, 3)
    pos = {ch: s for s, ch in enumerate(order)}
    tot = []
    for ch in range(4):
        last = ch * c if reverse else ch * c + c - 1
        tot.append(b[last:last + 1, :])
    zero = jnp.zeros_like(tot[0])
    before = [sum((tot[order[s2]] for s2 in range(pos[ch])), zero) for ch in range(4)]
    after = [sum((tot[order[s2]] for s2 in range(pos[ch] + 1, 4)), zero) for ch in range(4)]

    def rows(vals):
        return jnp.concatenate([jnp.broadcast_to(v, (c, hk)) for v in vals], axis=0)

    b_tot = rows(tot)
    kf = k.astype(f32)
    qd = q.astype(f32) * jnp.exp(b)
    ki = kf * jnp.exp(-b)
    ke = kf * jnp.exp(b_tot - b)
    qd16, ki16, ke16 = qd.astype(bf16), ki.astype(bf16), ke.astype(bf16)
    yield

    lrow = lax.broadcasted_iota(jnp.int32, (half, half), 0)
    lcol = lax.broadcasted_iota(jnp.int32, (half, half), 1)
    m_same = ((lrow // c) == (lcol // c)) & ((lcol >= lrow) if reverse else (lcol <= lrow))
    m_next = ((lrow < c) & (lcol >= c)) if reverse else ((lrow >= c) & (lcol < c))
    xs = [_nt(qd16[lo:lo + half], jnp.concatenate([ki16[lo:lo + half], ke16[lo:lo + half]], axis=0))
          for lo in (0, half)]

    late, early = (0, half) if reverse else (half, 0)
    q_far = order[3] * c - late
    k_far = order[0] * c - early
    q_scale = [zero, zero]
    q_scale[q_far // c] = tot[order[2]]
    k_scale = [zero, zero]
    k_scale[k_far // c] = tot[order[1]]
    q2 = (qd[late:late + half] * jnp.exp(rows(q_scale))).astype(bf16)
    k2 = (ke[early:early + half] * jnp.exp(rows(k_scale))).astype(bf16)
    cross_f32 = _nt(q2, k2)
    yield
    cross = cross_f32.astype(bf16)

    diag = [jnp.where(m_same, x[:, :half], jnp.where(m_next, x[:, half:], 0.0)).astype(bf16) for x in xs]
    if reverse:
        p_early = diag[1]
        p_late = jnp.concatenate([diag[0], cross], axis=1)
    else:
        p_early = diag[0]
        p_late = jnp.concatenate([cross, diag[1]], axis=1)
    s = s_ref[...]
    inter = _nt((qd * jnp.exp(rows(before))).astype(bf16), s.astype(bf16))
    pv_early = _nt(p_early, vt[:, early:early + half])
    pv_late = _nt(p_late, vt)
    k_end = (ke * jnp.exp(rows(after))).astype(bf16)
    upd = _nn(vt, k_end)
    yield
    o_ref[early:early + half, :] = (pv_early + inter[early:early + half]).astype(o_ref.dtype)
    o_ref[late:late + half, :] = (pv_late + inter[late:late + half]).astype(o_ref.dtype)
    s_ref[...] = s * jnp.exp(before[order[3]] + tot[order[3]]) + upd


def _gla_scan_kernel(qkf_ref, vtf_ref, lowf_ref, qkb_ref, vtb_ref, lowb_ref, wup_ref, bup_ref,
                     of_ref, ob_ref, s_ref, *, heads):
    @pl.when(pl.program_id(1) == 0)
    def _():
        s_ref[...] = jnp.zeros_like(s_ref)

    hv = vtf_ref.shape[0] // heads
    hk = qkf_ref.shape[2]
    dk = heads * hk

    def log_decay(low_ref, direction):
        cols = slice(direction * dk, (direction + 1) * dk)
        z = _nn(low_ref[...], wup_ref[:, cols]) + bup_ref[:, cols]
        return (jnp.minimum(z, 0.0) - jnp.log(1.0 + jnp.exp(-jnp.abs(z)))) * (1.0 / GLA_GATE_NORM)

    g_f = log_decay(lowf_ref, 0)
    g_b = log_decay(lowb_ref, 1)
    chains = []
    for hh in range(heads):
        cols = slice(hh * hv, (hh + 1) * hv)
        gcols = slice(hh * hk, (hh + 1) * hk)
        chains.append(_gla_block(qkf_ref[hh], qkf_ref[heads + hh], vtf_ref[cols, :], g_f[:, gcols],
                                 s_ref.at[0, hh], of_ref.at[:, cols], False))
        chains.append(_gla_block(qkb_ref[hh], qkb_ref[heads + hh], vtb_ref[cols, :], g_b[:, gcols],
                                 s_ref.at[1, hh], ob_ref.at[:, cols], True))
    groups = [chains[k:k + SCAN_SKEW_GROUP] for k in range(0, len(chains), SCAN_SKEW_GROUP)]
    tick = 0
    while any(groups):
        for k, group in enumerate(groups):
            if tick < k:
                continue
            alive = []
            for chain in group:
                try:
                    next(chain)
                    alive.append(chain)
                except StopIteration:
                    pass
            groups[k] = alive
        tick += 1


def _gla_scan(qk, vt, low, w_gate_up, b_gate_up, n_batch, seq, ctx_len):
    heads = GLA_HEADS
    _, t_all, hk = qk.shape
    dk = heads * hk
    dv = vt.shape[0]
    hv = dv // heads
    rank = GLA_GATE_RANK
    tb = GLA_BLOCK
    assert ctx_len == tb and seq % tb == 0
    nlat = seq // tb
    ctx0 = n_batch * nlat
    wup = jnp.zeros((2 * rank, 2 * dk), f32)
    wup = wup.at[:rank, :dk].set(w_gate_up[0]).at[rank:, dk:].set(w_gate_up[1]).astype(bf16)
    bup = b_gate_up.reshape(1, 2 * dk)

    def fwd_row(b, t):
        return jnp.where(t == 0, ctx0 + b, b * nlat + t - 1)

    def bwd_row(b, t):
        return jnp.where(t == 0, ctx0 + b, b * nlat + nlat - t)

    def specs(row_fn):
        return [
            pl.BlockSpec((2 * heads, tb, hk), lambda b, t: (0, row_fn(b, t), 0)),
            pl.BlockSpec((dv, tb), lambda b, t: (0, row_fn(b, t))),
            pl.BlockSpec((tb, 2 * rank), lambda b, t: (row_fn(b, t), 0)),
        ]

    return pl.pallas_call(
        functools.partial(_gla_scan_kernel, heads=heads),
        grid=(n_batch, nlat + 1),
        in_specs=specs(fwd_row) + specs(bwd_row) + [_const_spec((2 * rank, 2 * dk)), _const_spec((1, 2 * dk))],
        out_specs=[
            pl.BlockSpec((tb, dv), lambda b, t: (fwd_row(b, t), 0)),
            pl.BlockSpec((tb, dv), lambda b, t: (bwd_row(b, t), 0)),
        ],
        out_shape=[jax.ShapeDtypeStruct((t_all, dv), bf16)] * 2,
        scratch_shapes=[pltpu.VMEM((2, heads, hv, hk), f32)],
        compiler_params=pltpu.CompilerParams(dimension_semantics=("arbitrary", "arbitrary"),
                                             vmem_limit_bytes=VMEM_LIMIT),
        name="gla_scan",
    )(qk, vt, low, qk, vt, low, wup, bup)


def _swa_proj_kernel(x_ref, mod_ref, nw_ref, w_ref, b_ref, cos_ref, sin_ref,
                     q_ref, kx_ref, vx_ref, *, d, nq):
    sh, sc, _ = _mod_slices(mod_ref, d, 0)
    subs = _sub_tiles(x_ref.shape[0])
    hs = [(_rms(x_ref[rs, :], nw_ref[...]) * (1.0 + sc) + sh).astype(bf16) for rs in subs]
    half = SWA_HD // 2
    for rs, h in zip(subs, hs):
        qkv = _nn(h, w_ref[...]) + b_ref[...]
        cos = cos_ref[rs, :]
        sin = sin_ref[rs, :]
        lane = lax.broadcasted_iota(jnp.int32, cos.shape, 1)
        first_half = (lane % SWA_HD) < half
        low_head = lane < SWA_HD

        def rope(v):
            partner = jnp.where(first_half, pltpu.roll(v, LANES - half, 1), pltpu.roll(v, half, 1))
            return v * cos + partner * sin

        for j in range(nq // LANES):
            q_ref[rs, j * LANES:(j + 1) * LANES] = (
                rope(qkv[:, j * LANES:(j + 1) * LANES]) * (SWA_HD ** -0.5)).astype(bf16)
        k = rope(qkv[:, nq:nq + LANES])
        v = qkv[:, nq + LANES:nq + 2 * LANES]
        for src_val, dst in ((k, kx_ref), (v, vx_ref)):
            swapped = pltpu.roll(src_val, SWA_HD, 1)
            dst[0, rs, :] = jnp.where(low_head, src_val, 0.0).astype(bf16)
            dst[1, rs, :] = jnp.where(low_head, 0.0, swapped).astype(bf16)
            dst[2, rs, :] = jnp.where(low_head, swapped, 0.0).astype(bf16)
            dst[3, rs, :] = jnp.where(low_head, 0.0, src_val).astype(bf16)


def _rope_tables(seq, extra_rows):
    rows = seq // GRID_W
    row = jnp.broadcast_to(jnp.arange(rows)[:, None], (rows, GRID_W)).reshape(-1).astype(f32)
    col = jnp.broadcast_to(jnp.arange(GRID_W)[None, :], (rows, GRID_W)).reshape(-1).astype(f32)
    n_freq = SWA_HD // 4
    inv = ROPE_THETA ** (-jnp.arange(n_freq, dtype=f32) / n_freq)
    ang = jnp.concatenate([row[:, None] * inv, col[:, None] * inv], axis=-1)
    cos, sin = jnp.cos(ang), jnp.sin(ang)
    reps = LANES // SWA_HD
    cos_l = jnp.tile(jnp.concatenate([cos, cos], axis=-1), (1, reps))
    sin_l = jnp.tile(jnp.concatenate([-sin, sin], axis=-1), (1, reps))
    cos_l = jnp.concatenate([cos_l, jnp.ones((extra_rows, LANES), f32)], axis=0)
    sin_l = jnp.concatenate([sin_l, jnp.zeros((extra_rows, LANES), f32)], axis=0)
    return cos_l, sin_l


def _swa_proj(x_all, mod, norm_w, w_qkv16, b_qkv, cos_l, sin_l, geo):
    t_all, d = x_all.shape
    nq = SWA_HEADS * SWA_HD
    nqkv = w_qkv16.shape[1]
    tm = ROW_TILE
    n_lat_tiles, tiles_per_batch, _ = geo

    def pos_map(i):
        return (jnp.where(i < n_lat_tiles, i % tiles_per_batch, tiles_per_batch), 0)

    return pl.pallas_call(
        functools.partial(_swa_proj_kernel, d=d, nq=nq),
        grid=(t_all // tm,),
        in_specs=[
            pl.BlockSpec((tm, d), lambda i: (i, 0)),
            pl.BlockSpec((1, 1, 6 * d), _group_map(*geo)),
            _const_spec((1, d)),
            _const_spec(w_qkv16.shape),
            _const_spec((1, nqkv)),
            pl.BlockSpec((tm, LANES), pos_map),
            pl.BlockSpec((tm, LANES), pos_map),
        ],
        out_specs=[
            pl.BlockSpec((tm, nq), lambda i: (i, 0)),
            pl.BlockSpec((4, tm, LANES), lambda i: (0, i, 0)),
            pl.BlockSpec((4, tm, LANES), lambda i: (0, i, 0)),
        ],
        out_shape=[
            jax.ShapeDtypeStruct((t_all, nq), bf16),
            jax.ShapeDtypeStruct((4, t_all, LANES), bf16),
            jax.ShapeDtypeStruct((4, t_all, LANES), bf16),
        ],
        compiler_params=pltpu.CompilerParams(dimension_semantics=("arbitrary",),
                                             vmem_limit_bytes=VMEM_LIMIT),
        name="swa_proj",
    )(x_all, mod, norm_w.reshape(1, d), w_qkv16, b_qkv.reshape(1, nqkv), cos_l, sin_l)


def _attend(q_ref, o_ref, sink_ref, k_pieces, v_pieces, mask, n_masked):
    tq = q_ref.shape[0]
    pairs = q_ref.shape[1] // LANES
    group = pairs // SWA_KV_HEADS
    stack = ATTN_STACK
    lane = lax.broadcasted_iota(jnp.int32, (tq, LANES), 1)
    low_head = lane < SWA_HD
    k_cat, v_cat = [], []
    for g in range(SWA_KV_HEADS):
        k_cat.append(jnp.concatenate(k_pieces[g][0] + k_pieces[g][1], axis=0))
        nk = k_cat[g].shape[0] // 2
        lane2 = lax.broadcasted_iota(jnp.int32, (nk, LANES), 1)
        ones_lo = jnp.where(lane2 < SWA_HD, 1.0, 0.0).astype(bf16)
        ones_hi = jnp.where(lane2 < SWA_HD, 0.0, 1.0).astype(bf16)
        v_cat.append(jnp.concatenate([
            jnp.concatenate([jnp.concatenate(v_pieces[g][0], axis=0), ones_lo], axis=1),
            jnp.concatenate([jnp.concatenate(v_pieces[g][1], axis=0), ones_hi], axis=1)], axis=0))
    units = [(u * stack) // group for u in range(pairs // stack)]

    def scores(u):
        q_stack = jnp.concatenate(
            [q_ref[:, (u * stack + jj) * LANES:(u * stack + jj + 1) * LANES] for jj in range(stack)], axis=0)
        return _nt(q_stack, k_cat[units[u]])

    ahead = [scores(u) for u in range(min(ATTN_AHEAD, len(units)))]
    for u, g in enumerate(units):
        s_all = ahead.pop(0)
        if u + ATTN_AHEAD < len(units):
            ahead.append(scores(u + ATTN_AHEAD))
        prob_rows, sink_rows = [], []
        for jj in range(stack):
            j = u * stack + jj
            probs, sink_terms = [], []
            for p in range(2):
                sp = s_all[jj * tq:(jj + 1) * tq, p * nk:(p + 1) * nk]
                if mask is not None:
                    sp = jnp.concatenate(
                        [jnp.where(mask, sp[:, :n_masked], NEG_INF), sp[:, n_masked:]], axis=1)
                sink = sink_ref[2 * j + p]
                m = jnp.maximum(jnp.max(sp, axis=-1, keepdims=True), sink)
                probs.append(jnp.exp(sp - m).astype(bf16))
                sink_terms.append(jnp.exp(sink - m))
            prob_rows.append(jnp.concatenate(probs, axis=1))
            sink_rows.append(jnp.where(low_head, sink_terms[0], sink_terms[1]))
        out_all = _nn(jnp.concatenate(prob_rows, axis=0), v_cat[g])
        for jj in range(stack):
            j = u * stack + jj
            out = out_all[jj * tq:(jj + 1) * tq]
            den = out[:, LANES:] + sink_rows[jj]
            o_ref[:, j * LANES:(j + 1) * LANES] = (out[:, :LANES] / den).astype(bf16)
        yield


def _trace_in_order(chains):
    for chain in chains:
        for _ in chain:
            pass


def _swa_kernel(sink_ref, q_ref, kp_ref, ko_ref, kn_ref, kc_ref, vp_ref, vo_ref, vn_ref, vc_ref,
                o_ref, *, seq, n_lat_steps, need_ctx):
    i = pl.program_id(1)
    blk = SWA_BLOCK

    def pieces(refs_of_half):
        return [[[load(2 * g + p) for load in refs_of_half] for p in range(2)] for g in range(SWA_KV_HEADS)]

    def rows(ref, lo):
        return lambda idx: ref[idx, lo:lo + blk, :]

    @pl.when(i < n_lat_steps)
    def _():
        halves = []
        for half in range(2):
            first = 2 * i + half
            qi = first * blk + lax.broadcasted_iota(jnp.int32, (blk, 3 * blk), 0)
            kj = (first - 1) * blk + lax.broadcasted_iota(jnp.int32, (blk, 3 * blk), 1)
            mask = (jnp.abs(kj - qi) <= SWA_WINDOW) & (kj >= 0) & (kj < seq)
            if half == 0:
                k_loads = (rows(kp_ref, 0), rows(ko_ref, 0), rows(ko_ref, blk), lambda idx: kc_ref[idx])
                v_loads = (rows(vp_ref, 0), rows(vo_ref, 0), rows(vo_ref, blk), lambda idx: vc_ref[idx])
            else:
                k_loads = (rows(ko_ref, 0), rows(ko_ref, blk), rows(kn_ref, 0), lambda idx: kc_ref[idx])
                v_loads = (rows(vo_ref, 0), rows(vo_ref, blk), rows(vn_ref, 0), lambda idx: vc_ref[idx])
            rs = slice(half * blk, (half + 1) * blk)
            halves.append(_attend(q_ref.at[rs, :], o_ref.at[rs, :], sink_ref, pieces(k_loads), pieces(v_loads),
                                  mask, 3 * blk))
        _trace_in_order(halves)

    if need_ctx:
        @pl.when(i >= n_lat_steps)
        def _():
            halves = []
            for half in range(2):
                rs = slice(half * blk, (half + 1) * blk)
                halves.append(_attend(q_ref.at[rs, :], o_ref.at[rs, :], sink_ref, pieces((lambda idx: kc_ref[idx],)),
                                      pieces((lambda idx: vc_ref[idx],)), None, 0))
            _trace_in_order(halves)


def _swa_attention(q, kx, vx, sinks, n_batch, seq, ctx_len, need_ctx):
    t_all, nq = q.shape
    blk = SWA_BLOCK
    step = 2 * blk
    assert seq % step == 0 and ctx_len % step == 0
    nb = seq // blk
    n_lat_steps = seq // step
    n_ctx_steps = ctx_len // step
    out_rows = t_all if need_ctx else n_batch * seq
    smem = pl.BlockSpec(memory_space=pltpu.SMEM)

    def edge(off):
        return pl.BlockSpec((4, blk, LANES),
                            lambda b, i: (0, b * nb + jnp.clip(2 * i + off, 0, nb - 1), 0))

    def own():
        return pl.BlockSpec((4, step, LANES),
                            lambda b, i: (0, b * n_lat_steps + jnp.minimum(i, n_lat_steps - 1), 0))

    def ctx_spec():
        return pl.BlockSpec((4, ctx_len, LANES), lambda b, i: (0, (n_batch * seq) // ctx_len + b, 0))

    def q_row(b, i):
        return (jnp.where(i < n_lat_steps, b * n_lat_steps + i,
                          n_batch * n_lat_steps + b * n_ctx_steps + i - n_lat_steps), 0)

    return pl.pallas_call(
        functools.partial(_swa_kernel, seq=seq, n_lat_steps=n_lat_steps, need_ctx=need_ctx),
        grid=(n_batch, n_lat_steps + n_ctx_steps if need_ctx else n_lat_steps),
        in_specs=[smem, pl.BlockSpec((step, nq), q_row),
                  edge(-1), own(), edge(2), ctx_spec(), edge(-1), own(), edge(2), ctx_spec()],
        out_specs=pl.BlockSpec((step, nq), q_row),
        out_shape=jax.ShapeDtypeStruct((out_rows, nq), bf16),
        compiler_params=pltpu.CompilerParams(dimension_semantics=("arbitrary", "arbitrary"),
                                             vmem_limit_bytes=VMEM_LIMIT),
        name="swa_attn",
    )(sinks, q, kx, kx, kx, kx, vx, vx, vx, vx)


def _post_kernel(*refs, d, gla, final, hidden, n_x, n_lat_tiles, n_cast):
    x_refs, refs = refs[:n_x], refs[n_x:]
    if gla:
        of_ref, ob_ref, r_ref, on_ref = refs[:4]
        rest = refs[4:]
    else:
        a_ref = refs[0]
        rest = refs[1:]
    mod_ref, wo_ref, nw_ref, win_ref, wout_ref = rest[:5]
    rest = rest[5:]
    if final:
        nf_ref, rest = rest[0], rest[1:]
    cast_in, out_ref, cast_out, acc_ref = rest[:n_cast], rest[n_cast], rest[n_cast + 1:-1], rest[-1]
    for src_ref, dst_ref in zip(cast_in, cast_out):
        dst_ref[...] = src_ref[...].astype(bf16)

    if gla:
        o = of_ref[...].astype(f32) + ob_ref[...].astype(f32)
        hv = d // GLA_HEADS
        parts = []
        for hh in range(GLA_HEADS):
            seg = o[:, hh * hv:(hh + 1) * hv]
            parts.append(_rms(seg, on_ref[:, hh * hv:(hh + 1) * hv]))
        a = (jnp.concatenate(parts, axis=1) * r_ref[...].astype(f32)).astype(bf16)
    else:
        a = a_ref[...]
    _, _, g1 = _mod_slices(mod_ref, d, 0)
    sh2, sc2, g2 = _mod_slices(mod_ref, d, 3)
    x1 = _load_x(x_refs, n_lat_tiles) + g1 * _nn(a, wo_ref[...])
    hn = (_rms(x1, nw_ref[...]) * (1.0 + sc2) + sh2).astype(bf16)
    acc_ref[...] = jnp.zeros_like(acc_ref)

    def body(ci, carry):
        off = pl.multiple_of(ci * FFN_CHUNK, FFN_CHUNK)
        gate = _nn(hn, win_ref[:, pl.ds(off, FFN_CHUNK)])
        up = _nn(hn, win_ref[:, pl.ds(hidden + off, FFN_CHUNK)])
        act = (_silu(gate) * up).astype(bf16)
        acc_ref[...] += _nn(act, wout_ref[pl.ds(off, FFN_CHUNK), :])
        return carry

    lax.fori_loop(0, hidden // FFN_CHUNK, body, 0, unroll=True)
    x2 = x1 + g2 * acc_ref[...]
    if final:
        x2 = _rms(x2, nf_ref[...])
    out_ref[...] = x2


CAST_BLOCKS = 16


def _post(x_src, mixer_in, mod, w_o16, norm_w, w_in16, w_out16, geo, gla, norm_final, n_tiles, o_norm=None,
          cast_jobs=()):
    hidden, d = w_out16.shape
    assert hidden % FFN_CHUNK == 0
    tm = ROW_TILE
    final = norm_final is not None
    row = lambda i: (i, 0)
    args, in_specs = _x_specs(x_src, geo[0])
    n_x = len(args)
    if gla:
        o_f, o_b, r = mixer_in
        args += [o_f, o_b, r, jnp.tile(o_norm, GLA_HEADS).reshape(1, d)]
        in_specs += [pl.BlockSpec((tm, d), row)] * 3 + [_const_spec((1, d))]
    else:
        args += [mixer_in]
        in_specs += [pl.BlockSpec((tm, d), row)]
    args += [mod, w_o16, norm_w.reshape(1, d), w_in16, w_out16]
    in_specs += [pl.BlockSpec((1, 1, 6 * d), _group_map(*geo)), _const_spec(w_o16.shape), _const_spec((1, d)),
                 _const_spec(w_in16.shape), _const_spec(w_out16.shape)]
    if final:
        args += [norm_final.reshape(1, d)]
        in_specs += [_const_spec((1, d))]
    out_specs = [pl.BlockSpec((tm, d), row)]
    out_shape = [jax.ShapeDtypeStruct((n_tiles * tm, d), f32)]
    n_blocks = CAST_BLOCKS
    while n_blocks > n_tiles:
        n_blocks //= 2
    last_block = n_blocks - 1
    for stacked, layer in cast_jobs:
        _, rows, cols = stacked.shape
        assert rows % (16 * n_blocks) == 0
        br = rows // n_blocks
        args.append(stacked)
        in_specs.append(pl.BlockSpec((None, br, cols),
                                     lambda i, layer=layer: (layer, jnp.minimum(i, last_block), 0)))
        out_specs.append(pl.BlockSpec((br, cols), lambda i: (jnp.minimum(i, last_block), 0)))
        out_shape.append(jax.ShapeDtypeStruct((rows, cols), bf16))
    outs = pl.pallas_call(
        functools.partial(_post_kernel, d=d, gla=gla, final=final, hidden=hidden, n_x=n_x,
                          n_lat_tiles=geo[0], n_cast=len(cast_jobs)),
        grid=(n_tiles,),
        in_specs=in_specs,
        out_specs=out_specs,
        out_shape=out_shape,
        scratch_shapes=[pltpu.VMEM((tm, d), f32)],
        compiler_params=pltpu.CompilerParams(dimension_semantics=("arbitrary",),
                                             vmem_limit_bytes=VMEM_LIMIT),
        name="post_gla" if gla else "post_swa",
    )(*args)
    return outs[0], list(outs[1:])


def kernel(x, c, ctx, c_ctx, w_ada, b_ada, norm_mix, norm_ffn, gla_w_in, gla_w_gate_up, gla_b_gate_up,
           gla_w_o, gla_o_norm, swa_w_qkv, swa_b_qkv, swa_w_o, swa_sinks, ffn_w_in, ffn_w_out,
           norm_final):
    n_batch, seq, d = x.shape
    ctx_len = ctx.shape[1]
    depth = w_ada.shape[0]
    tm = ROW_TILE
    assert seq % tm == 0 and (n_batch * ctx_len) == tm
    n_lat_tiles = n_batch * seq // tm
    geo = (n_lat_tiles, seq // tm, n_batch)

    mod_all = _ada_mod(jnp.concatenate([c, c_ctx[None, :]], axis=0), w_ada, b_ada)
    t_all = n_batch * (seq + ctx_len)
    x_src = (x.reshape(n_batch * seq, d), ctx.reshape(n_batch * ctx_len, d))
    cos_l, sin_l = _rope_tables(seq, tm)

    def mixer_jobs(layer):
        jj = layer // 2
        mix = [(gla_w_o, jj)] if layer % 2 == 0 else [(swa_w_qkv, jj), (swa_w_o, jj)]
        return mix + [(ffn_w_in, layer), (ffn_w_out, layer)]

    w16 = [stacked[layer].astype(bf16) for stacked, layer in mixer_jobs(0)]
    for i in range(depth):
        last = i == depth - 1
        j = i // 2
        mod = mod_all[i].reshape(8, 1, 6 * d)
        n_tiles = n_lat_tiles if last else n_lat_tiles + 1
        nf = norm_final if last else None
        jobs = () if last else mixer_jobs(i + 1)
        w_o16, w_ffn_in16, w_ffn_out16 = w16[-3:]
        if i % 2 == 0:
            qk, r, vt, low = _gla_proj(x_src, t_all, mod, norm_mix[i], gla_w_in[j].astype(bf16), geo)
            o_f, o_b = _gla_scan(qk, vt, low, gla_w_gate_up[j], gla_b_gate_up[j], n_batch, seq, ctx_len)
            x_src, w16 = _post(x_src, (o_f, o_b, r), mod, w_o16, norm_ffn[i], w_ffn_in16, w_ffn_out16,
                               geo, True, nf, n_tiles, o_norm=gla_o_norm[j], cast_jobs=jobs)
        else:
            q, kx, vx = _swa_proj(x_src, mod, norm_mix[i], w16[0], swa_b_qkv[j], cos_l, sin_l, geo)
            attn = _swa_attention(q, kx, vx, swa_sinks[j], n_batch, seq, ctx_len, not last)
            x_src, w16 = _post(x_src, attn, mod, w_o16, norm_ffn[i], w_ffn_in16, w_ffn_out16,
                               geo, False, nf, n_tiles, cast_jobs=jobs)
    return x_src.reshape(n_batch, seq, d)
```

```python
import functools

import jax
import jax.numpy as jnp
from jax import lax
from jax.experimental import pallas as pl
from jax.experimental.pallas import tpu as pltpu

f32 = jnp.float32
bf16 = jnp.bfloat16

EPS = 1e-6
LOG2E = 1.4426950408889634
NEG_INF = -1e30
GRID_W = 64
ROPE_THETA = 10000.0

GLA_HEADS = 4
GLA_GATE_RANK = 16
GLA_GATE_NORM = 16.0
GLA_CHUNK = 64
GLA_BLOCK = 256
SCAN_SKEW_GROUP = 4

SWA_HEADS = 16
SWA_KV_HEADS = 2
SWA_HD = 64
SWA_WINDOW = 128
SWA_BLOCK = 128
ATTN_STACK = 1

ATTN_AHEAD = 2

LANES = 128
FFN_CHUNK = 256
ROW_TILE = 512
SUB_TILES = 2
VMEM_LIMIT = 56 * 1024 * 1024


def _nt(a, b):
    return lax.dot_general(a, b, (((1,), (1,)), ((), ())), preferred_element_type=f32)


def _nn(a, b):
    return jnp.dot(a, b, preferred_element_type=f32)


def _silu(v):
    return v * (1.0 / (1.0 + jnp.exp(-v)))


def _rms(v, w):
    return v * lax.rsqrt(jnp.mean(v * v, axis=-1, keepdims=True) + EPS) * w


def _const_spec(shape):
    zeros = (0,) * len(shape)
    return pl.BlockSpec(shape, lambda *_: zeros, pipeline_mode=pl.Buffered(1))


def _ada_kernel(cb_ref, w_ref, b_ref, out_ref, *, n_rows, d_model, tn):
    ncol = tn // LANES

    def body(i, accs):
        d0 = pl.multiple_of(i * 8, 8)
        w8 = w_ref[0, pl.ds(d0, 8), :]
        new = []
        for r in range(n_rows):
            s8 = _silu(cb_ref[r, pl.ds(d0, 8), :])
            new.append(tuple(accs[r][j] + w8[:, j * LANES:(j + 1) * LANES] * s8 for j in range(ncol)))
        return tuple(new)

    init = tuple(tuple(jnp.zeros((8, LANES), f32) for _ in range(ncol)) for _ in range(n_rows))
    accs = lax.fori_loop(0, d_model // 8, body, init, unroll=8)
    rows = [jnp.concatenate([jnp.sum(a, axis=0, keepdims=True) for a in accs[r]], axis=1)
            for r in range(n_rows)]
    rows += [jnp.zeros((1, tn), f32)] * (8 - n_rows)
    out_ref[0] = jnp.concatenate(rows, axis=0) + b_ref[0]


def _ada_mod(cvecs, w_ada, b_ada):
    n_rows, d_model = cvecs.shape
    depth, _, e = w_ada.shape
    tn = 1536
    cb = jnp.broadcast_to(cvecs[:, :, None], (n_rows, d_model, LANES))
    return pl.pallas_call(
        functools.partial(_ada_kernel, n_rows=n_rows, d_model=d_model, tn=tn),
        grid=(depth, e // tn),
        in_specs=[
            pl.BlockSpec((n_rows, d_model, LANES), lambda l, j: (0, 0, 0)),
            pl.BlockSpec((1, d_model, tn), lambda l, j: (l, 0, j)),
            pl.BlockSpec((1, 1, tn), lambda l, j: (l, 0, j)),
        ],
        out_specs=pl.BlockSpec((1, 8, tn), lambda l, j: (l, 0, j)),
        out_shape=jax.ShapeDtypeStruct((depth, 8, e), f32),
        compiler_params=pltpu.CompilerParams(dimension_semantics=("arbitrary", "arbitrary")),
        name="ada_mod",
    )(cb, w_ada, b_ada.reshape(depth, 1, e))


def _mod_slices(mod_ref, d, first):
    return tuple(mod_ref[0, :, (first + k) * d:(first + k + 1) * d] for k in range(3))


def _x_specs(x_src, n_lat_tiles):
    tm = ROW_TILE
    if isinstance(x_src, tuple):
        lat, ctx = x_src
        d = lat.shape[1]
        assert ctx.shape == (tm, d)
        return [lat, ctx], [pl.BlockSpec((tm, d), lambda i: (jnp.minimum(i, n_lat_tiles - 1), 0)),
                            pl.BlockSpec((tm, d), lambda i: (0, 0))]
    return [x_src], [pl.BlockSpec((tm, x_src.shape[1]), lambda i: (i, 0))]


def _load_x(x_refs, n_lat_tiles, rs=slice(None)):
    if len(x_refs) == 2:
        return jnp.where(pl.program_id(0) < n_lat_tiles, x_refs[0][rs, :], x_refs[1][rs, :])
    return x_refs[0][rs, :]


def _sub_tiles(rows):
    step = rows // SUB_TILES
    return [slice(s * step, (s + 1) * step) for s in range(SUB_TILES)]


def _group_map(n_lat_tiles, tiles_per_batch, n_batch):
    def index_map(i):
        return (jnp.where(i < n_lat_tiles, i // tiles_per_batch, n_batch), 0, 0)
    return index_map


def _gla_proj_kernel(*refs, d, dk, dv, heads, n_x, n_lat_tiles):
    x_refs, (mod_ref, nw_ref, win_ref, qk_ref, r_ref, vt_ref, low_ref, wvt_ref) = refs[:n_x], refs[n_x:]

    @pl.when(pl.program_id(0) == 0)
    def _():
        wvt_ref[...] = win_ref[:, 2 * dk:2 * dk + dv].astype(f32).T.astype(bf16)

    sh, sc, _ = _mod_slices(mod_ref, d, 0)
    hk = dk // heads
    subs = _sub_tiles(r_ref.shape[0])
    hs = [(_rms(_load_x(x_refs, n_lat_tiles, rs), nw_ref[...]) * (1.0 + sc) + sh).astype(bf16) for rs in subs]
    for rs, h in zip(subs, hs):
        low_ref[rs, :] = _nn(h, win_ref[:, 2 * dk + 2 * dv:]).astype(bf16)
        r_ref[rs, :] = _silu(_nn(h, win_ref[:, 2 * dk + dv:2 * dk + 2 * dv])).astype(bf16)
        qk = _nn(h, win_ref[:, :2 * dk])
        q = qk[:, :dk] * (hk ** -0.5)
        for hh in range(heads):
            qk_ref[hh, rs, :] = q[:, hh * hk:(hh + 1) * hk].astype(bf16)
            qk_ref[heads + hh, rs, :] = qk[:, dk + hh * hk:dk + (hh + 1) * hk].astype(bf16)
        vt_ref[:, rs] = _nt(wvt_ref[...], h).astype(bf16)


def _gla_proj(x_src, t_all, mod, norm_w, w_in16, geo):
    d = w_in16.shape[0]
    dk, dv, heads = d // 2, d, GLA_HEADS
    hk = dk // heads
    tm = ROW_TILE
    x_args, x_specs = _x_specs(x_src, geo[0])
    kern = functools.partial(_gla_proj_kernel, d=d, dk=dk, dv=dv, heads=heads, n_x=len(x_args),
                             n_lat_tiles=geo[0])
    return pl.pallas_call(
        kern,
        grid=(t_all // tm,),
        in_specs=x_specs + [
            pl.BlockSpec((1, 1, 6 * d), _group_map(*geo)),
            _const_spec((1, d)),
            _const_spec(w_in16.shape),
        ],
        out_specs=[
            pl.BlockSpec((2 * heads, tm, hk), lambda i: (0, i, 0)),
            pl.BlockSpec((tm, dv), lambda i: (i, 0)),
            pl.BlockSpec((dv, tm), lambda i: (0, i)),
            pl.BlockSpec((tm, 2 * GLA_GATE_RANK), lambda i: (i, 0)),
        ],
        out_shape=[
            jax.ShapeDtypeStruct((2 * heads, t_all, hk), bf16),
            jax.ShapeDtypeStruct((t_all, dv), bf16),
            jax.ShapeDtypeStruct((dv, t_all), bf16),
            jax.ShapeDtypeStruct((t_all, 2 * GLA_GATE_RANK), bf16),
        ],
        scratch_shapes=[pltpu.VMEM((dv, d), bf16)],
        compiler_params=pltpu.CompilerParams(dimension_semantics=("arbitrary",),
                                             vmem_limit_bytes=VMEM_LIMIT),
        name="gla_proj",
    )(*x_args, mod, norm_w.reshape(1, d), w_in16)


def _gla_block(q, k, vt, g, s_ref, o_ref, reverse):
    tb, hk = q.shape
    c = GLA_CHUNK
    half = tb // 2
    assert tb == 4 * c
    row = lax.broadcasted_iota(jnp.int32, (tb, tb), 0)
    col = lax.broadcasted_iota(jnp.int32, (tb, tb), 1)
    causal = ((row // c) == (col // c)) & ((col >= row) if reverse else (col <= row))
    cum = jnp.where(causal, 1.0, 0.0).astype(bf16)
    g_hi = g.astype(bf16)
    g_lo = (g - g_hi.astype(f32)).astype(bf16)
    b2 = _nn(cum, jnp.concatenate([g_hi, g_lo], axis=1))
    yield
    b = b2[:, :hk] + b2[:, hk:]
    order = (3, 2, 1, 0) if reverse else (0, 1, 2, 3)
    pos = {ch: s for s, ch in enumerate(order)}
    tot = []
    for ch in range(4):
        last = ch * c if reverse else ch * c + c - 1
        tot.append(b[last:last + 1, :])
    zero = jnp.zeros_like(tot[0])
    before = [sum((tot[order[s2]] for s2 in range(pos[ch])), zero) for ch in range(4)]
    after = [sum((tot[order[s2]] for s2 in range(pos[ch] + 1, 4)), zero) for ch in range(4)]

    def rows(vals):
        return jnp.concatenate([jnp.broadcast_to(v, (c, hk)) for v in vals], axis=0)

    b_tot = rows(tot)
    kf = k.astype(f32)
    qd = q.astype(f32) * jnp.exp2(b)
    ki = kf * jnp.exp2(-b)
    ke = kf * jnp.exp2(b_tot - b)
    qd16, ki16, ke16 = qd.astype(bf16), ki.astype(bf16), ke.astype(bf16)
    yield

    lrow = lax.broadcasted_iota(jnp.int32, (half, half), 0)
    lcol = lax.broadcasted_iota(jnp.int32, (half, half), 1)
    m_same = ((lrow // c) == (lcol // c)) & ((lcol >= lrow) if reverse else (lcol <= lrow))
    m_next = ((lrow < c) & (lcol >= c)) if reverse else ((lrow >= c) & (lcol < c))
    xs = [_nt(qd16[lo:lo + half], jnp.concatenate([ki16[lo:lo + half], ke16[lo:lo + half]], axis=0))
          for lo in (0, half)]

    late, early = (0, half) if reverse else (half, 0)
    q_far = order[3] * c - late
    k_far = order[0] * c - early
    q_scale = [zero, zero]
    q_scale[q_far // c] = tot[order[2]]
    k_scale = [zero, zero]
    k_scale[k_far // c] = tot[order[1]]
    q2 = (qd[late:late + half] * jnp.exp2(rows(q_scale))).astype(bf16)
    k2 = (ke[early:early + half] * jnp.exp2(rows(k_scale))).astype(bf16)
    cross_f32 = _nt(q2, k2)
    yield
    cross = cross_f32.astype(bf16)

    diag = [jnp.where(m_same, x[:, :half], jnp.where(m_next, x[:, half:], 0.0)).astype(bf16) for x in xs]
    if reverse:
        p_early = diag[1]
        p_late = jnp.concatenate([diag[0], cross], axis=1)
    else:
        p_early = diag[0]
        p_late = jnp.concatenate([cross, diag[1]], axis=1)
    s = s_ref[...]
    inter = _nt((qd * jnp.exp2(rows(before))).astype(bf16), s.astype(bf16))
    pv_early = _nt(p_early, vt[:, early:early + half])
    pv_late = _nt(p_late, vt)
    k_end = (ke * jnp.exp2(rows(after))).astype(bf16)
    upd = _nn(vt, k_end)
    yield
    o_ref[early:early + half, :] = (pv_early + inter[early:early + half]).astype(o_ref.dtype)
    o_ref[late:late + half, :] = (pv_late + inter[late:late + half]).astype(o_ref.dtype)
    s_ref[...] = s * jnp.exp2(before[order[3]] + tot[order[3]]) + upd


def _gla_scan_kernel(qkf_ref, vtf_ref, lowf_ref, qkb_ref, vtb_ref, lowb_ref, wup_ref, bup_ref,
                     of_ref, ob_ref, s_ref, *, heads):
    @pl.when(pl.program_id(1) == 0)
    def _():
        s_ref[...] = jnp.zeros_like(s_ref)

    hv = vtf_ref.shape[0] // heads
    hk = qkf_ref.shape[2]
    dk = heads * hk

    def log_decay(low_ref, direction):
        cols = slice(direction * dk, (direction + 1) * dk)
        z2 = _nn(low_ref[...], wup_ref[:, cols]) + bup_ref[:, cols]
        return (jnp.minimum(z2, 0.0) - jnp.log2(1.0 + jnp.exp2(-jnp.abs(z2)))) * (1.0 / GLA_GATE_NORM)

    g_f = log_decay(lowf_ref, 0)
    g_b = log_decay(lowb_ref, 1)
    chains = []
    for hh in range(heads):
        cols = slice(hh * hv, (hh + 1) * hv)
        gcols = slice(hh * hk, (hh + 1) * hk)
        chains.append(_gla_block(qkf_ref[hh], qkf_ref[heads + hh], vtf_ref[cols, :], g_f[:, gcols],
                                 s_ref.at[0, hh], of_ref.at[:, cols], False))
        chains.append(_gla_block(qkb_ref[hh], qkb_ref[heads + hh], vtb_ref[cols, :], g_b[:, gcols],
                                 s_ref.at[1, hh], ob_ref.at[:, cols], True))
    groups = [chains[k:k + SCAN_SKEW_GROUP] for k in range(0, len(chains), SCAN_SKEW_GROUP)]
    tick = 0
    while any(groups):
        for k, group in enumerate(groups):
            if tick < k:
                continue
            alive = []
            for chain in group:
                try:
                    next(chain)
                    alive.append(chain)
                except StopIteration:
                    pass
            groups[k] = alive
        tick += 1


def _gla_scan(qk, vt, low, w_gate_up, b_gate_up, n_batch, seq, ctx_len):
    heads = GLA_HEADS
    _, t_all, hk = qk.shape
    dk = heads * hk
    dv = vt.shape[0]
    hv = dv // heads
    rank = GLA_GATE_RANK
    tb = GLA_BLOCK
    assert ctx_len == tb and seq % tb == 0
    nlat = seq // tb
    ctx0 = n_batch * nlat
    wup = jnp.zeros((2 * rank, 2 * dk), f32)
    wup = (wup.at[:rank, :dk].set(w_gate_up[0]).at[rank:, dk:].set(w_gate_up[1]) * LOG2E).astype(bf16)
    bup = b_gate_up.reshape(1, 2 * dk) * LOG2E

    def fwd_row(b, t):
        return jnp.where(t == 0, ctx0 + b, b * nlat + t - 1)

    def bwd_row(b, t):
        return jnp.where(t == 0, ctx0 + b, b * nlat + nlat - t)

    def specs(row_fn):
        return [
            pl.BlockSpec((2 * heads, tb, hk), lambda b, t: (0, row_fn(b, t), 0)),
            pl.BlockSpec((dv, tb), lambda b, t: (0, row_fn(b, t))),
            pl.BlockSpec((tb, 2 * rank), lambda b, t: (row_fn(b, t), 0)),
        ]

    return pl.pallas_call(
        functools.partial(_gla_scan_kernel, heads=heads),
        grid=(n_batch, nlat + 1),
        in_specs=specs(fwd_row) + specs(bwd_row) + [_const_spec((2 * rank, 2 * dk)), _const_spec((1, 2 * dk))],
        out_specs=[
            pl.BlockSpec((tb, dv), lambda b, t: (fwd_row(b, t), 0)),
            pl.BlockSpec((tb, dv), lambda b, t: (bwd_row(b, t), 0)),
        ],
        out_shape=[jax.ShapeDtypeStruct((t_all, dv), bf16)] * 2,
        scratch_shapes=[pltpu.VMEM((2, heads, hv, hk), f32)],
        compiler_params=pltpu.CompilerParams(dimension_semantics=("arbitrary", "arbitrary"),
                                             vmem_limit_bytes=VMEM_LIMIT),
        name="gla_scan",
    )(qk, vt, low, qk, vt, low, wup, bup)


def _swa_proj_kernel(x_ref, mod_ref, nw_ref, w_ref, b_ref, cos_ref, sin_ref,
                     q_ref, kx_ref, vx_ref, *, d, nq):
    sh, sc, _ = _mod_slices(mod_ref, d, 0)
    subs = _sub_tiles(x_ref.shape[0])
    hs = [(_rms(x_ref[rs, :], nw_ref[...]) * (1.0 + sc) + sh).astype(bf16) for rs in subs]
    half = SWA_HD // 2
    for rs, h in zip(subs, hs):
        qkv = _nn(h, w_ref[...]) + b_ref[...]
        cos = cos_ref[rs, :]
        sin = sin_ref[rs, :]
        lane = lax.broadcasted_iota(jnp.int32, cos.shape, 1)
        first_half = (lane % SWA_HD) < half
        low_head = lane < SWA_HD

        def rope(v):
            partner = jnp.where(first_half, pltpu.roll(v, LANES - half, 1), pltpu.roll(v, half, 1))
            return v * cos + partner * sin

        for j in range(nq // LANES):
            q_ref[rs, j * LANES:(j + 1) * LANES] = (
                rope(qkv[:, j * LANES:(j + 1) * LANES]) * (SWA_HD ** -0.5 * LOG2E)).astype(bf16)
        k = rope(qkv[:, nq:nq + LANES])
        v = qkv[:, nq + LANES:nq + 2 * LANES]
        for src_val, dst in ((k, kx_ref), (v, vx_ref)):
            swapped = pltpu.roll(src_val, SWA_HD, 1)
            dst[0, rs, :] = jnp.where(low_head, src_val, 0.0).astype(bf16)
            dst[1, rs, :] = jnp.where(low_head, 0.0, swapped).astype(bf16)
            dst[2, rs, :] = jnp.where(low_head, swapped, 0.0).astype(bf16)
            dst[3, rs, :] = jnp.where(low_head, 0.0, src_val).astype(bf16)


def _rope_tables(seq, extra_rows):
    rows = seq // GRID_W
    row = jnp.broadcast_to(jnp.arange(rows)[:, None], (rows, GRID_W)).reshape(-1).astype(f32)
    col = jnp.broadcast_to(jnp.arange(GRID_W)[None, :], (rows, GRID_W)).reshape(-1).astype(f32)
    n_freq = SWA_HD // 4
    inv = ROPE_THETA ** (-jnp.arange(n_freq, dtype=f32) / n_freq)
    ang = jnp.concatenate([row[:, None] * inv, col[:, None] * inv], axis=-1)
    cos, sin = jnp.cos(ang), jnp.sin(ang)
    reps = LANES // SWA_HD
    cos_l = jnp.tile(jnp.concatenate([cos, cos], axis=-1), (1, reps))
    sin_l = jnp.tile(jnp.concatenate([-sin, sin], axis=-1), (1, reps))
    cos_l = jnp.concatenate([cos_l, jnp.ones((extra_rows, LANES), f32)], axis=0)
    sin_l = jnp.concatenate([sin_l, jnp.zeros((extra_rows, LANES), f32)], axis=0)
    return cos_l, sin_l


def _swa_proj(x_all, mod, norm_w, w_qkv16, b_qkv, cos_l, sin_l, geo):
    t_all, d = x_all.shape
    nq = SWA_HEADS * SWA_HD
    nqkv = w_qkv16.shape[1]
    tm = ROW_TILE
    n_lat_tiles, tiles_per_batch, _ = geo

    def pos_map(i):
        return (jnp.where(i < n_lat_tiles, i % tiles_per_batch, tiles_per_batch), 0)

    return pl.pallas_call(
        functools.partial(_swa_proj_kernel, d=d, nq=nq),
        grid=(t_all // tm,),
        in_specs=[
            pl.BlockSpec((tm, d), lambda i: (i, 0)),
            pl.BlockSpec((1, 1, 6 * d), _group_map(*geo)),
            _const_spec((1, d)),
            _const_spec(w_qkv16.shape),
            _const_spec((1, nqkv)),
            pl.BlockSpec((tm, LANES), pos_map),
            pl.BlockSpec((tm, LANES), pos_map),
        ],
        out_specs=[
            pl.BlockSpec((tm, nq), lambda i: (i, 0)),
            pl.BlockSpec((4, tm, LANES), lambda i: (0, i, 0)),
            pl.BlockSpec((4, tm, LANES), lambda i: (0, i, 0)),
        ],
        out_shape=[
            jax.ShapeDtypeStruct((t_all, nq), bf16),
            jax.ShapeDtypeStruct((4, t_all, LANES), bf16),
            jax.ShapeDtypeStruct((4, t_all, LANES), bf16),
        ],
        compiler_params=pltpu.CompilerParams(dimension_semantics=("arbitrary",),
                                             vmem_limit_bytes=VMEM_LIMIT),
        name="swa_proj",
    )(x_all, mod, norm_w.reshape(1, d), w_qkv16, b_qkv.reshape(1, nqkv), cos_l, sin_l)


def _attend(q_ref, o_ref, sink_ref, k_pieces, v_pieces, edge_masks):
    tq = q_ref.shape[0]
    pairs = q_ref.shape[1] // LANES
    group = pairs // SWA_KV_HEADS
    stack = ATTN_STACK
    lane = lax.broadcasted_iota(jnp.int32, (tq, LANES), 1)
    low_head = lane < SWA_HD
    k_cat, v_cat = [], []
    for g in range(SWA_KV_HEADS):
        k_cat.append(jnp.concatenate(k_pieces[g][0] + k_pieces[g][1], axis=0))
        nk = k_cat[g].shape[0] // 2
        lane2 = lax.broadcasted_iota(jnp.int32, (nk, LANES), 1)
        ones_lo = jnp.where(lane2 < SWA_HD, 1.0, 0.0).astype(bf16)
        ones_hi = jnp.where(lane2 < SWA_HD, 0.0, 1.0).astype(bf16)
        v_cat.append(jnp.concatenate([
            jnp.concatenate([jnp.concatenate(v_pieces[g][0], axis=0), ones_lo], axis=1),
            jnp.concatenate([jnp.concatenate(v_pieces[g][1], axis=0), ones_hi], axis=1)], axis=0))
    units = [(u * stack) // group for u in range(pairs // stack)]

    def scores(u):
        q_stack = jnp.concatenate(
            [q_ref[:, (u * stack + jj) * LANES:(u * stack + jj + 1) * LANES] for jj in range(stack)], axis=0)
        return _nt(q_stack, k_cat[units[u]])

    ahead = [scores(u) for u in range(min(ATTN_AHEAD, len(units)))]
    for u, g in enumerate(units):
        s_all = ahead.pop(0)
        if u + ATTN_AHEAD < len(units):
            ahead.append(scores(u + ATTN_AHEAD))
        prob_rows, sink_rows = [], []
        for jj in range(stack):
            j = u * stack + jj
            probs, sink_terms = [], []
            for p in range(2):
                sp = s_all[jj * tq:(jj + 1) * tq, p * nk:(p + 1) * nk]
                if edge_masks is not None:
                    before, after = edge_masks
                    blk = before.shape[1]
                    sp = jnp.concatenate([jnp.where(before, sp[:, :blk], NEG_INF), sp[:, blk:2 * blk],
                                          jnp.where(after, sp[:, 2 * blk:3 * blk], NEG_INF), sp[:, 3 * blk:]],
                                         axis=1)
                sink = sink_ref[2 * j + p] * LOG2E
                m = jnp.maximum(jnp.max(sp, axis=-1, keepdims=True), sink)
                probs.append(jnp.exp2(sp - m).astype(bf16))
                sink_terms.append(jnp.exp2(sink - m))
            prob_rows.append(jnp.concatenate(probs, axis=1))
            sink_rows.append(jnp.where(low_head, sink_terms[0], sink_terms[1]))
        out_all = _nn(jnp.concatenate(prob_rows, axis=0), v_cat[g])
        for jj in range(stack):
            j = u * stack + jj
            out = out_all[jj * tq:(jj + 1) * tq]
            den = out[:, LANES:] + sink_rows[jj]
            o_ref[:, j * LANES:(j + 1) * LANES] = (out[:, :LANES] / den).astype(bf16)
        yield


def _trace_in_order(chains):
    for chain in chains:
        for _ in chain:
            pass


def _swa_kernel(sink_ref, q_ref, kp_ref, ko_ref, kn_ref, kc_ref, vp_ref, vo_ref, vn_ref, vc_ref,
                o_ref, *, seq, n_lat_steps, need_ctx):
    i = pl.program_id(1)
    blk = SWA_BLOCK

    def pieces(refs_of_half):
        return [[[load(2 * g + p) for load in refs_of_half] for p in range(2)] for g in range(SWA_KV_HEADS)]

    def rows(ref, lo):
        return lambda idx: ref[idx, lo:lo + blk, :]

    @pl.when(i < n_lat_steps)
    def _():
        halves = []
        for half in range(2):
            first = 2 * i + half
            qi = first * blk + lax.broadcasted_iota(jnp.int32, (blk, blk), 0)
            local = lax.broadcasted_iota(jnp.int32, (blk, blk), 1)
            kj_before = (first - 1) * blk + local
            kj_after = (first + 1) * blk + local
            masks = ((qi - kj_before <= SWA_WINDOW) & (kj_before >= 0),
                     (kj_after - qi <= SWA_WINDOW) & (kj_after < seq))
            if half == 0:
                k_loads = (rows(kp_ref, 0), rows(ko_ref, 0), rows(ko_ref, blk), lambda idx: kc_ref[idx])
                v_loads = (rows(vp_ref, 0), rows(vo_ref, 0), rows(vo_ref, blk), lambda idx: vc_ref[idx])
            else:
                k_loads = (rows(ko_ref, 0), rows(ko_ref, blk), rows(kn_ref, 0), lambda idx: kc_ref[idx])
                v_loads = (rows(vo_ref, 0), rows(vo_ref, blk), rows(vn_ref, 0), lambda idx: vc_ref[idx])
            rs = slice(half * blk, (half + 1) * blk)
            halves.append(_attend(q_ref.at[rs, :], o_ref.at[rs, :], sink_ref, pieces(k_loads), pieces(v_loads),
                                  masks))
        _trace_in_order(halves)

    if need_ctx:
        @pl.when(i >= n_lat_steps)
        def _():
            halves = []
            for half in range(2):
                rs = slice(half * blk, (half + 1) * blk)
                halves.append(_attend(q_ref.at[rs, :], o_ref.at[rs, :], sink_ref, pieces((lambda idx: kc_ref[idx],)),
                                      pieces((lambda idx: vc_ref[idx],)), None))
            _trace_in_order(halves)


def _swa_attention(q, kx, vx, sinks, n_batch, seq, ctx_len, need_ctx):
    t_all, nq = q.shape
    blk = SWA_BLOCK
    step = 2 * blk
    assert seq % step == 0 and ctx_len % step == 0
    nb = seq // blk
    n_lat_steps = seq // step
    n_ctx_steps = ctx_len // step
    out_rows = t_all if need_ctx else n_batch * seq
    smem = pl.BlockSpec(memory_space=pltpu.SMEM)

    def edge(off):
        return pl.BlockSpec((4, blk, LANES),
                            lambda b, i: (0, b * nb + jnp.clip(2 * i + off, 0, nb - 1), 0))

    def own():
        return pl.BlockSpec((4, step, LANES),
                            lambda b, i: (0, b * n_lat_steps + jnp.minimum(i, n_lat_steps - 1), 0))

    def ctx_spec():
        return pl.BlockSpec((4, ctx_len, LANES), lambda b, i: (0, (n_batch * seq) // ctx_len + b, 0))

    def q_row(b, i):
        return (jnp.where(i < n_lat_steps, b * n_lat_steps + i,
                          n_batch * n_lat_steps + b * n_ctx_steps + i - n_lat_steps), 0)

    return pl.pallas_call(
        functools.partial(_swa_kernel, seq=seq, n_lat_steps=n_lat_steps, need_ctx=need_ctx),
        grid=(n_batch, n_lat_steps + n_ctx_steps if need_ctx else n_lat_steps),
        in_specs=[smem, pl.BlockSpec((step, nq), q_row),
                  edge(-1), own(), edge(2), ctx_spec(), edge(-1), own(), edge(2), ctx_spec()],
        out_specs=pl.BlockSpec((step, nq), q_row),
        out_shape=jax.ShapeDtypeStruct((out_rows, nq), bf16),
        compiler_params=pltpu.CompilerParams(dimension_semantics=("arbitrary", "arbitrary"),
                                             vmem_limit_bytes=VMEM_LIMIT),
        name="swa_attn",
    )(sinks, q, kx, kx, kx, kx, vx, vx, vx, vx)


def _post_kernel(*refs, d, gla, final, hidden, n_x, n_lat_tiles, n_cast):
    x_refs, refs = refs[:n_x], refs[n_x:]
    if gla:
        of_ref, ob_ref, r_ref, on_ref = refs[:4]
        rest = refs[4:]
    else:
        a_ref = refs[0]
        rest = refs[1:]
    mod_ref, wo_ref, nw_ref, win_ref, wout_ref = rest[:5]
    rest = rest[5:]
    if final:
        nf_ref, rest = rest[0], rest[1:]
    cast_in, out_ref, cast_out, acc_ref = rest[:n_cast], rest[n_cast], rest[n_cast + 1:-1], rest[-1]
    for src_ref, dst_ref in zip(cast_in, cast_out):
        dst_ref[...] = src_ref[...].astype(bf16)

    if gla:
        o = of_ref[...].astype(f32) + ob_ref[...].astype(f32)
        hv = d // GLA_HEADS
        parts = []
        for hh in range(GLA_HEADS):
            seg = o[:, hh * hv:(hh + 1) * hv]
            parts.append(_rms(seg, on_ref[:, hh * hv:(hh + 1) * hv]))
        a = (jnp.concatenate(parts, axis=1) * r_ref[...].astype(f32)).astype(bf16)
    else:
        a = a_ref[...]
    _, _, g1 = _mod_slices(mod_ref, d, 0)
    sh2, sc2, g2 = _mod_slices(mod_ref, d, 3)
    x1 = _load_x(x_refs, n_lat_tiles) + g1 * _nn(a, wo_ref[...])
    hn = (_rms(x1, nw_ref[...]) * (1.0 + sc2) + sh2).astype(bf16)
    acc_ref[...] = jnp.zeros_like(acc_ref)

    def body(ci, carry):
        off = pl.multiple_of(ci * FFN_CHUNK, FFN_CHUNK)
        gate = _nn(hn, win_ref[:, pl.ds(off, FFN_CHUNK)])
        up = _nn(hn, win_ref[:, pl.ds(hidden + off, FFN_CHUNK)])
        act = (_silu(gate) * up).astype(bf16)
        acc_ref[...] += _nn(act, wout_ref[pl.ds(off, FFN_CHUNK), :])
        return carry

    lax.fori_loop(0, hidden // FFN_CHUNK, body, 0, unroll=True)
    x2 = x1 + g2 * acc_ref[...]
    if final:
        x2 = _rms(x2, nf_ref[...])
    out_ref[...] = x2


CAST_BLOCKS = 16


def _post(x_src, mixer_in, mod, w_o16, norm_w, w_in16, w_out16, geo, gla, norm_final, n_tiles, o_norm=None,
          cast_jobs=()):
    hidden, d = w_out16.shape
    assert hidden % FFN_CHUNK == 0
    tm = ROW_TILE
    final = norm_final is not None
    row = lambda i: (i, 0)
    args, in_specs = _x_specs(x_src, geo[0])
    n_x = len(args)
    if gla:
        o_f, o_b, r = mixer_in
        args += [o_f, o_b, r, jnp.tile(o_norm, GLA_HEADS).reshape(1, d)]
        in_specs += [pl.BlockSpec((tm, d), row)] * 3 + [_const_spec((1, d))]
    else:
        args += [mixer_in]
        in_specs += [pl.BlockSpec((tm, d), row)]
    args += [mod, w_o16, norm_w.reshape(1, d), w_in16, w_out16]
    in_specs += [pl.BlockSpec((1, 1, 6 * d), _group_map(*geo)), _const_spec(w_o16.shape), _const_spec((1, d)),
                 _const_spec(w_in16.shape), _const_spec(w_out16.shape)]
    if final:
        args += [norm_final.reshape(1, d)]
        in_specs += [_const_spec((1, d))]
    out_specs = [pl.BlockSpec((tm, d), row)]
    out_shape = [jax.ShapeDtypeStruct((n_tiles * tm, d), f32)]
    n_blocks = CAST_BLOCKS
    while n_blocks > n_tiles:
        n_blocks //= 2
    last_block = n_blocks - 1
    for stacked, layer in cast_jobs:
        _, rows, cols = stacked.shape
        assert rows % (16 * n_blocks) == 0
        br = rows // n_blocks
        args.append(stacked)
        in_specs.append(pl.BlockSpec((None, br, cols),
                                     lambda i, layer=layer: (layer, jnp.minimum(i, last_block), 0)))
        out_specs.append(pl.BlockSpec((br, cols), lambda i: (jnp.minimum(i, last_block), 0)))
        out_shape.append(jax.ShapeDtypeStruct((rows, cols), bf16))
    outs = pl.pallas_call(
        functools.partial(_post_kernel, d=d, gla=gla, final=final, hidden=hidden, n_x=n_x,
                          n_lat_tiles=geo[0], n_cast=len(cast_jobs)),
        grid=(n_tiles,),
        in_specs=in_specs,
        out_specs=out_specs,
        out_shape=out_shape,
        scratch_shapes=[pltpu.VMEM((tm, d), f32)],
        compiler_params=pltpu.CompilerParams(dimension_semantics=("arbitrary",),
                                             vmem_limit_bytes=VMEM_LIMIT),
        name="post_gla" if gla else "post_swa",
    )(*args)
    return outs[0], list(outs[1:])


def kernel(x, c, ctx, c_ctx, w_ada, b_ada, norm_mix, norm_ffn, gla_w_in, gla_w_gate_up, gla_b_gate_up,
           gla_w_o, gla_o_norm, swa_w_qkv, swa_b_qkv, swa_w_o, swa_sinks, ffn_w_in, ffn_w_out,
           norm_final):
    n_batch, seq, d = x.shape
    ctx_len = ctx.shape[1]
    depth = w_ada.shape[0]
    tm = ROW_TILE
    assert seq % tm == 0 and (n_batch * ctx_len) == tm
    n_lat_tiles = n_batch * seq // tm
    geo = (n_lat_tiles, seq // tm, n_batch)

    mod_all = _ada_mod(jnp.concatenate([c, c_ctx[None, :]], axis=0), w_ada, b_ada)
    t_all = n_batch * (seq + ctx_len)
    x_src = (x.reshape(n_batch * seq, d), ctx.reshape(n_batch * ctx_len, d))
    cos_l, sin_l = _rope_tables(seq, tm)

    def mixer_jobs(layer):
        jj = layer // 2
        mix = [(gla_w_o, jj)] if layer % 2 == 0 else [(swa_w_qkv, jj), (swa_w_o, jj)]
        return mix + [(ffn_w_in, layer), (ffn_w_out, layer)]

    w16 = [stacked[layer].astype(bf16) for stacked, layer in mixer_jobs(0)]
    for i in range(depth):
        last = i == depth - 1
        j = i // 2
        mod = mod_all[i].reshape(8, 1, 6 * d)
        n_tiles = n_lat_tiles if last else n_lat_tiles + 1
        nf = norm_final if last else None
        jobs = () if last else mixer_jobs(i + 1)
        w_o16, w_ffn_in16, w_ffn_out16 = w16[-3:]
        if i % 2 == 0:
            qk, r, vt, low = _gla_proj(x_src, t_all, mod, norm_mix[i], gla_w_in[j].astype(bf16), geo)
            o_f, o_b = _gla_scan(qk, vt, low, gla_w_gate_up[j], gla_b_gate_up[j], n_batch, seq, ctx_len)
            x_src, w16 = _post(x_src, (o_f, o_b, r), mod, w_o16, norm_ffn[i], w_ffn_in16, w_ffn_out16,
                               geo, True, nf, n_tiles, o_norm=gla_o_norm[j], cast_jobs=jobs)
        else:
            q, kx, vx = _swa_proj(x_src, mod, norm_mix[i], w16[0], swa_b_qkv[j], cos_l, sin_l, geo)
            attn = _swa_attention(q, kx, vx, swa_sinks[j], n_batch, seq, ctx_len, not last)
            x_src, w16 = _post(x_src, attn, mod, w_o16, norm_ffn[i], w_ffn_in16, w_ffn_out16,
                               geo, False, nf, n_tiles, cast_jobs=jobs)
    return x_src.reshape(n_batch, seq, d)
```

```python
import functools

import jax
import jax.numpy as jnp
from jax import lax
from jax.experimental import pallas as pl
from jax.experimental.pallas import tpu as pltpu

f32 = jnp.float32
bf16 = jnp.bfloat16

EPS = 1e-6
LOG2E = 1.4426950408889634
NEG_INF = -1e30
GRID_W = 64
ROPE_THETA = 10000.0

GLA_HEADS = 4
GLA_GATE_RANK = 16
GLA_GATE_NORM = 16.0
GLA_CHUNK = 64
GLA_BLOCK = 256
SCAN_SKEW_GROUP = 4

SWA_HEADS = 16
SWA_KV_HEADS = 2
SWA_HD = 64
SWA_WINDOW = 128
SWA_BLOCK = 128
ATTN_STACK = 1

ATTN_AHEAD = 2

LANES = 128
FFN_CHUNK = 256
ROW_TILE = 512
SUB_TILES = 2
VMEM_LIMIT = 56 * 1024 * 1024


def _nt(a, b):
    return lax.dot_general(a, b, (((1,), (1,)), ((), ())), preferred_element_type=f32)


def _nn(a, b):
    return jnp.dot(a, b, preferred_element_type=f32)


def _silu(v):
    return v * (1.0 / (1.0 + jnp.exp(-v)))


def _rms(v, w):
    return v * lax.rsqrt(jnp.mean(v * v, axis=-1, keepdims=True) + EPS) * w


def _const_spec(shape):
    zeros = (0,) * len(shape)
    return pl.BlockSpec(shape, lambda *_: zeros, pipeline_mode=pl.Buffered(1))


def _ada_kernel(cb_ref, w_ref, b_ref, out_ref, *, n_rows, d_model, tn):
    ncol = tn // LANES

    def body(i, accs):
        d0 = pl.multiple_of(i * 8, 8)
        w8 = w_ref[0, pl.ds(d0, 8), :]
        new = []
        for r in range(n_rows):
            s8 = _silu(cb_ref[r, pl.ds(d0, 8), :])
            new.append(tuple(accs[r][j] + w8[:, j * LANES:(j + 1) * LANES] * s8 for j in range(ncol)))
        return tuple(new)

    init = tuple(tuple(jnp.zeros((8, LANES), f32) for _ in range(ncol)) for _ in range(n_rows))
    accs = lax.fori_loop(0, d_model // 8, body, init, unroll=8)
    rows = [jnp.concatenate([jnp.sum(a, axis=0, keepdims=True) for a in accs[r]], axis=1)
            for r in range(n_rows)]
    rows += [jnp.zeros((1, tn), f32)] * (8 - n_rows)
    out_ref[0] = jnp.concatenate(rows, axis=0) + b_ref[0]


def _ada_mod(cvecs, w_ada, b_ada):
    n_rows, d_model = cvecs.shape
    depth, _, e = w_ada.shape
    tn = 1536
    cb = jnp.broadcast_to(cvecs[:, :, None], (n_rows, d_model, LANES))
    return pl.pallas_call(
        functools.partial(_ada_kernel, n_rows=n_rows, d_model=d_model, tn=tn),
        grid=(depth, e // tn),
        in_specs=[
            pl.BlockSpec((n_rows, d_model, LANES), lambda l, j: (0, 0, 0)),
            pl.BlockSpec((1, d_model, tn), lambda l, j: (l, 0, j)),
            pl.BlockSpec((1, 1, tn), lambda l, j: (l, 0, j)),
        ],
        out_specs=pl.BlockSpec((1, 8, tn), lambda l, j: (l, 0, j)),
        out_shape=jax.ShapeDtypeStruct((depth, 8, e), f32),
        compiler_params=pltpu.CompilerParams(dimension_semantics=("arbitrary", "arbitrary")),
        name="ada_mod",
    )(cb, w_ada, b_ada.reshape(depth, 1, e))


def _mod_slices(mod_ref, d, first):
    return tuple(mod_ref[0, :, (first + k) * d:(first + k + 1) * d] for k in range(3))


def _x_specs(x_src, n_lat_tiles):
    tm = ROW_TILE
    if isinstance(x_src, tuple):
        lat, ctx = x_src
        d = lat.shape[1]
        assert ctx.shape == (tm, d)
        return [lat, ctx], [pl.BlockSpec((tm, d), lambda i: (jnp.minimum(i, n_lat_tiles - 1), 0)),
                            pl.BlockSpec((tm, d), lambda i: (0, 0))]
    return [x_src], [pl.BlockSpec((tm, x_src.shape[1]), lambda i: (i, 0))]


def _load_x(x_refs, n_lat_tiles, rs=slice(None)):
    if len(x_refs) == 2:
        return jnp.where(pl.program_id(0) < n_lat_tiles, x_refs[0][rs, :], x_refs[1][rs, :])
    return x_refs[0][rs, :]


def _sub_tiles(rows):
    step = rows // SUB_TILES
    return [slice(s * step, (s + 1) * step) for s in range(SUB_TILES)]


def _group_map(n_lat_tiles, tiles_per_batch, n_batch):
    def index_map(i):
        return (jnp.where(i < n_lat_tiles, i // tiles_per_batch, n_batch), 0, 0)
    return index_map


CAST_BLOCKS = 16


def _cast_specs(cast_jobs, n_steps):
    n_blocks = CAST_BLOCKS
    while n_blocks > n_steps:
        n_blocks //= 2
    last_block = n_blocks - 1
    args, in_specs, out_specs, out_shape = [], [], [], []
    for stacked, layer in cast_jobs:
        _, rows, cols = stacked.shape
        assert rows % (16 * n_blocks) == 0
        br = rows // n_blocks
        args.append(stacked)
        in_specs.append(pl.BlockSpec((None, br, cols),
                                     lambda i, layer=layer: (layer, jnp.minimum(i, last_block), 0)))
        out_specs.append(pl.BlockSpec((br, cols), lambda i: (jnp.minimum(i, last_block), 0)))
        out_shape.append(jax.ShapeDtypeStruct((rows, cols), bf16))
    return args, in_specs, out_specs, out_shape


def _run_casts(cast_in, cast_out):
    for src_ref, dst_ref in zip(cast_in, cast_out):
        dst_ref[...] = src_ref[...].astype(bf16)


def _gla_proj_kernel(*refs, d, dk, dv, heads, n_x, n_lat_tiles, n_cast):
    x_refs, (mod_ref, nw_ref, win_ref), rest = refs[:n_x], refs[n_x:n_x + 3], refs[n_x + 3:]
    cast_in, (qk_ref, r_ref, vt_ref, low_ref), rest = rest[:n_cast], rest[n_cast:n_cast + 4], rest[n_cast + 4:]
    cast_out, wvt_ref = rest[:n_cast], rest[n_cast]
    _run_casts(cast_in, cast_out)

    @pl.when(pl.program_id(0) == 0)
    def _():
        wvt_ref[...] = win_ref[:, 2 * dk:2 * dk + dv].astype(f32).T.astype(bf16)

    sh, sc, _ = _mod_slices(mod_ref, d, 0)
    hk = dk // heads
    subs = _sub_tiles(r_ref.shape[0])
    hs = [(_rms(_load_x(x_refs, n_lat_tiles, rs), nw_ref[...]) * (1.0 + sc) + sh).astype(bf16) for rs in subs]
    for rs, h in zip(subs, hs):
        low_ref[rs, :] = _nn(h, win_ref[:, 2 * dk + 2 * dv:]).astype(bf16)
        r_ref[rs, :] = _silu(_nn(h, win_ref[:, 2 * dk + dv:2 * dk + 2 * dv])).astype(bf16)
        qk = _nn(h, win_ref[:, :2 * dk])
        q = qk[:, :dk] * (hk ** -0.5)
        for hh in range(heads):
            qk_ref[hh, rs, :] = q[:, hh * hk:(hh + 1) * hk].astype(bf16)
            qk_ref[heads + hh, rs, :] = qk[:, dk + hh * hk:dk + (hh + 1) * hk].astype(bf16)
        vt_ref[:, rs] = _nt(wvt_ref[...], h).astype(bf16)


def _gla_proj(x_src, t_all, mod, norm_w, w_in16, geo, cast_jobs=()):
    d = w_in16.shape[0]
    dk, dv, heads = d // 2, d, GLA_HEADS
    hk = dk // heads
    tm = ROW_TILE
    x_args, x_specs = _x_specs(x_src, geo[0])
    c_args, c_in_specs, c_out_specs, c_out_shape = _cast_specs(cast_jobs, t_all // tm)
    kern = functools.partial(_gla_proj_kernel, d=d, dk=dk, dv=dv, heads=heads, n_x=len(x_args),
                             n_lat_tiles=geo[0], n_cast=len(cast_jobs))
    outs = pl.pallas_call(
        kern,
        grid=(t_all // tm,),
        in_specs=x_specs + [
            pl.BlockSpec((1, 1, 6 * d), _group_map(*geo)),
            _const_spec((1, d)),
            _const_spec(w_in16.shape),
        ] + c_in_specs,
        out_specs=[
            pl.BlockSpec((2 * heads, tm, hk), lambda i: (0, i, 0)),
            pl.BlockSpec((tm, dv), lambda i: (i, 0)),
            pl.BlockSpec((dv, tm), lambda i: (0, i)),
            pl.BlockSpec((tm, 2 * GLA_GATE_RANK), lambda i: (i, 0)),
        ] + c_out_specs,
        out_shape=[
            jax.ShapeDtypeStruct((2 * heads, t_all, hk), bf16),
            jax.ShapeDtypeStruct((t_all, dv), bf16),
            jax.ShapeDtypeStruct((dv, t_all), bf16),
            jax.ShapeDtypeStruct((t_all, 2 * GLA_GATE_RANK), bf16),
        ] + c_out_shape,
        scratch_shapes=[pltpu.VMEM((dv, d), bf16)],
        compiler_params=pltpu.CompilerParams(dimension_semantics=("arbitrary",),
                                             vmem_limit_bytes=VMEM_LIMIT),
        name="gla_proj",
    )(*x_args, mod, norm_w.reshape(1, d), w_in16, *c_args)
    return outs[:4], list(outs[4:])


def _gla_block(q, k, vt, g, s_ref, o_ref, reverse):
    tb, hk = q.shape
    c = GLA_CHUNK
    half = tb // 2
    assert tb == 4 * c
    row = lax.broadcasted_iota(jnp.int32, (tb, tb), 0)
    col = lax.broadcasted_iota(jnp.int32, (tb, tb), 1)
    causal = ((row // c) == (col // c)) & ((col >= row) if reverse else (col <= row))
    cum = jnp.where(causal, 1.0, 0.0).astype(bf16)
    g_hi = g.astype(bf16)
    g_lo = (g - g_hi.astype(f32)).astype(bf16)
    b2 = _nn(cum, jnp.concatenate([g_hi, g_lo], axis=1))
    yield
    b = b2[:, :hk] + b2[:, hk:]
    order = (3, 2, 1, 0) if reverse else (0, 1, 2, 3)
    pos = {ch: s for s, ch in enumerate(order)}
    tot = []
    for ch in range(4):
        last = ch * c if reverse else ch * c + c - 1
        tot.append(b[last:last + 1, :])
    zero = jnp.zeros_like(tot[0])
    before = [sum((tot[order[s2]] for s2 in range(pos[ch])), zero) for ch in range(4)]
    after = [sum((tot[order[s2]] for s2 in range(pos[ch] + 1, 4)), zero) for ch in range(4)]

    def rows(vals):
        return jnp.concatenate([jnp.broadcast_to(v, (c, hk)) for v in vals], axis=0)

    b_tot = rows(tot)
    kf = k.astype(f32)
    qd = q.astype(f32) * jnp.exp2(b)
    ki = kf * jnp.exp2(-b)
    ke = kf * jnp.exp2(b_tot - b)
    qd16, ki16, ke16 = qd.astype(bf16), ki.astype(bf16), ke.astype(bf16)
    yield

    lrow = lax.broadcasted_iota(jnp.int32, (half, half), 0)
    lcol = lax.broadcasted_iota(jnp.int32, (half, half), 1)
    m_same = ((lrow // c) == (lcol // c)) & ((lcol >= lrow) if reverse else (lcol <= lrow))
    m_next = ((lrow < c) & (lcol >= c)) if reverse else ((lrow >= c) & (lcol < c))
    xs = [_nt(qd16[lo:lo + half], jnp.concatenate([ki16[lo:lo + half], ke16[lo:lo + half]], axis=0))
          for lo in (0, half)]

    late, early = (0, half) if reverse else (half, 0)
    q_far = order[3] * c - late
    k_far = order[0] * c - early
    q_scale = [zero, zero]
    q_scale[q_far // c] = tot[order[2]]
    k_scale = [zero, zero]
    k_scale[k_far // c] = tot[order[1]]
    q2 = (qd[late:late + half] * jnp.exp2(rows(q_scale))).astype(bf16)
    k2 = (ke[early:early + half] * jnp.exp2(rows(k_scale))).astype(bf16)
    cross_f32 = _nt(q2, k2)
    yield
    cross = cross_f32.astype(bf16)

    diag = [jnp.where(m_same, x[:, :half], jnp.where(m_next, x[:, half:], 0.0)).astype(bf16) for x in xs]
    if reverse:
        p_early = diag[1]
        p_late = jnp.concatenate([diag[0], cross], axis=1)
    else:
        p_early = diag[0]
        p_late = jnp.concatenate([cross, diag[1]], axis=1)
    s = s_ref[...]
    inter = _nt((qd * jnp.exp2(rows(before))).astype(bf16), s.astype(bf16))
    pv_early = _nt(p_early, vt[:, early:early + half])
    pv_late = _nt(p_late, vt)
    k_end = (ke * jnp.exp2(rows(after))).astype(bf16)
    upd = _nn(vt, k_end)
    yield
    o_ref[early:early + half, :] = (pv_early + inter[early:early + half]).astype(o_ref.dtype)
    o_ref[late:late + half, :] = (pv_late + inter[late:late + half]).astype(o_ref.dtype)
    s_ref[...] = s * jnp.exp2(before[order[3]] + tot[order[3]]) + upd


def _gla_scan_kernel(qkf_ref, vtf_ref, lowf_ref, qkb_ref, vtb_ref, lowb_ref, wup_ref, bup_ref,
                     of_ref, ob_ref, s_ref, *, heads):
    @pl.when(pl.program_id(1) == 0)
    def _():
        s_ref[...] = jnp.zeros_like(s_ref)

    hv = vtf_ref.shape[0] // heads
    hk = qkf_ref.shape[2]
    dk = heads * hk

    def log_decay(low_ref, direction):
        cols = slice(direction * dk, (direction + 1) * dk)
        z2 = _nn(low_ref[...], wup_ref[:, cols]) + bup_ref[:, cols]
        return (jnp.minimum(z2, 0.0) - jnp.log2(1.0 + jnp.exp2(-jnp.abs(z2)))) * (1.0 / GLA_GATE_NORM)

    g_f = log_decay(lowf_ref, 0)
    g_b = log_decay(lowb_ref, 1)
    chains = []
    for hh in range(heads):
        cols = slice(hh * hv, (hh + 1) * hv)
        gcols = slice(hh * hk, (hh + 1) * hk)
        chains.append(_gla_block(qkf_ref[hh], qkf_ref[heads + hh], vtf_ref[cols, :], g_f[:, gcols],
                                 s_ref.at[0, hh], of_ref.at[:, cols], False))
        chains.append(_gla_block(qkb_ref[hh], qkb_ref[heads + hh], vtb_ref[cols, :], g_b[:, gcols],
                                 s_ref.at[1, hh], ob_ref.at[:, cols], True))
    groups = [chains[k:k + SCAN_SKEW_GROUP] for k in range(0, len(chains), SCAN_SKEW_GROUP)]
    tick = 0
    while any(groups):
        for k, group in enumerate(groups):
            if tick < k:
                continue
            alive = []
            for chain in group:
                try:
                    next(chain)
                    alive.append(chain)
                except StopIteration:
                    pass
            groups[k] = alive
        tick += 1


def _gla_scan(qk, vt, low, w_gate_up, b_gate_up, n_batch, seq, ctx_len):
    heads = GLA_HEADS
    _, t_all, hk = qk.shape
    dk = heads * hk
    dv = vt.shape[0]
    hv = dv // heads
    rank = GLA_GATE_RANK
    tb = GLA_BLOCK
    assert ctx_len == tb and seq % tb == 0
    nlat = seq // tb
    ctx0 = n_batch * nlat
    wup = jnp.zeros((2 * rank, 2 * dk), f32)
    wup = (wup.at[:rank, :dk].set(w_gate_up[0]).at[rank:, dk:].set(w_gate_up[1]) * LOG2E).astype(bf16)
    bup = b_gate_up.reshape(1, 2 * dk) * LOG2E

    def fwd_row(b, t):
        return jnp.where(t == 0, ctx0 + b, b * nlat + t - 1)

    def bwd_row(b, t):
        return jnp.where(t == 0, ctx0 + b, b * nlat + nlat - t)

    def specs(row_fn):
        return [
            pl.BlockSpec((2 * heads, tb, hk), lambda b, t: (0, row_fn(b, t), 0)),
            pl.BlockSpec((dv, tb), lambda b, t: (0, row_fn(b, t))),
            pl.BlockSpec((tb, 2 * rank), lambda b, t: (row_fn(b, t), 0)),
        ]

    return pl.pallas_call(
        functools.partial(_gla_scan_kernel, heads=heads),
        grid=(n_batch, nlat + 1),
        in_specs=specs(fwd_row) + specs(bwd_row) + [_const_spec((2 * rank, 2 * dk)), _const_spec((1, 2 * dk))],
        out_specs=[
            pl.BlockSpec((tb, dv), lambda b, t: (fwd_row(b, t), 0)),
            pl.BlockSpec((tb, dv), lambda b, t: (bwd_row(b, t), 0)),
        ],
        out_shape=[jax.ShapeDtypeStruct((t_all, dv), bf16)] * 2,
        scratch_shapes=[pltpu.VMEM((2, heads, hv, hk), f32)],
        compiler_params=pltpu.CompilerParams(dimension_semantics=("arbitrary", "arbitrary"),
                                             vmem_limit_bytes=VMEM_LIMIT),
        name="gla_scan",
    )(qk, vt, low, qk, vt, low, wup, bup)


def _swa_proj_kernel(x_ref, mod_ref, nw_ref, w_ref, b_ref, cos_ref, sin_ref,
                     q_ref, kx_ref, vx_ref, *, d, nq):
    sh, sc, _ = _mod_slices(mod_ref, d, 0)
    subs = _sub_tiles(x_ref.shape[0])
    hs = [(_rms(x_ref[rs, :], nw_ref[...]) * (1.0 + sc) + sh).astype(bf16) for rs in subs]
    half = SWA_HD // 2
    for rs, h in zip(subs, hs):
        qkv = _nn(h, w_ref[...]) + b_ref[...]
        cos = cos_ref[rs, :]
        sin = sin_ref[rs, :]
        lane = lax.broadcasted_iota(jnp.int32, cos.shape, 1)
        first_half = (lane % SWA_HD) < half
        low_head = lane < SWA_HD

        def rope(v):
            partner = jnp.where(first_half, pltpu.roll(v, LANES - half, 1), pltpu.roll(v, half, 1))
            return v * cos + partner * sin

        for j in range(nq // LANES):
            q_ref[rs, j * LANES:(j + 1) * LANES] = (
                rope(qkv[:, j * LANES:(j + 1) * LANES]) * (SWA_HD ** -0.5 * LOG2E)).astype(bf16)
        k = rope(qkv[:, nq:nq + LANES])
        v = qkv[:, nq + LANES:nq + 2 * LANES]
        for src_val, dst in ((k, kx_ref), (v, vx_ref)):
            swapped = pltpu.roll(src_val, SWA_HD, 1)
            dst[0, rs, :] = jnp.where(low_head, src_val, 0.0).astype(bf16)
            dst[1, rs, :] = jnp.where(low_head, 0.0, swapped).astype(bf16)
            dst[2, rs, :] = jnp.where(low_head, swapped, 0.0).astype(bf16)
            dst[3, rs, :] = jnp.where(low_head, 0.0, src_val).astype(bf16)


def _rope_tables(seq, extra_rows):
    rows = seq // GRID_W
    row = jnp.broadcast_to(jnp.arange(rows)[:, None], (rows, GRID_W)).reshape(-1).astype(f32)
    col = jnp.broadcast_to(jnp.arange(GRID_W)[None, :], (rows, GRID_W)).reshape(-1).astype(f32)
    n_freq = SWA_HD // 4
    inv = ROPE_THETA ** (-jnp.arange(n_freq, dtype=f32) / n_freq)
    ang = jnp.concatenate([row[:, None] * inv, col[:, None] * inv], axis=-1)
    cos, sin = jnp.cos(ang), jnp.sin(ang)
    reps = LANES // SWA_HD
    cos_l = jnp.tile(jnp.concatenate([cos, cos], axis=-1), (1, reps))
    sin_l = jnp.tile(jnp.concatenate([-sin, sin], axis=-1), (1, reps))
    cos_l = jnp.concatenate([cos_l, jnp.ones((extra_rows, LANES), f32)], axis=0)
    sin_l = jnp.concatenate([sin_l, jnp.zeros((extra_rows, LANES), f32)], axis=0)
    return cos_l, sin_l


def _swa_proj(x_all, mod, norm_w, w_qkv16, b_qkv, cos_l, sin_l, geo):
    t_all, d = x_all.shape
    nq = SWA_HEADS * SWA_HD
    nqkv = w_qkv16.shape[1]
    tm = ROW_TILE
    n_lat_tiles, tiles_per_batch, _ = geo

    def pos_map(i):
        return (jnp.where(i < n_lat_tiles, i % tiles_per_batch, tiles_per_batch), 0)

    return pl.pallas_call(
        functools.partial(_swa_proj_kernel, d=d, nq=nq),
        grid=(t_all // tm,),
        in_specs=[
            pl.BlockSpec((tm, d), lambda i: (i, 0)),
            pl.BlockSpec((1, 1, 6 * d), _group_map(*geo)),
            _const_spec((1, d)),
            _const_spec(w_qkv16.shape),
            _const_spec((1, nqkv)),
            pl.BlockSpec((tm, LANES), pos_map),
            pl.BlockSpec((tm, LANES), pos_map),
        ],
        out_specs=[
            pl.BlockSpec((tm, nq), lambda i: (i, 0)),
            pl.BlockSpec((4, tm, LANES), lambda i: (0, i, 0)),
            pl.BlockSpec((4, tm, LANES), lambda i: (0, i, 0)),
        ],
        out_shape=[
            jax.ShapeDtypeStruct((t_all, nq), bf16),
            jax.ShapeDtypeStruct((4, t_all, LANES), bf16),
            jax.ShapeDtypeStruct((4, t_all, LANES), bf16),
        ],
        compiler_params=pltpu.CompilerParams(dimension_semantics=("arbitrary",),
                                             vmem_limit_bytes=VMEM_LIMIT),
        name="swa_proj",
    )(x_all, mod, norm_w.reshape(1, d), w_qkv16, b_qkv.reshape(1, nqkv), cos_l, sin_l)


def _attend(q_ref, o_ref, sink_ref, k_pieces, v_pieces, edge_masks):
    tq = q_ref.shape[0]
    pairs = q_ref.shape[1] // LANES
    group = pairs // SWA_KV_HEADS
    stack = ATTN_STACK
    lane = lax.broadcasted_iota(jnp.int32, (tq, LANES), 1)
    low_head = lane < SWA_HD
    k_cat, v_cat = [], []
    for g in range(SWA_KV_HEADS):
        k_cat.append(jnp.concatenate(k_pieces[g][0] + k_pieces[g][1], axis=0))
        nk = k_cat[g].shape[0] // 2
        lane2 = lax.broadcasted_iota(jnp.int32, (nk, LANES), 1)
        ones_lo = jnp.where(lane2 < SWA_HD, 1.0, 0.0).astype(bf16)
        ones_hi = jnp.where(lane2 < SWA_HD, 0.0, 1.0).astype(bf16)
        v_cat.append(jnp.concatenate([
            jnp.concatenate([jnp.concatenate(v_pieces[g][0], axis=0), ones_lo], axis=1),
            jnp.concatenate([jnp.concatenate(v_pieces[g][1], axis=0), ones_hi], axis=1)], axis=0))
    units = [(u * stack) // group for u in range(pairs // stack)]

    def scores(u):
        q_stack = jnp.concatenate(
            [q_ref[:, (u * stack + jj) * LANES:(u * stack + jj + 1) * LANES] for jj in range(stack)], axis=0)
        return _nt(q_stack, k_cat[units[u]])

    ahead = [scores(u) for u in range(min(ATTN_AHEAD, len(units)))]
    for u, g in enumerate(units):
        s_all = ahead.pop(0)
        if u + ATTN_AHEAD < len(units):
            ahead.append(scores(u + ATTN_AHEAD))
        prob_rows, sink_rows = [], []
        for jj in range(stack):
            j = u * stack + jj
            probs, sink_terms = [], []
            for p in range(2):
                sp = s_all[jj * tq:(jj + 1) * tq, p * nk:(p + 1) * nk]
                if edge_masks is not None:
                    before, after = edge_masks
                    blk = before.shape[1]
                    sp = jnp.concatenate([jnp.where(before, sp[:, :blk], NEG_INF), sp[:, blk:2 * blk],
                                          jnp.where(after, sp[:, 2 * blk:3 * blk], NEG_INF), sp[:, 3 * blk:]],
                                         axis=1)
                sink = sink_ref[2 * j + p] * LOG2E
                m = jnp.maximum(jnp.max(sp, axis=-1, keepdims=True), sink)
                probs.append(jnp.exp2(sp - m).astype(bf16))
                sink_terms.append(jnp.exp2(sink - m))
            prob_rows.append(jnp.concatenate(probs, axis=1))
            sink_rows.append(jnp.where(low_head, sink_terms[0], sink_terms[1]))
        out_all = _nn(jnp.concatenate(prob_rows, axis=0), v_cat[g])
        for jj in range(stack):
            j = u * stack + jj
            out = out_all[jj * tq:(jj + 1) * tq]
            den = out[:, LANES:] + sink_rows[jj]
            o_ref[:, j * LANES:(j + 1) * LANES] = (out[:, :LANES] / den).astype(bf16)
        yield


def _trace_in_order(chains):
    for chain in chains:
        for _ in chain:
            pass


def _swa_kernel(sink_ref, q_ref, kp_ref, ko_ref, kn_ref, kc_ref, vp_ref, vo_ref, vn_ref, vc_ref,
                o_ref, *, seq, n_lat_steps, need_ctx):
    i = pl.program_id(1)
    blk = SWA_BLOCK

    def pieces(refs_of_half):
        return [[[load(2 * g + p) for load in refs_of_half] for p in range(2)] for g in range(SWA_KV_HEADS)]

    def rows(ref, lo):
        return lambda idx: ref[idx, lo:lo + blk, :]

    @pl.when(i < n_lat_steps)
    def _():
        halves = []
        for half in range(2):
            first = 2 * i + half
            qi = first * blk + lax.broadcasted_iota(jnp.int32, (blk, blk), 0)
            local = lax.broadcasted_iota(jnp.int32, (blk, blk), 1)
            kj_before = (first - 1) * blk + local
            kj_after = (first + 1) * blk + local
            masks = ((qi - kj_before <= SWA_WINDOW) & (kj_before >= 0),
                     (kj_after - qi <= SWA_WINDOW) & (kj_after < seq))
            if half == 0:
                k_loads = (rows(kp_ref, 0), rows(ko_ref, 0), rows(ko_ref, blk), lambda idx: kc_ref[idx])
                v_loads = (rows(vp_ref, 0), rows(vo_ref, 0), rows(vo_ref, blk), lambda idx: vc_ref[idx])
            else:
                k_loads = (rows(ko_ref, 0), rows(ko_ref, blk), rows(kn_ref, 0), lambda idx: kc_ref[idx])
                v_loads = (rows(vo_ref, 0), rows(vo_ref, blk), rows(vn_ref, 0), lambda idx: vc_ref[idx])
            rs = slice(half * blk, (half + 1) * blk)
            halves.append(_attend(q_ref.at[rs, :], o_ref.at[rs, :], sink_ref, pieces(k_loads), pieces(v_loads),
                                  masks))
        _trace_in_order(halves)

    if need_ctx:
        @pl.when(i >= n_lat_steps)
        def _():
            halves = []
            for half in range(2):
                rs = slice(half * blk, (half + 1) * blk)
                halves.append(_attend(q_ref.at[rs, :], o_ref.at[rs, :], sink_ref, pieces((lambda idx: kc_ref[idx],)),
                                      pieces((lambda idx: vc_ref[idx],)), None))
            _trace_in_order(halves)


def _swa_attention(q, kx, vx, sinks, n_batch, seq, ctx_len, need_ctx):
    t_all, nq = q.shape
    blk = SWA_BLOCK
    step = 2 * blk
    assert seq % step == 0 and ctx_len % step == 0
    nb = seq // blk
    n_lat_steps = seq // step
    n_ctx_steps = ctx_len // step
    out_rows = t_all if need_ctx else n_batch * seq
    smem = pl.BlockSpec(memory_space=pltpu.SMEM)

    def edge(off):
        return pl.BlockSpec((4, blk, LANES),
                            lambda b, i: (0, b * nb + jnp.clip(2 * i + off, 0, nb - 1), 0))

    def own():
        return pl.BlockSpec((4, step, LANES),
                            lambda b, i: (0, b * n_lat_steps + jnp.minimum(i, n_lat_steps - 1), 0))

    def ctx_spec():
        return pl.BlockSpec((4, ctx_len, LANES), lambda b, i: (0, (n_batch * seq) // ctx_len + b, 0))

    def q_row(b, i):
        return (jnp.where(i < n_lat_steps, b * n_lat_steps + i,
                          n_batch * n_lat_steps + b * n_ctx_steps + i - n_lat_steps), 0)

    return pl.pallas_call(
        functools.partial(_swa_kernel, seq=seq, n_lat_steps=n_lat_steps, need_ctx=need_ctx),
        grid=(n_batch, n_lat_steps + n_ctx_steps if need_ctx else n_lat_steps),
        in_specs=[smem, pl.BlockSpec((step, nq), q_row),
                  edge(-1), own(), edge(2), ctx_spec(), edge(-1), own(), edge(2), ctx_spec()],
        out_specs=pl.BlockSpec((step, nq), q_row),
        out_shape=jax.ShapeDtypeStruct((out_rows, nq), bf16),
        compiler_params=pltpu.CompilerParams(dimension_semantics=("arbitrary", "arbitrary"),
                                             vmem_limit_bytes=VMEM_LIMIT),
        name="swa_attn",
    )(sinks, q, kx, kx, kx, kx, vx, vx, vx, vx)


def _post_kernel(*refs, d, gla, final, hidden, n_x, n_lat_tiles, n_cast):
    x_refs, refs = refs[:n_x], refs[n_x:]
    if gla:
        of_ref, ob_ref, r_ref, on_ref = refs[:4]
        rest = refs[4:]
    else:
        a_ref = refs[0]
        rest = refs[1:]
    mod_ref, wo_ref, nw_ref, win_ref, wout_ref = rest[:5]
    rest = rest[5:]
    if final:
        nf_ref, rest = rest[0], rest[1:]
    cast_in, out_ref, cast_out = rest[:n_cast], rest[n_cast], rest[n_cast + 1:]
    _run_casts(cast_in, cast_out)

    _, _, g1 = _mod_slices(mod_ref, d, 0)
    sh2, sc2, g2 = _mod_slices(mod_ref, d, 3)

    def mixer_residual(rs):
        if gla:
            o = of_ref[rs, :].astype(f32) + ob_ref[rs, :].astype(f32)
            hv = d // GLA_HEADS
            parts = []
            for hh in range(GLA_HEADS):
                seg = o[:, hh * hv:(hh + 1) * hv]
                parts.append(_rms(seg, on_ref[:, hh * hv:(hh + 1) * hv]))
            a = (jnp.concatenate(parts, axis=1) * r_ref[rs, :].astype(f32)).astype(bf16)
        else:
            a = a_ref[rs, :]
        x1 = _load_x(x_refs, n_lat_tiles, rs) + g1 * _nn(a, wo_ref[...])
        return x1, (_rms(x1, nw_ref[...]) * (1.0 + sc2) + sh2).astype(bf16)

    subs = _sub_tiles(out_ref.shape[0])
    pre = [mixer_residual(rs) for rs in subs]
    for rs, (x1, hn) in zip(subs, pre):
        acts = []
        for off in range(0, hidden, FFN_CHUNK):
            gate = _nn(hn, win_ref[:, off:off + FFN_CHUNK])
            up = _nn(hn, win_ref[:, hidden + off:hidden + off + FFN_CHUNK])
            acts.append((_silu(gate) * up).astype(bf16))
        x2 = x1 + g2 * _nn(jnp.concatenate(acts, axis=1), wout_ref[...])
        if final:
            x2 = _rms(x2, nf_ref[...])
        out_ref[rs, :] = x2


def _post(x_src, mixer_in, mod, w_o16, norm_w, w_in16, w_out16, geo, gla, norm_final, n_tiles, o_norm=None,
          cast_jobs=()):
    hidden, d = w_out16.shape
    assert hidden % FFN_CHUNK == 0
    tm = ROW_TILE
    final = norm_final is not None
    row = lambda i: (i, 0)
    args, in_specs = _x_specs(x_src, geo[0])
    n_x = len(args)
    if gla:
        o_f, o_b, r = mixer_in
        args += [o_f, o_b, r, jnp.tile(o_norm, GLA_HEADS).reshape(1, d)]
        in_specs += [pl.BlockSpec((tm, d), row)] * 3 + [_const_spec((1, d))]
    else:
        args += [mixer_in]
        in_specs += [pl.BlockSpec((tm, d), row)]
    args += [mod, w_o16, norm_w.reshape(1, d), w_in16, w_out16]
    in_specs += [pl.BlockSpec((1, 1, 6 * d), _group_map(*geo)), _const_spec(w_o16.shape), _const_spec((1, d)),
                 _const_spec(w_in16.shape), _const_spec(w_out16.shape)]
    if final:
        args += [norm_final.reshape(1, d)]
        in_specs += [_const_spec((1, d))]
    c_args, c_in_specs, c_out_specs, c_out_shape = _cast_specs(cast_jobs, n_tiles)
    args += c_args
    in_specs += c_in_specs
    out_specs = [pl.BlockSpec((tm, d), row)] + c_out_specs
    out_shape = [jax.ShapeDtypeStruct((n_tiles * tm, d), f32)] + c_out_shape
    outs = pl.pallas_call(
        functools.partial(_post_kernel, d=d, gla=gla, final=final, hidden=hidden, n_x=n_x,
                          n_lat_tiles=geo[0], n_cast=len(cast_jobs)),
        grid=(n_tiles,),
        in_specs=in_specs,
        out_specs=out_specs,
        out_shape=out_shape,
        compiler_params=pltpu.CompilerParams(dimension_semantics=("arbitrary",),
                                             vmem_limit_bytes=VMEM_LIMIT),
        name="post_gla" if gla else "post_swa",
    )(*args)
    return outs[0], list(outs[1:])


def kernel(x, c, ctx, c_ctx, w_ada, b_ada, norm_mix, norm_ffn, gla_w_in, gla_w_gate_up, gla_b_gate_up,
           gla_w_o, gla_o_norm, swa_w_qkv, swa_b_qkv, swa_w_o, swa_sinks, ffn_w_in, ffn_w_out,
           norm_final):
    n_batch, seq, d = x.shape
    ctx_len = ctx.shape[1]
    depth = w_ada.shape[0]
    tm = ROW_TILE
    assert seq % tm == 0 and (n_batch * ctx_len) == tm
    n_lat_tiles = n_batch * seq // tm
    geo = (n_lat_tiles, seq // tm, n_batch)

    mod_all = _ada_mod(jnp.concatenate([c, c_ctx[None, :]], axis=0), w_ada, b_ada)
    t_all = n_batch * (seq + ctx_len)
    x_src = (x.reshape(n_batch * seq, d), ctx.reshape(n_batch * ctx_len, d))
    cos_l, sin_l = _rope_tables(seq, tm)

    def mixer_jobs(layer):
        jj = layer // 2
        mix = [(gla_w_o, jj)] if layer % 2 == 0 else [(swa_w_qkv, jj), (swa_w_o, jj)]
        return mix + [(ffn_w_in, layer), (ffn_w_out, layer)]

    w16 = None
    for i in range(depth):
        last = i == depth - 1
        j = i // 2
        mod = mod_all[i].reshape(8, 1, 6 * d)
        n_tiles = n_lat_tiles if last else n_lat_tiles + 1
        nf = norm_final if last else None
        jobs = () if last else mixer_jobs(i + 1)
        if i % 2 == 0:
            (qk, r, vt, low), first_w16 = _gla_proj(x_src, t_all, mod, norm_mix[i], gla_w_in[j].astype(bf16), geo,
                                                    cast_jobs=mixer_jobs(i) if w16 is None else ())
            w16 = first_w16 if w16 is None else w16
        w_o16, w_ffn_in16, w_ffn_out16 = w16[-3:]
        if i % 2 == 0:
            o_f, o_b = _gla_scan(qk, vt, low, gla_w_gate_up[j], gla_b_gate_up[j], n_batch, seq, ctx_len)
            x_src, w16 = _post(x_src, (o_f, o_b, r), mod, w_o16, norm_ffn[i], w_ffn_in16, w_ffn_out16,
                               geo, True, nf, n_tiles, o_norm=gla_o_norm[j], cast_jobs=jobs)
        else:
            q, kx, vx = _swa_proj(x_src, mod, norm_mix[i], w16[0], swa_b_qkv[j], cos_l, sin_l, geo)
            attn = _swa_attention(q, kx, vx, swa_sinks[j], n_batch, seq, ctx_len, not last)
            x_src, w16 = _post(x_src, attn, mod, w_o16, norm_ffn[i], w_ffn_in16, w_ffn_out16,
                               geo, False, nf, n_tiles, cast_jobs=jobs)
    return x_src.reshape(n_batch, seq, d)
```

```python
import functools

import jax
import jax.numpy as jnp
from jax import lax
from jax.experimental import pallas as pl
from jax.experimental.pallas import tpu as pltpu

f32 = jnp.float32
bf16 = jnp.bfloat16

EPS = 1e-6
LOG2E = 1.4426950408889634
NEG_INF = -1e30
GRID_W = 64
ROPE_THETA = 10000.0

GLA_HEADS = 4
GLA_GATE_RANK = 16
GLA_GATE_NORM = 16.0
GLA_CHUNK = 64
GLA_BLOCK = 256
SCAN_SKEW_GROUP = 4

SWA_HEADS = 16
SWA_KV_HEADS = 2
SWA_HD = 64
SWA_WINDOW = 128
SWA_BLOCK = 128
ATTN_STACK = 1

ATTN_AHEAD = 3
ATTN_OVERLAP = 1

LANES = 128
FFN_CHUNK = 256
ROW_TILE = 512
SUB_TILES = 2
VMEM_LIMIT = 56 * 1024 * 1024


def _nt(a, b):
    return lax.dot_general(a, b, (((1,), (1,)), ((), ())), preferred_element_type=f32)


def _nn(a, b):
    return jnp.dot(a, b, preferred_element_type=f32)


def _silu(v):
    return v * (1.0 / (1.0 + jnp.exp(-v)))


def _rms(v, w):
    return v * lax.rsqrt(jnp.mean(v * v, axis=-1, keepdims=True) + EPS) * w


def _const_spec(shape):
    zeros = (0,) * len(shape)
    return pl.BlockSpec(shape, lambda *_: zeros, pipeline_mode=pl.Buffered(1))


def _ada_kernel(cb_ref, w_ref, b_ref, out_ref, *, n_rows, d_model, tn):
    ncol = tn // LANES

    def body(i, accs):
        d0 = pl.multiple_of(i * 8, 8)
        w8 = w_ref[0, pl.ds(d0, 8), :]
        new = []
        for r in range(n_rows):
            s8 = _silu(cb_ref[r, pl.ds(d0, 8), :])
            new.append(tuple(accs[r][j] + w8[:, j * LANES:(j + 1) * LANES] * s8 for j in range(ncol)))
        return tuple(new)

    init = tuple(tuple(jnp.zeros((8, LANES), f32) for _ in range(ncol)) for _ in range(n_rows))
    accs = lax.fori_loop(0, d_model // 8, body, init, unroll=8)
    rows = [jnp.concatenate([jnp.sum(a, axis=0, keepdims=True) for a in accs[r]], axis=1)
            for r in range(n_rows)]
    rows += [jnp.zeros((1, tn), f32)] * (8 - n_rows)
    out_ref[0] = jnp.concatenate(rows, axis=0) + b_ref[0]


def _ada_mod(cvecs, w_ada, b_ada):
    n_rows, d_model = cvecs.shape
    depth, _, e = w_ada.shape
    tn = 1536
    cb = jnp.broadcast_to(cvecs[:, :, None], (n_rows, d_model, LANES))
    return pl.pallas_call(
        functools.partial(_ada_kernel, n_rows=n_rows, d_model=d_model, tn=tn),
        grid=(depth, e // tn),
        in_specs=[
            pl.BlockSpec((n_rows, d_model, LANES), lambda l, j: (0, 0, 0)),
            pl.BlockSpec((1, d_model, tn), lambda l, j: (l, 0, j)),
            pl.BlockSpec((1, 1, tn), lambda l, j: (l, 0, j)),
        ],
        out_specs=pl.BlockSpec((1, 8, tn), lambda l, j: (l, 0, j)),
        out_shape=jax.ShapeDtypeStruct((depth, 8, e), f32),
        compiler_params=pltpu.CompilerParams(dimension_semantics=("arbitrary", "arbitrary")),
        name="ada_mod",
    )(cb, w_ada, b_ada.reshape(depth, 1, e))


def _mod_slices(mod_ref, d, first):
    return tuple(mod_ref[0, :, (first + k) * d:(first + k + 1) * d] for k in range(3))


def _x_specs(x_src, n_lat_tiles):
    tm = ROW_TILE
    if isinstance(x_src, tuple):
        lat, ctx = x_src
        d = lat.shape[1]
        assert ctx.shape == (tm, d)
        return [lat, ctx], [pl.BlockSpec((tm, d), lambda i: (jnp.minimum(i, n_lat_tiles - 1), 0)),
                            pl.BlockSpec((tm, d), lambda i: (0, 0))]
    return [x_src], [pl.BlockSpec((tm, x_src.shape[1]), lambda i: (i, 0))]


def _load_x(x_refs, n_lat_tiles, rs=slice(None)):
    if len(x_refs) == 2:
        return jnp.where(pl.program_id(0) < n_lat_tiles, x_refs[0][rs, :], x_refs[1][rs, :])
    return x_refs[0][rs, :]


def _sub_tiles(rows):
    step = rows // SUB_TILES
    return [slice(s * step, (s + 1) * step) for s in range(SUB_TILES)]


def _group_map(n_lat_tiles, tiles_per_batch, n_batch):
    def index_map(i):
        return (jnp.where(i < n_lat_tiles, i // tiles_per_batch, n_batch), 0, 0)
    return index_map


CAST_BLOCKS = 16


def _cast_specs(cast_jobs, n_steps):
    n_blocks = CAST_BLOCKS
    while n_blocks > n_steps:
        n_blocks //= 2
    last_block = n_blocks - 1
    args, in_specs, out_specs, out_shape = [], [], [], []
    for stacked, layer in cast_jobs:
        _, rows, cols = stacked.shape
        assert rows % (16 * n_blocks) == 0
        br = rows // n_blocks
        args.append(stacked)
        in_specs.append(pl.BlockSpec((None, br, cols),
                                     lambda i, layer=layer: (layer, jnp.minimum(i, last_block), 0)))
        out_specs.append(pl.BlockSpec((br, cols), lambda i: (jnp.minimum(i, last_block), 0)))
        out_shape.append(jax.ShapeDtypeStruct((rows, cols), bf16))
    return args, in_specs, out_specs, out_shape


def _run_casts(cast_in, cast_out):
    for src_ref, dst_ref in zip(cast_in, cast_out):
        dst_ref[...] = src_ref[...].astype(bf16)


def _gla_proj_kernel(*refs, d, dk, dv, heads, n_x, n_lat_tiles, n_cast):
    x_refs, (mod_ref, nw_ref, win_ref), rest = refs[:n_x], refs[n_x:n_x + 3], refs[n_x + 3:]
    cast_in, (qk_ref, r_ref, vt_ref, low_ref), rest = rest[:n_cast], rest[n_cast:n_cast + 4], rest[n_cast + 4:]
    cast_out, wvt_ref = rest[:n_cast], rest[n_cast]
    _run_casts(cast_in, cast_out)

    @pl.when(pl.program_id(0) == 0)
    def _():
        wvt_ref[...] = win_ref[:, 2 * dk:2 * dk + dv].astype(f32).T.astype(bf16)

    sh, sc, _ = _mod_slices(mod_ref, d, 0)
    hk = dk // heads
    subs = _sub_tiles(r_ref.shape[0])
    hs = [(_rms(_load_x(x_refs, n_lat_tiles, rs), nw_ref[...]) * (1.0 + sc) + sh).astype(bf16) for rs in subs]
    for rs, h in zip(subs, hs):
        low_ref[rs, :] = _nn(h, win_ref[:, 2 * dk + 2 * dv:]).astype(bf16)
        r_ref[rs, :] = _silu(_nn(h, win_ref[:, 2 * dk + dv:2 * dk + 2 * dv])).astype(bf16)
        qk = _nn(h, win_ref[:, :2 * dk])
        q = qk[:, :dk] * (hk ** -0.5)
        for hh in range(heads):
            qk_ref[hh, rs, :] = q[:, hh * hk:(hh + 1) * hk].astype(bf16)
            qk_ref[heads + hh, rs, :] = qk[:, dk + hh * hk:dk + (hh + 1) * hk].astype(bf16)
        vt_ref[:, rs] = _nt(wvt_ref[...], h).astype(bf16)


def _gla_proj(x_src, t_all, mod, norm_w, w_in16, geo, cast_jobs=()):
    d = w_in16.shape[0]
    dk, dv, heads = d // 2, d, GLA_HEADS
    hk = dk // heads
    tm = ROW_TILE
    x_args, x_specs = _x_specs(x_src, geo[0])
    c_args, c_in_specs, c_out_specs, c_out_shape = _cast_specs(cast_jobs, t_all // tm)
    kern = functools.partial(_gla_proj_kernel, d=d, dk=dk, dv=dv, heads=heads, n_x=len(x_args),
                             n_lat_tiles=geo[0], n_cast=len(cast_jobs))
    outs = pl.pallas_call(
        kern,
        grid=(t_all // tm,),
        in_specs=x_specs + [
            pl.BlockSpec((1, 1, 6 * d), _group_map(*geo)),
            _const_spec((1, d)),
            _const_spec(w_in16.shape),
        ] + c_in_specs,
        out_specs=[
            pl.BlockSpec((2 * heads, tm, hk), lambda i: (0, i, 0)),
            pl.BlockSpec((tm, dv), lambda i: (i, 0)),
            pl.BlockSpec((dv, tm), lambda i: (0, i)),
            pl.BlockSpec((tm, 2 * GLA_GATE_RANK), lambda i: (i, 0)),
        ] + c_out_specs,
        out_shape=[
            jax.ShapeDtypeStruct((2 * heads, t_all, hk), bf16),
            jax.ShapeDtypeStruct((t_all, dv), bf16),
            jax.ShapeDtypeStruct((dv, t_all), bf16),
            jax.ShapeDtypeStruct((t_all, 2 * GLA_GATE_RANK), bf16),
        ] + c_out_shape,
        scratch_shapes=[pltpu.VMEM((dv, d), bf16)],
        compiler_params=pltpu.CompilerParams(dimension_semantics=("arbitrary",),
                                             vmem_limit_bytes=VMEM_LIMIT),
        name="gla_proj",
    )(*x_args, mod, norm_w.reshape(1, d), w_in16, *c_args)
    return outs[:4], list(outs[4:])


def _gla_block(q, k, vt, z2, s_ref, o_ref, reverse):
    tb, hk = q.shape
    c = GLA_CHUNK
    half = tb // 2
    assert tb == 4 * c
    row = lax.broadcasted_iota(jnp.int32, (tb, tb), 0)
    col = lax.broadcasted_iota(jnp.int32, (tb, tb), 1)
    causal = ((row // c) == (col // c)) & ((col >= row) if reverse else (col <= row))
    cum = jnp.where(causal, 1.0, 0.0).astype(bf16)
    g = (jnp.minimum(z2, 0.0) - jnp.log2(1.0 + jnp.exp2(-jnp.abs(z2)))) * (1.0 / GLA_GATE_NORM)
    g_hi = g.astype(bf16)
    g_lo = (g - g_hi.astype(f32)).astype(bf16)
    b2 = _nn(cum, jnp.concatenate([g_hi, g_lo], axis=1))
    yield
    b = b2[:, :hk] + b2[:, hk:]
    order = (3, 2, 1, 0) if reverse else (0, 1, 2, 3)
    pos = {ch: s for s, ch in enumerate(order)}
    tot = []
    for ch in range(4):
        last = ch * c if reverse else ch * c + c - 1
        tot.append(b[last:last + 1, :])
    zero = jnp.zeros_like(tot[0])
    before = [sum((tot[order[s2]] for s2 in range(pos[ch])), zero) for ch in range(4)]
    after = [sum((tot[order[s2]] for s2 in range(pos[ch] + 1, 4)), zero) for ch in range(4)]

    def rows(vals):
        return jnp.concatenate([jnp.broadcast_to(v, (c, hk)) for v in vals], axis=0)

    def row_scales(vals):
        return rows([jnp.exp2(v) for v in vals])

    kf = k.astype(f32)
    qd = q.astype(f32) * jnp.exp2(b)
    ki = kf * jnp.exp2(-b)
    ke = kf * jnp.exp2(rows(tot) - b)
    qd16, ki16, ke16 = qd.astype(bf16), ki.astype(bf16), ke.astype(bf16)
    yield

    lrow = lax.broadcasted_iota(jnp.int32, (half, half), 0)
    lcol = lax.broadcasted_iota(jnp.int32, (half, half), 1)
    m_same = ((lrow // c) == (lcol // c)) & ((lcol >= lrow) if reverse else (lcol <= lrow))
    m_next = ((lrow < c) & (lcol >= c)) if reverse else ((lrow >= c) & (lcol < c))
    xs = [_nt(qd16[lo:lo + half], jnp.concatenate([ki16[lo:lo + half], ke16[lo:lo + half]], axis=0))
          for lo in (0, half)]

    late, early = (0, half) if reverse else (half, 0)
    q_far = order[3] * c - late
    k_far = order[0] * c - early
    q_scale = [zero, zero]
    q_scale[q_far // c] = tot[order[2]]
    k_scale = [zero, zero]
    k_scale[k_far // c] = tot[order[1]]
    q2 = (qd[late:late + half] * row_scales(q_scale)).astype(bf16)
    k2 = (ke[early:early + half] * row_scales(k_scale)).astype(bf16)
    cross_f32 = _nt(q2, k2)
    s = s_ref[...]
    inter = _nt((qd * row_scales(before)).astype(bf16), s.astype(bf16))
    k_end = (ke * row_scales(after)).astype(bf16)
    upd = _nn(vt, k_end)
    yield
    cross = cross_f32.astype(bf16)

    diag = [jnp.where(m_same, x[:, :half], jnp.where(m_next, x[:, half:], 0.0)).astype(bf16) for x in xs]
    if reverse:
        p_early = diag[1]
        p_late = jnp.concatenate([diag[0], cross], axis=1)
    else:
        p_early = diag[0]
        p_late = jnp.concatenate([cross, diag[1]], axis=1)
    pv_early = _nt(p_early, vt[:, early:early + half])
    pv_late = _nt(p_late, vt)
    yield
    o_ref[early:early + half, :] = (pv_early + inter[early:early + half]).astype(o_ref.dtype)
    o_ref[late:late + half, :] = (pv_late + inter[late:late + half]).astype(o_ref.dtype)
    s_ref[...] = s * jnp.exp2(before[order[3]] + tot[order[3]]) + upd


def _gla_scan_kernel(qkf_ref, vtf_ref, lowf_ref, qkb_ref, vtb_ref, lowb_ref, wup_ref, bup_ref,
                     of_ref, ob_ref, s_ref, *, heads):
    @pl.when(pl.program_id(1) == 0)
    def _():
        s_ref[...] = jnp.zeros_like(s_ref)

    hv = vtf_ref.shape[0] // heads
    hk = qkf_ref.shape[2]
    dk = heads * hk

    def gate_preact(low_ref, direction):
        cols = slice(direction * dk, (direction + 1) * dk)
        return _nn(low_ref[...], wup_ref[:, cols]) + bup_ref[:, cols]

    g_f = gate_preact(lowf_ref, 0)
    g_b = gate_preact(lowb_ref, 1)
    chains = []
    for hh in range(heads):
        cols = slice(hh * hv, (hh + 1) * hv)
        gcols = slice(hh * hk, (hh + 1) * hk)
        chains.append(_gla_block(qkf_ref[hh], qkf_ref[heads + hh], vtf_ref[cols, :], g_f[:, gcols],
                                 s_ref.at[0, hh], of_ref.at[:, cols], False))
        chains.append(_gla_block(qkb_ref[hh], qkb_ref[heads + hh], vtb_ref[cols, :], g_b[:, gcols],
                                 s_ref.at[1, hh], ob_ref.at[:, cols], True))
    groups = [chains[k:k + SCAN_SKEW_GROUP] for k in range(0, len(chains), SCAN_SKEW_GROUP)]
    tick = 0
    while any(groups):
        for k, group in enumerate(groups):
            if tick < k:
                continue
            alive = []
            for chain in group:
                try:
                    next(chain)
                    alive.append(chain)
                except StopIteration:
                    pass
            groups[k] = alive
        tick += 1


def _gla_scan(qk, vt, low, w_gate_up, b_gate_up, n_batch, seq, ctx_len):
    heads = GLA_HEADS
    _, t_all, hk = qk.shape
    dk = heads * hk
    dv = vt.shape[0]
    hv = dv // heads
    rank = GLA_GATE_RANK
    tb = GLA_BLOCK
    assert ctx_len == tb and seq % tb == 0
    nlat = seq // tb
    ctx0 = n_batch * nlat
    wup = jnp.zeros((2 * rank, 2 * dk), f32)
    wup = (wup.at[:rank, :dk].set(w_gate_up[0]).at[rank:, dk:].set(w_gate_up[1]) * LOG2E).astype(bf16)
    bup = b_gate_up.reshape(1, 2 * dk) * LOG2E

    def fwd_row(b, t):
        return jnp.where(t == 0, ctx0 + b, b * nlat + t - 1)

    def bwd_row(b, t):
        return jnp.where(t == 0, ctx0 + b, b * nlat + nlat - t)

    def specs(row_fn):
        return [
            pl.BlockSpec((2 * heads, tb, hk), lambda b, t: (0, row_fn(b, t), 0)),
            pl.BlockSpec((dv, tb), lambda b, t: (0, row_fn(b, t))),
            pl.BlockSpec((tb, 2 * rank), lambda b, t: (row_fn(b, t), 0)),
        ]

    return pl.pallas_call(
        functools.partial(_gla_scan_kernel, heads=heads),
        grid=(n_batch, nlat + 1),
        in_specs=specs(fwd_row) + specs(bwd_row) + [_const_spec((2 * rank, 2 * dk)), _const_spec((1, 2 * dk))],
        out_specs=[
            pl.BlockSpec((tb, dv), lambda b, t: (fwd_row(b, t), 0)),
            pl.BlockSpec((tb, dv), lambda b, t: (bwd_row(b, t), 0)),
        ],
        out_shape=[jax.ShapeDtypeStruct((t_all, dv), bf16)] * 2,
        scratch_shapes=[pltpu.VMEM((2, heads, hv, hk), f32)],
        compiler_params=pltpu.CompilerParams(dimension_semantics=("arbitrary", "arbitrary"),
                                             vmem_limit_bytes=VMEM_LIMIT),
        name="gla_scan",
    )(qk, vt, low, qk, vt, low, wup, bup)


def _swa_proj_kernel(x_ref, mod_ref, nw_ref, w_ref, b_ref, cos_ref, sin_ref,
                     q_ref, kx_ref, vx_ref, *, d, nq):
    sh, sc, _ = _mod_slices(mod_ref, d, 0)
    subs = _sub_tiles(x_ref.shape[0])
    hs = [(_rms(x_ref[rs, :], nw_ref[...]) * (1.0 + sc) + sh).astype(bf16) for rs in subs]
    half = SWA_HD // 2
    for rs, h in zip(subs, hs):
        qkv = _nn(h, w_ref[...]) + b_ref[...]
        cos = cos_ref[rs, :]
        sin = sin_ref[rs, :]
        lane = lax.broadcasted_iota(jnp.int32, cos.shape, 1)
        first_half = (lane % SWA_HD) < half
        low_head = lane < SWA_HD

        def rope(v):
            partner = jnp.where(first_half, pltpu.roll(v, LANES - half, 1), pltpu.roll(v, half, 1))
            return v * cos + partner * sin

        for j in range(nq // LANES):
            q_ref[rs, j * LANES:(j + 1) * LANES] = (
                rope(qkv[:, j * LANES:(j + 1) * LANES]) * (SWA_HD ** -0.5 * LOG2E)).astype(bf16)
        k = rope(qkv[:, nq:nq + LANES])
        v = qkv[:, nq + LANES:nq + 2 * LANES]
        for src_val, dst in ((k, kx_ref), (v, vx_ref)):
            swapped = pltpu.roll(src_val, SWA_HD, 1)
            dst[0, rs, :] = jnp.where(low_head, src_val, 0.0).astype(bf16)
            dst[1, rs, :] = jnp.where(low_head, 0.0, swapped).astype(bf16)
            dst[2, rs, :] = jnp.where(low_head, swapped, 0.0).astype(bf16)
            dst[3, rs, :] = jnp.where(low_head, 0.0, src_val).astype(bf16)


def _rope_tables(seq, extra_rows):
    rows = seq // GRID_W
    row = jnp.broadcast_to(jnp.arange(rows)[:, None], (rows, GRID_W)).reshape(-1).astype(f32)
    col = jnp.broadcast_to(jnp.arange(GRID_W)[None, :], (rows, GRID_W)).reshape(-1).astype(f32)
    n_freq = SWA_HD // 4
    inv = ROPE_THETA ** (-jnp.arange(n_freq, dtype=f32) / n_freq)
    ang = jnp.concatenate([row[:, None] * inv, col[:, None] * inv], axis=-1)
    cos, sin = jnp.cos(ang), jnp.sin(ang)
    reps = LANES // SWA_HD
    cos_l = jnp.tile(jnp.concatenate([cos, cos], axis=-1), (1, reps))
    sin_l = jnp.tile(jnp.concatenate([-sin, sin], axis=-1), (1, reps))
    cos_l = jnp.concatenate([cos_l, jnp.ones((extra_rows, LANES), f32)], axis=0)
    sin_l = jnp.concatenate([sin_l, jnp.zeros((extra_rows, LANES), f32)], axis=0)
    return cos_l, sin_l


def _swa_proj(x_all, mod, norm_w, w_qkv16, b_qkv, cos_l, sin_l, geo):
    t_all, d = x_all.shape
    nq = SWA_HEADS * SWA_HD
    nqkv = w_qkv16.shape[1]
    tm = ROW_TILE
    n_lat_tiles, tiles_per_batch, _ = geo

    def pos_map(i):
        return (jnp.where(i < n_lat_tiles, i % tiles_per_batch, tiles_per_batch), 0)

    return pl.pallas_call(
        functools.partial(_swa_proj_kernel, d=d, nq=nq),
        grid=(t_all // tm,),
        in_specs=[
            pl.BlockSpec((tm, d), lambda i: (i, 0)),
            pl.BlockSpec((1, 1, 6 * d), _group_map(*geo)),
            _const_spec((1, d)),
            _const_spec(w_qkv16.shape),
            _const_spec((1, nqkv)),
            pl.BlockSpec((tm, LANES), pos_map),
            pl.BlockSpec((tm, LANES), pos_map),
        ],
        out_specs=[
            pl.BlockSpec((tm, nq), lambda i: (i, 0)),
            pl.BlockSpec((4, tm, LANES), lambda i: (0, i, 0)),
            pl.BlockSpec((4, tm, LANES), lambda i: (0, i, 0)),
        ],
        out_shape=[
            jax.ShapeDtypeStruct((t_all, nq), bf16),
            jax.ShapeDtypeStruct((4, t_all, LANES), bf16),
            jax.ShapeDtypeStruct((4, t_all, LANES), bf16),
        ],
        compiler_params=pltpu.CompilerParams(dimension_semantics=("arbitrary",),
                                             vmem_limit_bytes=VMEM_LIMIT),
        name="swa_proj",
    )(x_all, mod, norm_w.reshape(1, d), w_qkv16, b_qkv.reshape(1, nqkv), cos_l, sin_l)


def _attend(q_ref, o_ref, sink_ref, k_pieces, v_pieces, edge_masks):
    tq = q_ref.shape[0]
    pairs = q_ref.shape[1] // LANES
    group = pairs // SWA_KV_HEADS
    stack = ATTN_STACK
    lane = lax.broadcasted_iota(jnp.int32, (tq, LANES), 1)
    low_head = lane < SWA_HD
    k_cat, v_cat = [], []
    for g in range(SWA_KV_HEADS):
        k_cat.append(jnp.concatenate(k_pieces[g][0] + k_pieces[g][1], axis=0))
        nk = k_cat[g].shape[0] // 2
        lane2 = lax.broadcasted_iota(jnp.int32, (nk, LANES), 1)
        ones_lo = jnp.where(lane2 < SWA_HD, 1.0, 0.0).astype(bf16)
        ones_hi = jnp.where(lane2 < SWA_HD, 0.0, 1.0).astype(bf16)
        v_cat.append(jnp.concatenate([
            jnp.concatenate([jnp.concatenate(v_pieces[g][0], axis=0), ones_lo], axis=1),
            jnp.concatenate([jnp.concatenate(v_pieces[g][1], axis=0), ones_hi], axis=1)], axis=0))
    units = [(u * stack) // group for u in range(pairs // stack)]

    def scores(u):
        q_stack = jnp.concatenate(
            [q_ref[:, (u * stack + jj) * LANES:(u * stack + jj + 1) * LANES] for jj in range(stack)], axis=0)
        return _nt(q_stack, k_cat[units[u]])

    ahead = [scores(u) for u in range(min(ATTN_AHEAD, len(units)))]
    yield
    for u, g in enumerate(units):
        s_all = ahead.pop(0)
        if u + ATTN_AHEAD < len(units):
            ahead.append(scores(u + ATTN_AHEAD))
        prob_rows, sink_rows = [], []
        for jj in range(stack):
            j = u * stack + jj
            probs, sink_terms = [], []
            for p in range(2):
                sp = s_all[jj * tq:(jj + 1) * tq, p * nk:(p + 1) * nk]
                if edge_masks is not None:
                    before, after = edge_masks
                    blk = before.shape[1]
                    sp = jnp.concatenate([jnp.where(before, sp[:, :blk], NEG_INF), sp[:, blk:2 * blk],
                                          jnp.where(after, sp[:, 2 * blk:3 * blk], NEG_INF), sp[:, 3 * blk:]],
                                         axis=1)
                sink = sink_ref[2 * j + p] * LOG2E
                m = jnp.maximum(jnp.max(sp, axis=-1, keepdims=True), sink)
                probs.append(jnp.exp2(sp - m).astype(bf16))
                sink_terms.append(jnp.exp2(sink - m))
            prob_rows.append(jnp.concatenate(probs, axis=1))
            sink_rows.append(jnp.where(low_head, sink_terms[0], sink_terms[1]))
        out_all = _nn(jnp.concatenate(prob_rows, axis=0), v_cat[g])
        for jj in range(stack):
            j = u * stack + jj
            out = out_all[jj * tq:(jj + 1) * tq]
            den = out[:, LANES:] + sink_rows[jj]
            o_ref[:, j * LANES:(j + 1) * LANES] = (out[:, :LANES] / den).astype(bf16)
        yield


def _trace_in_order(chains):
    chains = list(chains)
    stages = [list(range(SWA_HEADS // 2 // ATTN_STACK + 1)) for _ in chains]
    for c, chain in enumerate(chains):
        for s in stages[c]:
            if not (c > 0 and s == 0):
                next(chain)
            if c + 1 < len(chains) and s == len(stages[c]) - 1 - ATTN_OVERLAP:
                next(chains[c + 1])


def _swa_kernel(sink_ref, q_ref, kp_ref, ko_ref, kn_ref, kc_ref, vp_ref, vo_ref, vn_ref, vc_ref,
                o_ref, *, seq, n_lat_steps, need_ctx):
    i = pl.program_id(1)
    blk = SWA_BLOCK

    def pieces(refs_of_half):
        return [[[load(2 * g + p) for load in refs_of_half] for p in range(2)] for g in range(SWA_KV_HEADS)]

    def rows(ref, lo):
        return lambda idx: ref[idx, lo:lo + blk, :]

    @pl.when(i < n_lat_steps)
    def _():
        halves = []
        for half in range(2):
            first = 2 * i + half
            qi = first * blk + lax.broadcasted_iota(jnp.int32, (blk, blk), 0)
            local = lax.broadcasted_iota(jnp.int32, (blk, blk), 1)
            kj_before = (first - 1) * blk + local
            kj_after = (first + 1) * blk + local
            masks = ((qi - kj_before <= SWA_WINDOW) & (kj_before >= 0),
                     (kj_after - qi <= SWA_WINDOW) & (kj_after < seq))
            if half == 0:
                k_loads = (rows(kp_ref, 0), rows(ko_ref, 0), rows(ko_ref, blk), lambda idx: kc_ref[idx])
                v_loads = (rows(vp_ref, 0), rows(vo_ref, 0), rows(vo_ref, blk), lambda idx: vc_ref[idx])
            else:
                k_loads = (rows(ko_ref, 0), rows(ko_ref, blk), rows(kn_ref, 0), lambda idx: kc_ref[idx])
                v_loads = (rows(vo_ref, 0), rows(vo_ref, blk), rows(vn_ref, 0), lambda idx: vc_ref[idx])
            rs = slice(half * blk, (half + 1) * blk)
            halves.append(_attend(q_ref.at[rs, :], o_ref.at[rs, :], sink_ref, pieces(k_loads), pieces(v_loads),
                                  masks))
        _trace_in_order(halves)

    if need_ctx:
        @pl.when(i >= n_lat_steps)
        def _():
            halves = []
            for half in range(2):
                rs = slice(half * blk, (half + 1) * blk)
                halves.append(_attend(q_ref.at[rs, :], o_ref.at[rs, :], sink_ref, pieces((lambda idx: kc_ref[idx],)),
                                      pieces((lambda idx: vc_ref[idx],)), None))
            _trace_in_order(halves)


def _swa_attention(q, kx, vx, sinks, n_batch, seq, ctx_len, need_ctx):
    t_all, nq = q.shape
    blk = SWA_BLOCK
    step = 2 * blk
    assert seq % step == 0 and ctx_len % step == 0
    nb = seq // blk
    n_lat_steps = seq // step
    n_ctx_steps = ctx_len // step
    out_rows = t_all if need_ctx else n_batch * seq
    smem = pl.BlockSpec(memory_space=pltpu.SMEM)

    def edge(off):
        return pl.BlockSpec((4, blk, LANES),
                            lambda b, i: (0, b * nb + jnp.clip(2 * i + off, 0, nb - 1), 0))

    def own():
        return pl.BlockSpec((4, step, LANES),
                            lambda b, i: (0, b * n_lat_steps + jnp.minimum(i, n_lat_steps - 1), 0))

    def ctx_spec():
        return pl.BlockSpec((4, ctx_len, LANES), lambda b, i: (0, (n_batch * seq) // ctx_len + b, 0))

    def q_row(b, i):
        return (jnp.where(i < n_lat_steps, b * n_lat_steps + i,
                          n_batch * n_lat_steps + b * n_ctx_steps + i - n_lat_steps), 0)

    return pl.pallas_call(
        functools.partial(_swa_kernel, seq=seq, n_lat_steps=n_lat_steps, need_ctx=need_ctx),
        grid=(n_batch, n_lat_steps + n_ctx_steps if need_ctx else n_lat_steps),
        in_specs=[smem, pl.BlockSpec((step, nq), q_row),
                  edge(-1), own(), edge(2), ctx_spec(), edge(-1), own(), edge(2), ctx_spec()],
        out_specs=pl.BlockSpec((step, nq), q_row),
        out_shape=jax.ShapeDtypeStruct((out_rows, nq), bf16),
        compiler_params=pltpu.CompilerParams(dimension_semantics=("arbitrary", "arbitrary"),
                                             vmem_limit_bytes=VMEM_LIMIT),
        name="swa_attn",
    )(sinks, q, kx, kx, kx, kx, vx, vx, vx, vx)


def _post_kernel(*refs, d, gla, final, hidden, n_x, n_lat_tiles, n_cast):
    x_refs, refs = refs[:n_x], refs[n_x:]
    if gla:
        of_ref, ob_ref, r_ref, on_ref = refs[:4]
        rest = refs[4:]
    else:
        a_ref = refs[0]
        rest = refs[1:]
    mod_ref, wo_ref, nw_ref, win_ref, wout_ref = rest[:5]
    rest = rest[5:]
    if final:
        nf_ref, rest = rest[0], rest[1:]
    cast_in, out_ref, cast_out = rest[:n_cast], rest[n_cast], rest[n_cast + 1:]
    _run_casts(cast_in, cast_out)

    _, _, g1 = _mod_slices(mod_ref, d, 0)
    sh2, sc2, g2 = _mod_slices(mod_ref, d, 3)

    def mixer_residual(rs):
        if gla:
            o = of_ref[rs, :].astype(f32) + ob_ref[rs, :].astype(f32)
            hv = d // GLA_HEADS
            parts = []
            for hh in range(GLA_HEADS):
                seg = o[:, hh * hv:(hh + 1) * hv]
                parts.append(_rms(seg, on_ref[:, hh * hv:(hh + 1) * hv]))
            a = (jnp.concatenate(parts, axis=1) * r_ref[rs, :].astype(f32)).astype(bf16)
        else:
            a = a_ref[rs, :]
        x1 = _load_x(x_refs, n_lat_tiles, rs) + g1 * _nn(a, wo_ref[...])
        return x1, (_rms(x1, nw_ref[...]) * (1.0 + sc2) + sh2).astype(bf16)

    subs = _sub_tiles(out_ref.shape[0])
    pre = [mixer_residual(rs) for rs in subs]
    for rs, (x1, hn) in zip(subs, pre):
        acts = []
        for off in range(0, hidden, FFN_CHUNK):
            gate = _nn(hn, win_ref[:, off:off + FFN_CHUNK])
            up = _nn(hn, win_ref[:, hidden + off:hidden + off + FFN_CHUNK])
            acts.append((_silu(gate) * up).astype(bf16))
        x2 = x1 + g2 * _nn(jnp.concatenate(acts, axis=1), wout_ref[...])
        if final:
            x2 = _rms(x2, nf_ref[...])
        out_ref[rs, :] = x2


def _post(x_src, mixer_in, mod, w_o16, norm_w, w_in16, w_out16, geo, gla, norm_final, n_tiles, o_norm=None,
          cast_jobs=()):
    hidden, d = w_out16.shape
    assert hidden % FFN_CHUNK == 0
    tm = ROW_TILE
    final = norm_final is not None
    row = lambda i: (i, 0)
    args, in_specs = _x_specs(x_src, geo[0])
    n_x = len(args)
    if gla:
        o_f, o_b, r = mixer_in
        args += [o_f, o_b, r, jnp.tile(o_norm, GLA_HEADS).reshape(1, d)]
        in_specs += [pl.BlockSpec((tm, d), row)] * 3 + [_const_spec((1, d))]
    else:
        args += [mixer_in]
        in_specs += [pl.BlockSpec((tm, d), row)]
    args += [mod, w_o16, norm_w.reshape(1, d), w_in16, w_out16]
    in_specs += [pl.BlockSpec((1, 1, 6 * d), _group_map(*geo)), _const_spec(w_o16.shape), _const_spec((1, d)),
                 _const_spec(w_in16.shape), _const_spec(w_out16.shape)]
    if final:
        args += [norm_final.reshape(1, d)]
        in_specs += [_const_spec((1, d))]
    c_args, c_in_specs, c_out_specs, c_out_shape = _cast_specs(cast_jobs, n_tiles)
    args += c_args
    in_specs += c_in_specs
    out_specs = [pl.BlockSpec((tm, d), row)] + c_out_specs
    out_shape = [jax.ShapeDtypeStruct((n_tiles * tm, d), f32)] + c_out_shape
    outs = pl.pallas_call(
        functools.partial(_post_kernel, d=d, gla=gla, final=final, hidden=hidden, n_x=n_x,
                          n_lat_tiles=geo[0], n_cast=len(cast_jobs)),
        grid=(n_tiles,),
        in_specs=in_specs,
        out_specs=out_specs,
        out_shape=out_shape,
        compiler_params=pltpu.CompilerParams(dimension_semantics=("arbitrary",),
                                             vmem_limit_bytes=VMEM_LIMIT),
        name="post_gla" if gla else "post_swa",
    )(*args)
    return outs[0], list(outs[1:])


def kernel(x, c, ctx, c_ctx, w_ada, b_ada, norm_mix, norm_ffn, gla_w_in, gla_w_gate_up, gla_b_gate_up,
           gla_w_o, gla_o_norm, swa_w_qkv, swa_b_qkv, swa_w_o, swa_sinks, ffn_w_in, ffn_w_out,
           norm_final):
    n_batch, seq, d = x.shape
    ctx_len = ctx.shape[1]
    depth = w_ada.shape[0]
    tm = ROW_TILE
    assert seq % tm == 0 and (n_batch * ctx_len) == tm
    n_lat_tiles = n_batch * seq // tm
    geo = (n_lat_tiles, seq // tm, n_batch)

    mod_all = _ada_mod(jnp.concatenate([c, c_ctx[None, :]], axis=0), w_ada, b_ada)
    t_all = n_batch * (seq + ctx_len)
    x_src = (x.reshape(n_batch * seq, d), ctx.reshape(n_batch * ctx_len, d))
    cos_l, sin_l = _rope_tables(seq, tm)

    def mixer_jobs(layer):
        jj = layer // 2
        mix = [(gla_w_o, jj)] if layer % 2 == 0 else [(swa_w_qkv, jj), (swa_w_o, jj)]
        return mix + [(ffn_w_in, layer), (ffn_w_out, layer)]

    w16 = None
    for i in range(depth):
        last = i == depth - 1
        j = i // 2
        mod = mod_all[i].reshape(8, 1, 6 * d)
        n_tiles = n_lat_tiles if last else n_lat_tiles + 1
        nf = norm_final if last else None
        jobs = () if last else mixer_jobs(i + 1)
        if i % 2 == 0:
            (qk, r, vt, low), first_w16 = _gla_proj(x_src, t_all, mod, norm_mix[i], gla_w_in[j].astype(bf16), geo,
                                                    cast_jobs=mixer_jobs(i) if w16 is None else ())
            w16 = first_w16 if w16 is None else w16
        w_o16, w_ffn_in16, w_ffn_out16 = w16[-3:]
        if i % 2 == 0:
            o_f, o_b = _gla_scan(qk, vt, low, gla_w_gate_up[j], gla_b_gate_up[j], n_batch, seq, ctx_len)
            x_src, w16 = _post(x_src, (o_f, o_b, r), mod, w_o16, norm_ffn[i], w_ffn_in16, w_ffn_out16,
                               geo, True, nf, n_tiles, o_norm=gla_o_norm[j], cast_jobs=jobs)
        else:
            q, kx, vx = _swa_proj(x_src, mod, norm_mix[i], w16[0], swa_b_qkv[j], cos_l, sin_l, geo)
            attn = _swa_attention(q, kx, vx, swa_sinks[j], n_batch, seq, ctx_len, not last)
            x_src, w16 = _post(x_src, attn, mod, w_o16, norm_ffn[i], w_ffn_in16, w_ffn_out16,
                               geo, False, nf, n_tiles, cast_jobs=jobs)
    return x_src.reshape(n_batch, seq, d)
```

```python
import functools

import jax
import jax.numpy as jnp
from jax import lax
from jax.experimental import pallas as pl
from jax.experimental.pallas import tpu as pltpu

f32 = jnp.float32
bf16 = jnp.bfloat16

EPS = 1e-6
LOG2E = 1.4426950408889634
NEG_INF = -1e30
GRID_W = 64
ROPE_THETA = 10000.0

GLA_HEADS = 4
GLA_GATE_RANK = 16
GLA_GATE_NORM = 16.0
GLA_CHUNK = 64
GLA_BLOCK = 256
SCAN_SKEW_STAGES = 2
SCAN_SKEW_GROUP = 4

SWA_HEADS = 16
SWA_KV_HEADS = 2
SWA_HD = 64
SWA_WINDOW = 128
SWA_BLOCK = 128
ATTN_STACK = 1

ATTN_AHEAD = 3
ATTN_OVERLAP = 1

LANES = 128
FFN_CHUNK = 256
ROW_TILE = 512
SUB_TILES = 2
VMEM_LIMIT = 56 * 1024 * 1024


def _nt(a, b):
    return lax.dot_general(a, b, (((1,), (1,)), ((), ())), preferred_element_type=f32)


def _nn(a, b):
    return jnp.dot(a, b, preferred_element_type=f32)


def _silu(v):
    return v * (1.0 / (1.0 + jnp.exp(-v)))


def _rms(v, w):
    return v * lax.rsqrt(jnp.mean(v * v, axis=-1, keepdims=True) + EPS) * w


def _const_spec(shape):
    zeros = (0,) * len(shape)
    return pl.BlockSpec(shape, lambda *_: zeros, pipeline_mode=pl.Buffered(1))


def _ada_kernel(cb_ref, w_ref, b_ref, out_ref, *, n_rows, d_model, tn):
    ncol = tn // LANES

    def body(i, accs):
        d0 = pl.multiple_of(i * 8, 8)
        w8 = w_ref[0, pl.ds(d0, 8), :]
        new = []
        for r in range(n_rows):
            s8 = _silu(cb_ref[r, pl.ds(d0, 8), :])
            new.append(tuple(accs[r][j] + w8[:, j * LANES:(j + 1) * LANES] * s8 for j in range(ncol)))
        return tuple(new)

    init = tuple(tuple(jnp.zeros((8, LANES), f32) for _ in range(ncol)) for _ in range(n_rows))
    accs = lax.fori_loop(0, d_model // 8, body, init, unroll=8)
    rows = [jnp.concatenate([jnp.sum(a, axis=0, keepdims=True) for a in accs[r]], axis=1)
            for r in range(n_rows)]
    rows += [jnp.zeros((1, tn), f32)] * (8 - n_rows)
    out_ref[0] = jnp.concatenate(rows, axis=0) + b_ref[0]


def _ada_mod(cvecs, w_ada, b_ada):
    n_rows, d_model = cvecs.shape
    depth, _, e = w_ada.shape
    tn = 1536
    cb = jnp.broadcast_to(cvecs[:, :, None], (n_rows, d_model, LANES))
    return pl.pallas_call(
        functools.partial(_ada_kernel, n_rows=n_rows, d_model=d_model, tn=tn),
        grid=(depth, e // tn),
        in_specs=[
            pl.BlockSpec((n_rows, d_model, LANES), lambda l, j: (0, 0, 0)),
            pl.BlockSpec((1, d_model, tn), lambda l, j: (l, 0, j)),
            pl.BlockSpec((1, 1, tn), lambda l, j: (l, 0, j)),
        ],
        out_specs=pl.BlockSpec((1, 8, tn), lambda l, j: (l, 0, j)),
        out_shape=jax.ShapeDtypeStruct((depth, 8, e), f32),
        compiler_params=pltpu.CompilerParams(dimension_semantics=("arbitrary", "arbitrary")),
        name="ada_mod",
    )(cb, w_ada, b_ada.reshape(depth, 1, e))


def _mod_slices(mod_ref, d, first):
    return tuple(mod_ref[0, :, (first + k) * d:(first + k + 1) * d] for k in range(3))


def _x_specs(x_src, n_lat_tiles):
    tm = ROW_TILE
    if isinstance(x_src, tuple):
        lat, ctx = x_src
        d = lat.shape[1]
        assert ctx.shape == (tm, d)
        return [lat, ctx], [pl.BlockSpec((tm, d), lambda i: (jnp.minimum(i, n_lat_tiles - 1), 0)),
                            pl.BlockSpec((tm, d), lambda i: (0, 0))]
    return [x_src], [pl.BlockSpec((tm, x_src.shape[1]), lambda i: (i, 0))]


def _load_x(x_refs, n_lat_tiles, rs=slice(None)):
    if len(x_refs) == 2:
        return jnp.where(pl.program_id(0) < n_lat_tiles, x_refs[0][rs, :], x_refs[1][rs, :])
    return x_refs[0][rs, :]


def _sub_tiles(rows):
    step = rows // SUB_TILES
    return [slice(s * step, (s + 1) * step) for s in range(SUB_TILES)]


def _group_map(n_lat_tiles, tiles_per_batch, n_batch):
    def index_map(i):
        return (jnp.where(i < n_lat_tiles, i // tiles_per_batch, n_batch), 0, 0)
    return index_map


CAST_BLOCKS = 16


def _cast_specs(cast_jobs, n_steps):
    n_blocks = CAST_BLOCKS
    while n_blocks > n_steps:
        n_blocks //= 2
    last_block = n_blocks - 1
    args, in_specs, out_specs, out_shape = [], [], [], []
    for stacked, layer in cast_jobs:
        _, rows, cols = stacked.shape
        assert rows % (16 * n_blocks) == 0
        br = rows // n_blocks
        args.append(stacked)
        in_specs.append(pl.BlockSpec((None, br, cols),
                                     lambda i, layer=layer: (layer, jnp.minimum(i, last_block), 0)))
        out_specs.append(pl.BlockSpec((br, cols), lambda i: (jnp.minimum(i, last_block), 0)))
        out_shape.append(jax.ShapeDtypeStruct((rows, cols), bf16))
    return args, in_specs, out_specs, out_shape


def _run_casts(cast_in, cast_out):
    for src_ref, dst_ref in zip(cast_in, cast_out):
        dst_ref[...] = src_ref[...].astype(bf16)


def _gla_proj_kernel(*refs, d, dk, dv, heads, n_x, n_lat_tiles, n_cast):
    x_refs, (mod_ref, nw_ref, win_ref), rest = refs[:n_x], refs[n_x:n_x + 3], refs[n_x + 3:]
    cast_in, (qk_ref, r_ref, vt_ref, low_ref), rest = rest[:n_cast], rest[n_cast:n_cast + 4], rest[n_cast + 4:]
    cast_out, wvt_ref = rest[:n_cast], rest[n_cast]
    _run_casts(cast_in, cast_out)

    @pl.when(pl.program_id(0) == 0)
    def _():
        wvt_ref[...] = win_ref[:, 2 * dk:2 * dk + dv].astype(f32).T.astype(bf16)

    sh, sc, _ = _mod_slices(mod_ref, d, 0)
    hk = dk // heads
    subs = _sub_tiles(r_ref.shape[0])
    hs = [(_rms(_load_x(x_refs, n_lat_tiles, rs), nw_ref[...]) * (1.0 + sc) + sh).astype(bf16) for rs in subs]
    for rs, h in zip(subs, hs):
        low_ref[rs, :] = _nn(h, win_ref[:, 2 * dk + 2 * dv:]).astype(bf16)
        r_ref[rs, :] = _silu(_nn(h, win_ref[:, 2 * dk + dv:2 * dk + 2 * dv])).astype(bf16)
        qk = _nn(h, win_ref[:, :2 * dk])
        q = qk[:, :dk] * (hk ** -0.5)
        for hh in range(heads):
            qk_ref[hh, rs, :] = q[:, hh * hk:(hh + 1) * hk].astype(bf16)
            qk_ref[heads + hh, rs, :] = qk[:, dk + hh * hk:dk + (hh + 1) * hk].astype(bf16)
        vt_ref[:, rs] = _nt(wvt_ref[...], h).astype(bf16)


def _gla_proj(x_src, t_all, mod, norm_w, w_in16, layer, geo, cast_jobs=()):
    d = w_in16.shape[1]
    dk, dv, heads = d // 2, d, GLA_HEADS
    hk = dk // heads
    tm = ROW_TILE
    x_args, x_specs = _x_specs(x_src, geo[0])
    c_args, c_in_specs, c_out_specs, c_out_shape = _cast_specs(cast_jobs, t_all // tm)
    kern = functools.partial(_gla_proj_kernel, d=d, dk=dk, dv=dv, heads=heads, n_x=len(x_args),
                             n_lat_tiles=geo[0], n_cast=len(cast_jobs))
    outs = pl.pallas_call(
        kern,
        grid=(t_all // tm,),
        in_specs=x_specs + [
            pl.BlockSpec((1, 1, 6 * d), _group_map(*geo)),
            _const_spec((1, d)),
            pl.BlockSpec((None,) + w_in16.shape[1:], lambda i: (layer, 0, 0), pipeline_mode=pl.Buffered(1)),
        ] + c_in_specs,
        out_specs=[
            pl.BlockSpec((2 * heads, tm, hk), lambda i: (0, i, 0)),
            pl.BlockSpec((tm, dv), lambda i: (i, 0)),
            pl.BlockSpec((dv, tm), lambda i: (0, i)),
            pl.BlockSpec((tm, 2 * GLA_GATE_RANK), lambda i: (i, 0)),
        ] + c_out_specs,
        out_shape=[
            jax.ShapeDtypeStruct((2 * heads, t_all, hk), bf16),
            jax.ShapeDtypeStruct((t_all, dv), bf16),
            jax.ShapeDtypeStruct((dv, t_all), bf16),
            jax.ShapeDtypeStruct((t_all, 2 * GLA_GATE_RANK), bf16),
        ] + c_out_shape,
        scratch_shapes=[pltpu.VMEM((dv, d), bf16)],
        compiler_params=pltpu.CompilerParams(dimension_semantics=("arbitrary",),
                                             vmem_limit_bytes=VMEM_LIMIT),
        name="gla_proj",
    )(*x_args, mod, norm_w.reshape(1, d), w_in16, *c_args)
    return outs[:4], list(outs[4:])


def _gla_block(q, k, vt, z2, s_ref, o_ref, reverse):
    tb, hk = q.shape
    c = GLA_CHUNK
    half = tb // 2
    assert tb == 4 * c
    row = lax.broadcasted_iota(jnp.int32, (tb, tb), 0)
    col = lax.broadcasted_iota(jnp.int32, (tb, tb), 1)
    causal = ((row // c) == (col // c)) & ((col >= row) if reverse else (col <= row))
    cum = jnp.where(causal, 1.0, 0.0).astype(bf16)
    g = (jnp.minimum(z2, 0.0) - jnp.log2(1.0 + jnp.exp2(-jnp.abs(z2)))) * (1.0 / GLA_GATE_NORM)
    g_hi = g.astype(bf16)
    g_lo = (g - g_hi.astype(f32)).astype(bf16)
    b2 = _nn(cum, jnp.concatenate([g_hi, g_lo], axis=1))
    yield
    b = b2[:, :hk] + b2[:, hk:]
    order = (3, 2, 1, 0) if reverse else (0, 1, 2, 3)
    pos = {ch: s for s, ch in enumerate(order)}
    tot = []
    for ch in range(4):
        last = ch * c if reverse else ch * c + c - 1
        tot.append(b[last:last + 1, :])
    zero = jnp.zeros_like(tot[0])
    before = [sum((tot[order[s2]] for s2 in range(pos[ch])), zero) for ch in range(4)]
    after = [sum((tot[order[s2]] for s2 in range(pos[ch] + 1, 4)), zero) for ch in range(4)]

    def rows(vals):
        return jnp.concatenate([jnp.broadcast_to(v, (c, hk)) for v in vals], axis=0)

    def row_scales(vals):
        return rows([jnp.exp2(v) for v in vals])

    kf = k.astype(f32)
    qd = q.astype(f32) * jnp.exp2(b)
    ki = kf * jnp.exp2(-b)
    ke = kf * jnp.exp2(rows(tot) - b)
    qd16, ki16, ke16 = qd.astype(bf16), ki.astype(bf16), ke.astype(bf16)
    yield

    lrow = lax.broadcasted_iota(jnp.int32, (half, half), 0)
    lcol = lax.broadcasted_iota(jnp.int32, (half, half), 1)
    m_same = ((lrow // c) == (lcol // c)) & ((lcol >= lrow) if reverse else (lcol <= lrow))
    m_next = ((lrow < c) & (lcol >= c)) if reverse else ((lrow >= c) & (lcol < c))
    xs = [_nt(qd16[lo:lo + half], jnp.concatenate([ki16[lo:lo + half], ke16[lo:lo + half]], axis=0))
          for lo in (0, half)]

    late, early = (0, half) if reverse else (half, 0)
    q_far = order[3] * c - late
    k_far = order[0] * c - early
    q_scale = [zero, zero]
    q_scale[q_far // c] = tot[order[2]]
    k_scale = [zero, zero]
    k_scale[k_far // c] = tot[order[1]]
    q2 = (qd[late:late + half] * row_scales(q_scale)).astype(bf16)
    k2 = (ke[early:early + half] * row_scales(k_scale)).astype(bf16)
    cross_f32 = _nt(q2, k2)
    s = s_ref[...]
    inter = _nt((qd * row_scales(before)).astype(bf16), s.astype(bf16))
    k_end = (ke * row_scales(after)).astype(bf16)
    upd = _nn(vt, k_end)
    yield
    cross = cross_f32.astype(bf16)

    diag = [jnp.where(m_same, x[:, :half], jnp.where(m_next, x[:, half:], 0.0)).astype(bf16) for x in xs]
    if reverse:
        p_early = diag[1]
        p_late = jnp.concatenate([diag[0], cross], axis=1)
    else:
        p_early = diag[0]
        p_late = jnp.concatenate([cross, diag[1]], axis=1)
    pv_early = _nt(p_early, vt[:, early:early + half])
    pv_late = _nt(p_late, vt)
    yield
    o_ref[early:early + half, :] = (pv_early + inter[early:early + half]).astype(o_ref.dtype)
    o_ref[late:late + half, :] = (pv_late + inter[late:late + half]).astype(o_ref.dtype)
    s_ref[...] = s * jnp.exp2(before[order[3]] + tot[order[3]]) + upd


def _gla_scan_kernel(qkf_ref, vtf_ref, lowf_ref, qkb_ref, vtb_ref, lowb_ref, wup_ref, bup_ref,
                     of_ref, ob_ref, s_ref, *, heads):
    @pl.when(pl.program_id(1) == 0)
    def _():
        s_ref[...] = jnp.zeros_like(s_ref)

    hv = vtf_ref.shape[0] // heads
    hk = qkf_ref.shape[2]
    dk = heads * hk

    def gate_preact(low_ref, direction):
        cols = slice(direction * dk, (direction + 1) * dk)
        return _nn(low_ref[...], wup_ref[:, cols]) + bup_ref[:, cols]

    g_f = gate_preact(lowf_ref, 0)
    g_b = gate_preact(lowb_ref, 1)
    chains = []
    for hh in range(heads):
        cols = slice(hh * hv, (hh + 1) * hv)
        gcols = slice(hh * hk, (hh + 1) * hk)
        chains.append(_gla_block(qkf_ref[hh], qkf_ref[heads + hh], vtf_ref[cols, :], g_f[:, gcols],
                                 s_ref.at[0, hh], of_ref.at[:, cols], False))
        chains.append(_gla_block(qkb_ref[hh], qkb_ref[heads + hh], vtb_ref[cols, :], g_b[:, gcols],
                                 s_ref.at[1, hh], ob_ref.at[:, cols], True))
    groups = [chains[k:k + SCAN_SKEW_GROUP] for k in range(0, len(chains), SCAN_SKEW_GROUP)]
    tick = 0
    while any(groups):
        for k, group in enumerate(groups):
            if tick < k * SCAN_SKEW_STAGES:
                continue
            alive = []
            for chain in group:
                try:
                    next(chain)
                    alive.append(chain)
                except StopIteration:
                    pass
            groups[k] = alive
        tick += 1


def _gla_scan(qk, vt, low, w_gate_up, b_gate_up, n_batch, seq, ctx_len):
    heads = GLA_HEADS
    _, t_all, hk = qk.shape
    dk = heads * hk
    dv = vt.shape[0]
    hv = dv // heads
    rank = GLA_GATE_RANK
    tb = GLA_BLOCK
    assert ctx_len == tb and seq % tb == 0
    nlat = seq // tb
    ctx0 = n_batch * nlat
    wup = jnp.zeros((2 * rank, 2 * dk), f32)
    wup = (wup.at[:rank, :dk].set(w_gate_up[0]).at[rank:, dk:].set(w_gate_up[1]) * LOG2E).astype(bf16)
    bup = b_gate_up.reshape(1, 2 * dk) * LOG2E

    def fwd_row(b, t):
        return jnp.where(t == 0, ctx0 + b, b * nlat + t - 1)

    def bwd_row(b, t):
        return jnp.where(t == 0, ctx0 + b, b * nlat + nlat - t)

    def specs(row_fn):
        return [
            pl.BlockSpec((2 * heads, tb, hk), lambda b, t: (0, row_fn(b, t), 0)),
            pl.BlockSpec((dv, tb), lambda b, t: (0, row_fn(b, t))),
            pl.BlockSpec((tb, 2 * rank), lambda b, t: (row_fn(b, t), 0)),
        ]

    return pl.pallas_call(
        functools.partial(_gla_scan_kernel, heads=heads),
        grid=(n_batch, nlat + 1),
        in_specs=specs(fwd_row) + specs(bwd_row) + [_const_spec((2 * rank, 2 * dk)), _const_spec((1, 2 * dk))],
        out_specs=[
            pl.BlockSpec((tb, dv), lambda b, t: (fwd_row(b, t), 0)),
            pl.BlockSpec((tb, dv), lambda b, t: (bwd_row(b, t), 0)),
        ],
        out_shape=[jax.ShapeDtypeStruct((t_all, dv), bf16)] * 2,
        scratch_shapes=[pltpu.VMEM((2, heads, hv, hk), f32)],
        compiler_params=pltpu.CompilerParams(dimension_semantics=("arbitrary", "arbitrary"),
                                             vmem_limit_bytes=VMEM_LIMIT),
        name="gla_scan",
    )(qk, vt, low, qk, vt, low, wup, bup)


def _swa_proj_kernel(x_ref, mod_ref, nw_ref, w_ref, b_ref, cos_ref, sin_ref,
                     q_ref, kx_ref, vx_ref, *, d, nq):
    sh, sc, _ = _mod_slices(mod_ref, d, 0)
    subs = _sub_tiles(x_ref.shape[0])
    hs = [(_rms(x_ref[rs, :], nw_ref[...]) * (1.0 + sc) + sh).astype(bf16) for rs in subs]
    half = SWA_HD // 2
    for rs, h in zip(subs, hs):
        qkv = _nn(h, w_ref[...]) + b_ref[...]
        cos = cos_ref[rs, :]
        sin = sin_ref[rs, :]
        lane = lax.broadcasted_iota(jnp.int32, cos.shape, 1)
        first_half = (lane % SWA_HD) < half
        low_head = lane < SWA_HD

        def rope(v):
            partner = jnp.where(first_half, pltpu.roll(v, LANES - half, 1), pltpu.roll(v, half, 1))
            return v * cos + partner * sin

        for j in range(nq // LANES):
            q_ref[rs, j * LANES:(j + 1) * LANES] = (
                rope(qkv[:, j * LANES:(j + 1) * LANES]) * (SWA_HD ** -0.5 * LOG2E)).astype(bf16)
        k = rope(qkv[:, nq:nq + LANES])
        v = qkv[:, nq + LANES:nq + 2 * LANES]
        for src_val, dst in ((k, kx_ref), (v, vx_ref)):
            swapped = pltpu.roll(src_val, SWA_HD, 1)
            dst[0, rs, :] = jnp.where(low_head, src_val, 0.0).astype(bf16)
            dst[1, rs, :] = jnp.where(low_head, 0.0, swapped).astype(bf16)
            dst[2, rs, :] = jnp.where(low_head, swapped, 0.0).astype(bf16)
            dst[3, rs, :] = jnp.where(low_head, 0.0, src_val).astype(bf16)


def _rope_tables(seq, extra_rows):
    rows = seq // GRID_W
    row = jnp.broadcast_to(jnp.arange(rows)[:, None], (rows, GRID_W)).reshape(-1).astype(f32)
    col = jnp.broadcast_to(jnp.arange(GRID_W)[None, :], (rows, GRID_W)).reshape(-1).astype(f32)
    n_freq = SWA_HD // 4
    inv = ROPE_THETA ** (-jnp.arange(n_freq, dtype=f32) / n_freq)
    ang = jnp.concatenate([row[:, None] * inv, col[:, None] * inv], axis=-1)
    cos, sin = jnp.cos(ang), jnp.sin(ang)
    reps = LANES // SWA_HD
    cos_l = jnp.tile(jnp.concatenate([cos, cos], axis=-1), (1, reps))
    sin_l = jnp.tile(jnp.concatenate([-sin, sin], axis=-1), (1, reps))
    cos_l = jnp.concatenate([cos_l, jnp.ones((extra_rows, LANES), f32)], axis=0)
    sin_l = jnp.concatenate([sin_l, jnp.zeros((extra_rows, LANES), f32)], axis=0)
    return cos_l, sin_l


def _swa_proj(x_all, mod, norm_w, w_qkv16, b_qkv, cos_l, sin_l, geo):
    t_all, d = x_all.shape
    nq = SWA_HEADS * SWA_HD
    nqkv = w_qkv16.shape[1]
    tm = ROW_TILE
    n_lat_tiles, tiles_per_batch, _ = geo

    def pos_map(i):
        return (jnp.where(i < n_lat_tiles, i % tiles_per_batch, tiles_per_batch), 0)

    return pl.pallas_call(
        functools.partial(_swa_proj_kernel, d=d, nq=nq),
        grid=(t_all // tm,),
        in_specs=[
            pl.BlockSpec((tm, d), lambda i: (i, 0)),
            pl.BlockSpec((1, 1, 6 * d), _group_map(*geo)),
            _const_spec((1, d)),
            _const_spec(w_qkv16.shape),
            _const_spec((1, nqkv)),
            pl.BlockSpec((tm, LANES), pos_map),
            pl.BlockSpec((tm, LANES), pos_map),
        ],
        out_specs=[
            pl.BlockSpec((tm, nq), lambda i: (i, 0)),
            pl.BlockSpec((4, tm, LANES), lambda i: (0, i, 0)),
            pl.BlockSpec((4, tm, LANES), lambda i: (0, i, 0)),
        ],
        out_shape=[
            jax.ShapeDtypeStruct((t_all, nq), bf16),
            jax.ShapeDtypeStruct((4, t_all, LANES), bf16),
            jax.ShapeDtypeStruct((4, t_all, LANES), bf16),
        ],
        compiler_params=pltpu.CompilerParams(dimension_semantics=("arbitrary",),
                                             vmem_limit_bytes=VMEM_LIMIT),
        name="swa_proj",
    )(x_all, mod, norm_w.reshape(1, d), w_qkv16, b_qkv.reshape(1, nqkv), cos_l, sin_l)


def _attend(q_ref, o_ref, sink_ref, k_pieces, v_pieces, edge_masks):
    tq = q_ref.shape[0]
    pairs = q_ref.shape[1] // LANES
    group = pairs // SWA_KV_HEADS
    stack = ATTN_STACK
    lane = lax.broadcasted_iota(jnp.int32, (tq, LANES), 1)
    low_head = lane < SWA_HD
    k_cat, v_cat = [], []
    for g in range(SWA_KV_HEADS):
        k_cat.append(jnp.concatenate(k_pieces[g][0] + k_pieces[g][1], axis=0))
        nk = k_cat[g].shape[0] // 2
        lane2 = lax.broadcasted_iota(jnp.int32, (nk, LANES), 1)
        ones_lo = jnp.where(lane2 < SWA_HD, 1.0, 0.0).astype(bf16)
        ones_hi = jnp.where(lane2 < SWA_HD, 0.0, 1.0).astype(bf16)
        v_cat.append(jnp.concatenate([
            jnp.concatenate([jnp.concatenate(v_pieces[g][0], axis=0), ones_lo], axis=1),
            jnp.concatenate([jnp.concatenate(v_pieces[g][1], axis=0), ones_hi], axis=1)], axis=0))
    units = [(u * stack) // group for u in range(pairs // stack)]

    def scores(u):
        q_stack = jnp.concatenate(
            [q_ref[:, (u * stack + jj) * LANES:(u * stack + jj + 1) * LANES] for jj in range(stack)], axis=0)
        return _nt(q_stack, k_cat[units[u]])

    ahead = [scores(u) for u in range(min(ATTN_AHEAD, len(units)))]
    yield
    for u, g in enumerate(units):
        s_all = ahead.pop(0)
        if u + ATTN_AHEAD < len(units):
            ahead.append(scores(u + ATTN_AHEAD))
        prob_rows, sink_rows = [], []
        for jj in range(stack):
            j = u * stack + jj
            probs, sink_terms = [], []
            for p in range(2):
                sp = s_all[jj * tq:(jj + 1) * tq, p * nk:(p + 1) * nk]
                if edge_masks is not None:
                    before, after = edge_masks
                    blk = before.shape[1]
                    sp = jnp.concatenate([jnp.where(before, sp[:, :blk], NEG_INF), sp[:, blk:2 * blk],
                                          jnp.where(after, sp[:, 2 * blk:3 * blk], NEG_INF), sp[:, 3 * blk:]],
                                         axis=1)
                sink = sink_ref[2 * j + p] * LOG2E
                m = jnp.maximum(jnp.max(sp, axis=-1, keepdims=True), sink)
                probs.append(jnp.exp2(sp - m).astype(bf16))
                sink_terms.append(jnp.exp2(sink - m))
            prob_rows.append(jnp.concatenate(probs, axis=1))
            sink_rows.append(jnp.where(low_head, sink_terms[0], sink_terms[1]))
        out_all = _nn(jnp.concatenate(prob_rows, axis=0), v_cat[g])
        for jj in range(stack):
            j = u * stack + jj
            out = out_all[jj * tq:(jj + 1) * tq]
            den = out[:, LANES:] + sink_rows[jj]
            o_ref[:, j * LANES:(j + 1) * LANES] = (out[:, :LANES] / den).astype(bf16)
        yield


def _trace_in_order(chains):
    chains = list(chains)
    stages = [list(range(SWA_HEADS // 2 // ATTN_STACK + 1)) for _ in chains]
    for c, chain in enumerate(chains):
        for s in stages[c]:
            if not (c > 0 and s == 0):
                next(chain)
            if c + 1 < len(chains) and s == len(stages[c]) - 1 - ATTN_OVERLAP:
                next(chains[c + 1])


def _swa_kernel(sink_ref, q_ref, kp_ref, ko_ref, kn_ref, kc_ref, vp_ref, vo_ref, vn_ref, vc_ref,
                o_ref, *, seq, n_lat_steps, need_ctx):
    i = pl.program_id(1)
    blk = SWA_BLOCK

    def pieces(refs_of_half):
        return [[[load(2 * g + p) for load in refs_of_half] for p in range(2)] for g in range(SWA_KV_HEADS)]

    def rows(ref, lo):
        return lambda idx: ref[idx, lo:lo + blk, :]

    @pl.when(i < n_lat_steps)
    def _():
        halves = []
        for half in range(2):
            first = 2 * i + half
            qi = first * blk + lax.broadcasted_iota(jnp.int32, (blk, blk), 0)
            local = lax.broadcasted_iota(jnp.int32, (blk, blk), 1)
            kj_before = (first - 1) * blk + local
            kj_after = (first + 1) * blk + local
            masks = ((qi - kj_before <= SWA_WINDOW) & (kj_before >= 0),
                     (kj_after - qi <= SWA_WINDOW) & (kj_after < seq))
            if half == 0:
                k_loads = (rows(kp_ref, 0), rows(ko_ref, 0), rows(ko_ref, blk), lambda idx: kc_ref[idx])
                v_loads = (rows(vp_ref, 0), rows(vo_ref, 0), rows(vo_ref, blk), lambda idx: vc_ref[idx])
            else:
                k_loads = (rows(ko_ref, 0), rows(ko_ref, blk), rows(kn_ref, 0), lambda idx: kc_ref[idx])
                v_loads = (rows(vo_ref, 0), rows(vo_ref, blk), rows(vn_ref, 0), lambda idx: vc_ref[idx])
            rs = slice(half * blk, (half + 1) * blk)
            halves.append(_attend(q_ref.at[rs, :], o_ref.at[rs, :], sink_ref, pieces(k_loads), pieces(v_loads),
                                  masks))
        _trace_in_order(halves)

    if need_ctx:
        @pl.when(i >= n_lat_steps)
        def _():
            halves = []
            for half in range(2):
                rs = slice(half * blk, (half + 1) * blk)
                halves.append(_attend(q_ref.at[rs, :], o_ref.at[rs, :], sink_ref, pieces((lambda idx: kc_ref[idx],)),
                                      pieces((lambda idx: vc_ref[idx],)), None))
            _trace_in_order(halves)


def _swa_attention(q, kx, vx, sinks, n_batch, seq, ctx_len, need_ctx):
    t_all, nq = q.shape
    blk = SWA_BLOCK
    step = 2 * blk
    assert seq % step == 0 and ctx_len % step == 0
    nb = seq // blk
    n_lat_steps = seq // step
    n_ctx_steps = ctx_len // step
    out_rows = t_all if need_ctx else n_batch * seq
    smem = pl.BlockSpec(memory_space=pltpu.SMEM)

    def edge(off):
        return pl.BlockSpec((4, blk, LANES),
                            lambda b, i: (0, b * nb + jnp.clip(2 * i + off, 0, nb - 1), 0))

    def own():
        return pl.BlockSpec((4, step, LANES),
                            lambda b, i: (0, b * n_lat_steps + jnp.minimum(i, n_lat_steps - 1), 0))

    def ctx_spec():
        return pl.BlockSpec((4, ctx_len, LANES), lambda b, i: (0, (n_batch * seq) // ctx_len + b, 0))

    def q_row(b, i):
        return (jnp.where(i < n_lat_steps, b * n_lat_steps + i,
                          n_batch * n_lat_steps + b * n_ctx_steps + i - n_lat_steps), 0)

    return pl.pallas_call(
        functools.partial(_swa_kernel, seq=seq, n_lat_steps=n_lat_steps, need_ctx=need_ctx),
        grid=(n_batch, n_lat_steps + n_ctx_steps if need_ctx else n_lat_steps),
        in_specs=[smem, pl.BlockSpec((step, nq), q_row),
                  edge(-1), own(), edge(2), ctx_spec(), edge(-1), own(), edge(2), ctx_spec()],
        out_specs=pl.BlockSpec((step, nq), q_row),
        out_shape=jax.ShapeDtypeStruct((out_rows, nq), bf16),
        compiler_params=pltpu.CompilerParams(dimension_semantics=("arbitrary", "arbitrary"),
                                             vmem_limit_bytes=VMEM_LIMIT),
        name="swa_attn",
    )(sinks, q, kx, kx, kx, kx, vx, vx, vx, vx)


def _post_kernel(*refs, d, gla, final, hidden, n_x, n_lat_tiles, n_cast):
    x_refs, refs = refs[:n_x], refs[n_x:]
    if gla:
        of_ref, ob_ref, r_ref, on_ref = refs[:4]
        rest = refs[4:]
    else:
        a_ref = refs[0]
        rest = refs[1:]
    mod_ref, wo_ref, nw_ref, win_ref, wout_ref = rest[:5]
    rest = rest[5:]
    if final:
        nf_ref, rest = rest[0], rest[1:]
    cast_in, out_ref, cast_out = rest[:n_cast], rest[n_cast], rest[n_cast + 1:]
    _run_casts(cast_in, cast_out)

    _, _, g1 = _mod_slices(mod_ref, d, 0)
    sh2, sc2, g2 = _mod_slices(mod_ref, d, 3)

    def mixer_residual(rs):
        if gla:
            o = of_ref[rs, :].astype(f32) + ob_ref[rs, :].astype(f32)
            hv = d // GLA_HEADS
            parts = []
            for hh in range(GLA_HEADS):
                seg = o[:, hh * hv:(hh + 1) * hv]
                parts.append(_rms(seg, on_ref[:, hh * hv:(hh + 1) * hv]))
            a = (jnp.concatenate(parts, axis=1) * r_ref[rs, :].astype(f32)).astype(bf16)
        else:
            a = a_ref[rs, :]
        x1 = _load_x(x_refs, n_lat_tiles, rs) + g1 * _nn(a, wo_ref[...])
        return x1, (_rms(x1, nw_ref[...]) * (1.0 + sc2) + sh2).astype(bf16)

    subs = _sub_tiles(out_ref.shape[0])
    pre = [mixer_residual(rs) for rs in subs]
    for rs, (x1, hn) in zip(subs, pre):
        acts = []
        for off in range(0, hidden, FFN_CHUNK):
            gate = _nn(hn, win_ref[:, off:off + FFN_CHUNK])
            up = _nn(hn, win_ref[:, hidden + off:hidden + off + FFN_CHUNK])
            acts.append((_silu(gate) * up).astype(bf16))
        x2 = x1 + g2 * _nn(jnp.concatenate(acts, axis=1), wout_ref[...])
        if final:
            x2 = _rms(x2, nf_ref[...])
        out_ref[rs, :] = x2


def _post(x_src, mixer_in, mod, w_o16, norm_w, w_in16, w_out16, geo, gla, norm_final, n_tiles, o_norm=None,
          cast_jobs=()):
    hidden, d = w_out16.shape
    assert hidden % FFN_CHUNK == 0
    tm = ROW_TILE
    final = norm_final is not None
    row = lambda i: (i, 0)
    args, in_specs = _x_specs(x_src, geo[0])
    n_x = len(args)
    if gla:
        o_f, o_b, r = mixer_in
        args += [o_f, o_b, r, jnp.tile(o_norm, GLA_HEADS).reshape(1, d)]
        in_specs += [pl.BlockSpec((tm, d), row)] * 3 + [_const_spec((1, d))]
    else:
        args += [mixer_in]
        in_specs += [pl.BlockSpec((tm, d), row)]
    args += [mod, w_o16, norm_w.reshape(1, d), w_in16, w_out16]
    in_specs += [pl.BlockSpec((1, 1, 6 * d), _group_map(*geo)), _const_spec(w_o16.shape), _const_spec((1, d)),
                 _const_spec(w_in16.shape), _const_spec(w_out16.shape)]
    if final:
        args += [norm_final.reshape(1, d)]
        in_specs += [_const_spec((1, d))]
    c_args, c_in_specs, c_out_specs, c_out_shape = _cast_specs(cast_jobs, n_tiles)
    args += c_args
    in_specs += c_in_specs
    out_specs = [pl.BlockSpec((tm, d), row)] + c_out_specs
    out_shape = [jax.ShapeDtypeStruct((n_tiles * tm, d), f32)] + c_out_shape
    outs = pl.pallas_call(
        functools.partial(_post_kernel, d=d, gla=gla, final=final, hidden=hidden, n_x=n_x,
                          n_lat_tiles=geo[0], n_cast=len(cast_jobs)),
        grid=(n_tiles,),
        in_specs=in_specs,
        out_specs=out_specs,
        out_shape=out_shape,
        compiler_params=pltpu.CompilerParams(dimension_semantics=("arbitrary",),
                                             vmem_limit_bytes=VMEM_LIMIT),
        name="post_gla" if gla else "post_swa",
    )(*args)
    return outs[0], list(outs[1:])


def kernel(x, c, ctx, c_ctx, w_ada, b_ada, norm_mix, norm_ffn, gla_w_in, gla_w_gate_up, gla_b_gate_up,
           gla_w_o, gla_o_norm, swa_w_qkv, swa_b_qkv, swa_w_o, swa_sinks, ffn_w_in, ffn_w_out,
           norm_final):
    n_batch, seq, d = x.shape
    ctx_len = ctx.shape[1]
    depth = w_ada.shape[0]
    tm = ROW_TILE
    assert seq % tm == 0 and (n_batch * ctx_len) == tm
    n_lat_tiles = n_batch * seq // tm
    geo = (n_lat_tiles, seq // tm, n_batch)

    mod_all = _ada_mod(jnp.concatenate([c, c_ctx[None, :]], axis=0), w_ada, b_ada)
    t_all = n_batch * (seq + ctx_len)
    x_src = (x.reshape(n_batch * seq, d), ctx.reshape(n_batch * ctx_len, d))
    cos_l, sin_l = _rope_tables(seq, tm)

    def mixer_jobs(layer):
        jj = layer // 2
        mix = [(gla_w_o, jj)] if layer % 2 == 0 else [(swa_w_qkv, jj), (swa_w_o, jj)]
        return mix + [(ffn_w_in, layer), (ffn_w_out, layer)]

    gla_w_in16 = gla_w_in.astype(bf16)
    w16 = None
    for i in range(depth):
        last = i == depth - 1
        j = i // 2
        mod = mod_all[i].reshape(8, 1, 6 * d)
        n_tiles = n_lat_tiles if last else n_lat_tiles + 1
        nf = norm_final if last else None
        jobs = () if last else mixer_jobs(i + 1)
        if i % 2 == 0:
            (qk, r, vt, low), first_w16 = _gla_proj(x_src, t_all, mod, norm_mix[i], gla_w_in16, j, geo,
                                                    cast_jobs=mixer_jobs(i) if w16 is None else ())
            w16 = first_w16 if w16 is None else w16
        w_o16, w_ffn_in16, w_ffn_out16 = w16[-3:]
        if i % 2 == 0:
            o_f, o_b = _gla_scan(qk, vt, low, gla_w_gate_up[j], gla_b_gate_up[j], n_batch, seq, ctx_len)
            x_src, w16 = _post(x_src, (o_f, o_b, r), mod, w_o16, norm_ffn[i], w_ffn_in16, w_ffn_out16,
                               geo, True, nf, n_tiles, o_norm=gla_o_norm[j], cast_jobs=jobs)
        else:
            q, kx, vx = _swa_proj(x_src, mod, norm_mix[i], w16[0], swa_b_qkv[j], cos_l, sin_l, geo)
            attn = _swa_attention(q, kx, vx, swa_sinks[j], n_batch, seq, ctx_len, not last)
            x_src, w16 = _post(x_src, attn, mod, w_o16, norm_ffn[i], w_ffn_in16, w_ffn_out16,
                               geo, False, nf, n_tiles, cast_jobs=jobs)
    return x_src.reshape(n_batch, seq, d)
```

```python
import functools

import jax
import jax.numpy as jnp
from jax import lax
from jax.experimental import pallas as pl
from jax.experimental.pallas import tpu as pltpu

f32 = jnp.float32
bf16 = jnp.bfloat16

EPS = 1e-6
LOG2E = 1.4426950408889634
NEG_INF = -1e30
GRID_W = 64
ROPE_THETA = 10000.0

GLA_HEADS = 4
GLA_GATE_RANK = 16
GLA_GATE_NORM = 16.0
GLA_CHUNK = 64
GLA_BLOCK = 256
SCAN_SKEW_STAGES = 2
SCAN_SKEW_GROUP = 4

SWA_HEADS = 16
SWA_KV_HEADS = 2
SWA_HD = 64
SWA_WINDOW = 128
SWA_BLOCK = 128
ATTN_STACK = 1

ATTN_AHEAD = 3
ATTN_OVERLAP = 1

LANES = 128
FFN_CHUNK = 256
ROW_TILE = 512
SUB_TILES = 2
VMEM_LIMIT = 56 * 1024 * 1024


def _nt(a, b):
    return lax.dot_general(a, b, (((1,), (1,)), ((), ())), preferred_element_type=f32)


def _nn(a, b):
    return jnp.dot(a, b, preferred_element_type=f32)


def _silu(v):
    return v * (1.0 / (1.0 + jnp.exp(-v)))


def _rms(v, w):
    return v * lax.rsqrt(jnp.mean(v * v, axis=-1, keepdims=True) + EPS) * w


def _const_spec(shape):
    zeros = (0,) * len(shape)
    return pl.BlockSpec(shape, lambda *_: zeros, pipeline_mode=pl.Buffered(1))


def _ada_kernel(cb_ref, *refs, n_rows, d_model, tn, streams):
    w_refs, b_ref, out_ref = refs[:streams], refs[streams], refs[streams + 1]
    ncol = tn // LANES
    for s, w_ref in enumerate(w_refs):
        def body(i, accs, w_ref=w_ref):
            d0 = pl.multiple_of(i * 8, 8)
            w8 = w_ref[0, pl.ds(d0, 8), :]
            new = []
            for r in range(n_rows):
                s8 = _silu(cb_ref[r, pl.ds(d0, 8), :])
                new.append(tuple(accs[r][j] + w8[:, j * LANES:(j + 1) * LANES] * s8 for j in range(ncol)))
            return tuple(new)

        init = tuple(tuple(jnp.zeros((8, LANES), f32) for _ in range(ncol)) for _ in range(n_rows))
        accs = lax.fori_loop(0, d_model // 8, body, init, unroll=8)
        rows = [jnp.concatenate([jnp.sum(a, axis=0, keepdims=True) for a in accs[r]], axis=1)
                for r in range(n_rows)]
        rows += [jnp.zeros((1, tn), f32)] * (8 - n_rows)
        cols = slice(s * tn, (s + 1) * tn)
        out_ref[0, :, cols] = jnp.concatenate(rows, axis=0) + b_ref[0, :, cols]


def _ada_mod(cvecs, w_ada, b_ada):
    n_rows, d_model = cvecs.shape
    depth, _, e = w_ada.shape
    tn, streams = 1536, 2
    step = tn * streams
    cb = jnp.broadcast_to(cvecs[:, :, None], (n_rows, d_model, LANES))
    w_specs = [pl.BlockSpec((1, d_model, tn), lambda l, j, s=s: (l, 0, j * streams + s)) for s in range(streams)]
    return pl.pallas_call(
        functools.partial(_ada_kernel, n_rows=n_rows, d_model=d_model, tn=tn, streams=streams),
        grid=(depth, e // step),
        in_specs=[pl.BlockSpec((n_rows, d_model, LANES), lambda l, j: (0, 0, 0))] + w_specs + [
            pl.BlockSpec((1, 1, step), lambda l, j: (l, 0, j)),
        ],
        out_specs=pl.BlockSpec((1, 8, step), lambda l, j: (l, 0, j)),
        out_shape=jax.ShapeDtypeStruct((depth, 8, e), f32),
        compiler_params=pltpu.CompilerParams(dimension_semantics=("arbitrary", "arbitrary"),
                                             vmem_limit_bytes=VMEM_LIMIT),
        name="ada_mod",
    )(cb, *([w_ada] * streams), b_ada.reshape(depth, 1, e))


def _mod_slices(mod_ref, d, first):
    return tuple(mod_ref[0, :, (first + k) * d:(first + k + 1) * d] for k in range(3))


def _x_specs(x_src, n_lat_tiles):
    tm = ROW_TILE
    if isinstance(x_src, tuple):
        lat, ctx = x_src
        d = lat.shape[1]
        assert ctx.shape == (tm, d)
        return [lat, ctx], [pl.BlockSpec((tm, d), lambda i: (jnp.minimum(i, n_lat_tiles - 1), 0)),
                            pl.BlockSpec((tm, d), lambda i: (0, 0))]
    return [x_src], [pl.BlockSpec((tm, x_src.shape[1]), lambda i: (i, 0))]


def _load_x(x_refs, n_lat_tiles, rs=slice(None)):
    if len(x_refs) == 2:
        return jnp.where(pl.program_id(0) < n_lat_tiles, x_refs[0][rs, :], x_refs[1][rs, :])
    return x_refs[0][rs, :]


def _sub_tiles(rows):
    step = rows // SUB_TILES
    return [slice(s * step, (s + 1) * step) for s in range(SUB_TILES)]


def _group_map(n_lat_tiles, tiles_per_batch, n_batch):
    def index_map(i):
        return (jnp.where(i < n_lat_tiles, i // tiles_per_batch, n_batch), 0, 0)
    return index_map


CAST_BLOCKS = 16


def _cast_specs(cast_jobs, n_steps):
    n_blocks = CAST_BLOCKS
    while n_blocks > n_steps:
        n_blocks //= 2
    last_block = n_blocks - 1
    args, in_specs, out_specs, out_shape = [], [], [], []
    for stacked, layer in cast_jobs:
        _, rows, cols = stacked.shape
        assert rows % (16 * n_blocks) == 0
        br = rows // n_blocks
        args.append(stacked)
        in_specs.append(pl.BlockSpec((None, br, cols),
                                     lambda i, layer=layer: (layer, jnp.minimum(i, last_block), 0)))
        out_specs.append(pl.BlockSpec((br, cols), lambda i: (jnp.minimum(i, last_block), 0)))
        out_shape.append(jax.ShapeDtypeStruct((rows, cols), bf16))
    return args, in_specs, out_specs, out_shape


def _run_casts(cast_in, cast_out):
    for src_ref, dst_ref in zip(cast_in, cast_out):
        dst_ref[...] = src_ref[...].astype(bf16)


def _gla_proj_kernel(*refs, d, dk, dv, heads, n_x, n_lat_tiles, n_cast):
    x_refs, (mod_ref, nw_ref, win_ref), rest = refs[:n_x], refs[n_x:n_x + 3], refs[n_x + 3:]
    cast_in, (qk_ref, r_ref, vt_ref, low_ref), rest = rest[:n_cast], rest[n_cast:n_cast + 4], rest[n_cast + 4:]
    cast_out, wvt_ref = rest[:n_cast], rest[n_cast]
    _run_casts(cast_in, cast_out)

    @pl.when(pl.program_id(0) == 0)
    def _():
        wvt_ref[...] = win_ref[:, 2 * dk:2 * dk + dv].astype(f32).T.astype(bf16)

    sh, sc, _ = _mod_slices(mod_ref, d, 0)
    hk = dk // heads
    subs = _sub_tiles(r_ref.shape[0])
    hs = [(_rms(_load_x(x_refs, n_lat_tiles, rs), nw_ref[...]) * (1.0 + sc) + sh).astype(bf16) for rs in subs]
    for rs, h in zip(subs, hs):
        low_ref[rs, :] = _nn(h, win_ref[:, 2 * dk + 2 * dv:]).astype(bf16)
        r_ref[rs, :] = _silu(_nn(h, win_ref[:, 2 * dk + dv:2 * dk + 2 * dv])).astype(bf16)
        qk = _nn(h, win_ref[:, :2 * dk])
        q = qk[:, :dk] * (hk ** -0.5)
        for hh in range(heads):
            qk_ref[hh, rs, :] = q[:, hh * hk:(hh + 1) * hk].astype(bf16)
            qk_ref[heads + hh, rs, :] = qk[:, dk + hh * hk:dk + (hh + 1) * hk].astype(bf16)
        vt_ref[:, rs] = _nt(wvt_ref[...], h).astype(bf16)


def _gla_proj(x_src, t_all, mod, norm_w, w_in16, layer, geo, cast_jobs=()):
    d = w_in16.shape[1]
    dk, dv, heads = d // 2, d, GLA_HEADS
    hk = dk // heads
    tm = ROW_TILE
    x_args, x_specs = _x_specs(x_src, geo[0])
    c_args, c_in_specs, c_out_specs, c_out_shape = _cast_specs(cast_jobs, t_all // tm)
    kern = functools.partial(_gla_proj_kernel, d=d, dk=dk, dv=dv, heads=heads, n_x=len(x_args),
                             n_lat_tiles=geo[0], n_cast=len(cast_jobs))
    outs = pl.pallas_call(
        kern,
        grid=(t_all // tm,),
        in_specs=x_specs + [
            pl.BlockSpec((1, 1, 6 * d), _group_map(*geo)),
            _const_spec((1, d)),
            pl.BlockSpec((None,) + w_in16.shape[1:], lambda i: (layer, 0, 0), pipeline_mode=pl.Buffered(1)),
        ] + c_in_specs,
        out_specs=[
            pl.BlockSpec((2 * heads, tm, hk), lambda i: (0, i, 0)),
            pl.BlockSpec((tm, dv), lambda i: (i, 0)),
            pl.BlockSpec((dv, tm), lambda i: (0, i)),
            pl.BlockSpec((tm, 2 * GLA_GATE_RANK), lambda i: (i, 0)),
        ] + c_out_specs,
        out_shape=[
            jax.ShapeDtypeStruct((2 * heads, t_all, hk), bf16),
            jax.ShapeDtypeStruct((t_all, dv), bf16),
            jax.ShapeDtypeStruct((dv, t_all), bf16),
            jax.ShapeDtypeStruct((t_all, 2 * GLA_GATE_RANK), bf16),
        ] + c_out_shape,
        scratch_shapes=[pltpu.VMEM((dv, d), bf16)],
        compiler_params=pltpu.CompilerParams(dimension_semantics=("arbitrary",),
                                             vmem_limit_bytes=VMEM_LIMIT),
        name="gla_proj",
    )(*x_args, mod, norm_w.reshape(1, d), w_in16, *c_args)
    return outs[:4], list(outs[4:])


def _gla_block(q, k, vt, z2, s_ref, o_ref, reverse):
    tb, hk = q.shape
    c = GLA_CHUNK
    half = tb // 2
    assert tb == 4 * c
    row = lax.broadcasted_iota(jnp.int32, (tb, tb), 0)
    col = lax.broadcasted_iota(jnp.int32, (tb, tb), 1)
    causal = ((row // c) == (col // c)) & ((col >= row) if reverse else (col <= row))
    cum = jnp.where(causal, 1.0, 0.0).astype(bf16)
    g = (jnp.minimum(z2, 0.0) - jnp.log2(1.0 + jnp.exp2(-jnp.abs(z2)))) * (1.0 / GLA_GATE_NORM)
    g_hi = g.astype(bf16)
    g_lo = (g - g_hi.astype(f32)).astype(bf16)
    b2 = _nn(cum, jnp.concatenate([g_hi, g_lo], axis=1))
    yield
    b = b2[:, :hk] + b2[:, hk:]
    order = (3, 2, 1, 0) if reverse else (0, 1, 2, 3)
    pos = {ch: s for s, ch in enumerate(order)}
    tot = []
    for ch in range(4):
        last = ch * c if reverse else ch * c + c - 1
        tot.append(b[last:last + 1, :])
    zero = jnp.zeros_like(tot[0])
    before = [sum((tot[order[s2]] for s2 in range(pos[ch])), zero) for ch in range(4)]
    after = [sum((tot[order[s2]] for s2 in range(pos[ch] + 1, 4)), zero) for ch in range(4)]

    def rows(vals):
        return jnp.concatenate([jnp.broadcast_to(v, (c, hk)) for v in vals], axis=0)

    def row_scales(vals):
        return rows([jnp.exp2(v) for v in vals])

    kf = k.astype(f32)
    qd = q.astype(f32) * jnp.exp2(b)
    ki = kf * jnp.exp2(-b)
    ke = kf * jnp.exp2(rows(tot) - b)
    qd16, ki16, ke16 = qd.astype(bf16), ki.astype(bf16), ke.astype(bf16)
    yield

    lrow = lax.broadcasted_iota(jnp.int32, (half, half), 0)
    lcol = lax.broadcasted_iota(jnp.int32, (half, half), 1)
    m_same = ((lrow // c) == (lcol // c)) & ((lcol >= lrow) if reverse else (lcol <= lrow))
    m_next = ((lrow < c) & (lcol >= c)) if reverse else ((lrow >= c) & (lcol < c))
    xs = [_nt(qd16[lo:lo + half], jnp.concatenate([ki16[lo:lo + half], ke16[lo:lo + half]], axis=0))
          for lo in (0, half)]

    late, early = (0, half) if reverse else (half, 0)
    q_far = order[3] * c - late
    k_far = order[0] * c - early
    q_scale = [zero, zero]
    q_scale[q_far // c] = tot[order[2]]
    k_scale = [zero, zero]
    k_scale[k_far // c] = tot[order[1]]
    q2 = (qd[late:late + half] * row_scales(q_scale)).astype(bf16)
    k2 = (ke[early:early + half] * row_scales(k_scale)).astype(bf16)
    cross_f32 = _nt(q2, k2)
    s = s_ref[...]
    inter = _nt((qd * row_scales(before)).astype(bf16), s.astype(bf16))
    k_end = (ke * row_scales(after)).astype(bf16)
    upd = _nn(vt, k_end)
    yield
    cross = cross_f32.astype(bf16)

    diag = [jnp.where(m_same, x[:, :half], jnp.where(m_next, x[:, half:], 0.0)).astype(bf16) for x in xs]
    if reverse:
        p_early = diag[1]
        p_late = jnp.concatenate([diag[0], cross], axis=1)
    else:
        p_early = diag[0]
        p_late = jnp.concatenate([cross, diag[1]], axis=1)
    pv_early = _nt(p_early, vt[:, early:early + half])
    pv_late = _nt(p_late, vt)
    yield
    o_ref[early:early + half, :] = (pv_early + inter[early:early + half]).astype(o_ref.dtype)
    o_ref[late:late + half, :] = (pv_late + inter[late:late + half]).astype(o_ref.dtype)
    s_ref[...] = s * jnp.exp2(before[order[3]] + tot[order[3]]) + upd


def _gla_scan_kernel(qkf_ref, vtf_ref, lowf_ref, qkb_ref, vtb_ref, lowb_ref, wup_ref, bup_ref,
                     of_ref, ob_ref, s_ref, *, heads):
    @pl.when(pl.program_id(1) == 0)
    def _():
        s_ref[...] = jnp.zeros_like(s_ref)

    hv = vtf_ref.shape[0] // heads
    hk = qkf_ref.shape[2]
    dk = heads * hk

    def gate_preact(low_ref, direction):
        cols = slice(direction * dk, (direction + 1) * dk)
        return _nn(low_ref[...], wup_ref[:, cols]) + bup_ref[:, cols]

    g_f = gate_preact(lowf_ref, 0)
    g_b = gate_preact(lowb_ref, 1)
    chains = []
    for hh in range(heads):
        cols = slice(hh * hv, (hh + 1) * hv)
        gcols = slice(hh * hk, (hh + 1) * hk)
        chains.append(_gla_block(qkf_ref[hh], qkf_ref[heads + hh], vtf_ref[cols, :], g_f[:, gcols],
                                 s_ref.at[0, hh], of_ref.at[:, cols], False))
        chains.append(_gla_block(qkb_ref[hh], qkb_ref[heads + hh], vtb_ref[cols, :], g_b[:, gcols],
                                 s_ref.at[1, hh], ob_ref.at[:, cols], True))
    groups = [chains[k:k + SCAN_SKEW_GROUP] for k in range(0, len(chains), SCAN_SKEW_GROUP)]
    tick = 0
    while any(groups):
        for k, group in enumerate(groups):
            if tick < k * SCAN_SKEW_STAGES:
                continue
            alive = []
            for chain in group:
                try:
                    next(chain)
                    alive.append(chain)
                except StopIteration:
                    pass
            groups[k] = alive
        tick += 1


def _gla_scan(qk, vt, low, w_gate_up, b_gate_up, n_batch, seq, ctx_len):
    heads = GLA_HEADS
    _, t_all, hk = qk.shape
    dk = heads * hk
    dv = vt.shape[0]
    hv = dv // heads
    rank = GLA_GATE_RANK
    tb = GLA_BLOCK
    assert ctx_len == tb and seq % tb == 0
    nlat = seq // tb
    ctx0 = n_batch * nlat
    wup = jnp.zeros((2 * rank, 2 * dk), f32)
    wup = (wup.at[:rank, :dk].set(w_gate_up[0]).at[rank:, dk:].set(w_gate_up[1]) * LOG2E).astype(bf16)
    bup = b_gate_up.reshape(1, 2 * dk) * LOG2E

    def fwd_row(b, t):
        return jnp.where(t == 0, ctx0 + b, b * nlat + t - 1)

    def bwd_row(b, t):
        return jnp.where(t == 0, ctx0 + b, b * nlat + nlat - t)

    def specs(row_fn):
        return [
            pl.BlockSpec((2 * heads, tb, hk), lambda b, t: (0, row_fn(b, t), 0)),
            pl.BlockSpec((dv, tb), lambda b, t: (0, row_fn(b, t))),
            pl.BlockSpec((tb, 2 * rank), lambda b, t: (row_fn(b, t), 0)),
        ]

    return pl.pallas_call(
        functools.partial(_gla_scan_kernel, heads=heads),
        grid=(n_batch, nlat + 1),
        in_specs=specs(fwd_row) + specs(bwd_row) + [_const_spec((2 * rank, 2 * dk)), _const_spec((1, 2 * dk))],
        out_specs=[
            pl.BlockSpec((tb, dv), lambda b, t: (fwd_row(b, t), 0)),
            pl.BlockSpec((tb, dv), lambda b, t: (bwd_row(b, t), 0)),
        ],
        out_shape=[jax.ShapeDtypeStruct((t_all, dv), bf16)] * 2,
        scratch_shapes=[pltpu.VMEM((2, heads, hv, hk), f32)],
        compiler_params=pltpu.CompilerParams(dimension_semantics=("arbitrary", "arbitrary"),
                                             vmem_limit_bytes=VMEM_LIMIT),
        name="gla_scan",
    )(qk, vt, low, qk, vt, low, wup, bup)


def _swa_proj_kernel(x_ref, mod_ref, nw_ref, w_ref, b_ref, cos_ref, sin_ref,
                     q_ref, kx_ref, vx_ref, *, d, nq):
    sh, sc, _ = _mod_slices(mod_ref, d, 0)
    subs = _sub_tiles(x_ref.shape[0])
    hs = [(_rms(x_ref[rs, :], nw_ref[...]) * (1.0 + sc) + sh).astype(bf16) for rs in subs]
    half = SWA_HD // 2
    for rs, h in zip(subs, hs):
        qkv = _nn(h, w_ref[...]) + b_ref[...]
        cos = cos_ref[rs, :]
        sin = sin_ref[rs, :]
        lane = lax.broadcasted_iota(jnp.int32, cos.shape, 1)
        first_half = (lane % SWA_HD) < half
        low_head = lane < SWA_HD

        def rope(v):
            partner = jnp.where(first_half, pltpu.roll(v, LANES - half, 1), pltpu.roll(v, half, 1))
            return v * cos + partner * sin

        for j in range(nq // LANES):
            q_ref[rs, j * LANES:(j + 1) * LANES] = (
                rope(qkv[:, j * LANES:(j + 1) * LANES]) * (SWA_HD ** -0.5 * LOG2E)).astype(bf16)
        k = rope(qkv[:, nq:nq + LANES])
        v = qkv[:, nq + LANES:nq + 2 * LANES]
        for src_val, dst in ((k, kx_ref), (v, vx_ref)):
            swapped = pltpu.roll(src_val, SWA_HD, 1)
            dst[0, rs, :] = jnp.where(low_head, src_val, 0.0).astype(bf16)
            dst[1, rs, :] = jnp.where(low_head, 0.0, swapped).astype(bf16)
            dst[2, rs, :] = jnp.where(low_head, swapped, 0.0).astype(bf16)
            dst[3, rs, :] = jnp.where(low_head, 0.0, src_val).astype(bf16)


def _rope_tables(seq, extra_rows):
    rows = seq // GRID_W
    row = jnp.broadcast_to(jnp.arange(rows)[:, None], (rows, GRID_W)).reshape(-1).astype(f32)
    col = jnp.broadcast_to(jnp.arange(GRID_W)[None, :], (rows, GRID_W)).reshape(-1).astype(f32)
    n_freq = SWA_HD // 4
    inv = ROPE_THETA ** (-jnp.arange(n_freq, dtype=f32) / n_freq)
    ang = jnp.concatenate([row[:, None] * inv, col[:, None] * inv], axis=-1)
    cos, sin = jnp.cos(ang), jnp.sin(ang)
    reps = LANES // SWA_HD
    cos_l = jnp.tile(jnp.concatenate([cos, cos], axis=-1), (1, reps))
    sin_l = jnp.tile(jnp.concatenate([-sin, sin], axis=-1), (1, reps))
    cos_l = jnp.concatenate([cos_l, jnp.ones((extra_rows, LANES), f32)], axis=0)
    sin_l = jnp.concatenate([sin_l, jnp.zeros((extra_rows, LANES), f32)], axis=0)
    return cos_l, sin_l


def _swa_proj(x_all, mod, norm_w, w_qkv16, b_qkv, cos_l, sin_l, geo):
    t_all, d = x_all.shape
    nq = SWA_HEADS * SWA_HD
    nqkv = w_qkv16.shape[1]
    tm = ROW_TILE
    n_lat_tiles, tiles_per_batch, _ = geo

    def pos_map(i):
        return (jnp.where(i < n_lat_tiles, i % tiles_per_batch, tiles_per_batch), 0)

    return pl.pallas_call(
        functools.partial(_swa_proj_kernel, d=d, nq=nq),
        grid=(t_all // tm,),
        in_specs=[
            pl.BlockSpec((tm, d), lambda i: (i, 0)),
            pl.BlockSpec((1, 1, 6 * d), _group_map(*geo)),
            _const_spec((1, d)),
            _const_spec(w_qkv16.shape),
            _const_spec((1, nqkv)),
            pl.BlockSpec((tm, LANES), pos_map),
            pl.BlockSpec((tm, LANES), pos_map),
        ],
        out_specs=[
            pl.BlockSpec((tm, nq), lambda i: (i, 0)),
            pl.BlockSpec((4, tm, LANES), lambda i: (0, i, 0)),
            pl.BlockSpec((4, tm, LANES), lambda i: (0, i, 0)),
        ],
        out_shape=[
            jax.ShapeDtypeStruct((t_all, nq), bf16),
            jax.ShapeDtypeStruct((4, t_all, LANES), bf16),
            jax.ShapeDtypeStruct((4, t_all, LANES), bf16),
        ],
        compiler_params=pltpu.CompilerParams(dimension_semantics=("arbitrary",),
                                             vmem_limit_bytes=VMEM_LIMIT),
        name="swa_proj",
    )(x_all, mod, norm_w.reshape(1, d), w_qkv16, b_qkv.reshape(1, nqkv), cos_l, sin_l)


def _attend(q_ref, o_ref, sink_ref, k_pieces, v_pieces, edge_masks):
    tq = q_ref.shape[0]
    pairs = q_ref.shape[1] // LANES
    group = pairs // SWA_KV_HEADS
    stack = ATTN_STACK
    lane = lax.broadcasted_iota(jnp.int32, (tq, LANES), 1)
    low_head = lane < SWA_HD
    k_cat, v_cat = [], []
    for g in range(SWA_KV_HEADS):
        k_cat.append(jnp.concatenate(k_pieces[g][0] + k_pieces[g][1], axis=0))
        nk = k_cat[g].shape[0] // 2
        lane2 = lax.broadcasted_iota(jnp.int32, (nk, LANES), 1)
        ones_lo = jnp.where(lane2 < SWA_HD, 1.0, 0.0).astype(bf16)
        ones_hi = jnp.where(lane2 < SWA_HD, 0.0, 1.0).astype(bf16)
        v_cat.append(jnp.concatenate([
            jnp.concatenate([jnp.concatenate(v_pieces[g][0], axis=0), ones_lo], axis=1),
            jnp.concatenate([jnp.concatenate(v_pieces[g][1], axis=0), ones_hi], axis=1)], axis=0))
    units = [(u * stack) // group for u in range(pairs // stack)]

    def scores(u):
        q_stack = jnp.concatenate(
            [q_ref[:, (u * stack + jj) * LANES:(u * stack + jj + 1) * LANES] for jj in range(stack)], axis=0)
        return _nt(q_stack, k_cat[units[u]])

    ahead = [scores(u) for u in range(min(ATTN_AHEAD, len(units)))]
    yield
    for u, g in enumerate(units):
        s_all = ahead.pop(0)
        if u + ATTN_AHEAD < len(units):
            ahead.append(scores(u + ATTN_AHEAD))
        prob_rows, sink_rows = [], []
        for jj in range(stack):
            j = u * stack + jj
            probs, sink_terms = [], []
            for p in range(2):
                sp = s_all[jj * tq:(jj + 1) * tq, p * nk:(p + 1) * nk]
                if edge_masks is not None:
                    before, after = edge_masks
                    blk = before.shape[1]
                    sp = jnp.concatenate([jnp.where(before, sp[:, :blk], NEG_INF), sp[:, blk:2 * blk],
                                          jnp.where(after, sp[:, 2 * blk:3 * blk], NEG_INF), sp[:, 3 * blk:]],
                                         axis=1)
                sink = sink_ref[2 * j + p] * LOG2E
                m = jnp.maximum(jnp.max(sp, axis=-1, keepdims=True), sink)
                probs.append(jnp.exp2(sp - m).astype(bf16))
                sink_terms.append(jnp.exp2(sink - m))
            prob_rows.append(jnp.concatenate(probs, axis=1))
            sink_rows.append(jnp.where(low_head, sink_terms[0], sink_terms[1]))
        out_all = _nn(jnp.concatenate(prob_rows, axis=0), v_cat[g])
        for jj in range(stack):
            j = u * stack + jj
            out = out_all[jj * tq:(jj + 1) * tq]
            den = out[:, LANES:] + sink_rows[jj]
            o_ref[:, j * LANES:(j + 1) * LANES] = (out[:, :LANES] / den).astype(bf16)
        yield


def _trace_in_order(chains):
    chains = list(chains)
    stages = [list(range(SWA_HEADS // 2 // ATTN_STACK + 1)) for _ in chains]
    for c, chain in enumerate(chains):
        for s in stages[c]:
            if not (c > 0 and s == 0):
                next(chain)
            if c + 1 < len(chains) and s == len(stages[c]) - 1 - ATTN_OVERLAP:
                next(chains[c + 1])


def _swa_kernel(sink_ref, q_ref, kp_ref, ko_ref, kn_ref, kc_ref, vp_ref, vo_ref, vn_ref, vc_ref,
                o_ref, *, seq, n_lat_steps, need_ctx):
    i = pl.program_id(1)
    blk = SWA_BLOCK

    def pieces(refs_of_half):
        return [[[load(2 * g + p) for load in refs_of_half] for p in range(2)] for g in range(SWA_KV_HEADS)]

    def rows(ref, lo):
        return lambda idx: ref[idx, lo:lo + blk, :]

    @pl.when(i < n_lat_steps)
    def _():
        halves = []
        for half in range(2):
            first = 2 * i + half
            qi = first * blk + lax.broadcasted_iota(jnp.int32, (blk, blk), 0)
            local = lax.broadcasted_iota(jnp.int32, (blk, blk), 1)
            kj_before = (first - 1) * blk + local
            kj_after = (first + 1) * blk + local
            masks = ((qi - kj_before <= SWA_WINDOW) & (kj_before >= 0),
                     (kj_after - qi <= SWA_WINDOW) & (kj_after < seq))
            if half == 0:
                k_loads = (rows(kp_ref, 0), rows(ko_ref, 0), rows(ko_ref, blk), lambda idx: kc_ref[idx])
                v_loads = (rows(vp_ref, 0), rows(vo_ref, 0), rows(vo_ref, blk), lambda idx: vc_ref[idx])
            else:
                k_loads = (rows(ko_ref, 0), rows(ko_ref, blk), rows(kn_ref, 0), lambda idx: kc_ref[idx])
                v_loads = (rows(vo_ref, 0), rows(vo_ref, blk), rows(vn_ref, 0), lambda idx: vc_ref[idx])
            rs = slice(half * blk, (half + 1) * blk)
            halves.append(_attend(q_ref.at[rs, :], o_ref.at[rs, :], sink_ref, pieces(k_loads), pieces(v_loads),
                                  masks))
        _trace_in_order(halves)

    if need_ctx:
        @pl.when(i >= n_lat_steps)
        def _():
            halves = []
            for half in range(2):
                rs = slice(half * blk, (half + 1) * blk)
                halves.append(_attend(q_ref.at[rs, :], o_ref.at[rs, :], sink_ref, pieces((lambda idx: kc_ref[idx],)),
                                      pieces((lambda idx: vc_ref[idx],)), None))
            _trace_in_order(halves)


def _swa_attention(q, kx, vx, sinks, n_batch, seq, ctx_len, need_ctx):
    t_all, nq = q.shape
    blk = SWA_BLOCK
    step = 2 * blk
    assert seq % step == 0 and ctx_len % step == 0
    nb = seq // blk
    n_lat_steps = seq // step
    n_ctx_steps = ctx_len // step
    out_rows = t_all if need_ctx else n_batch * seq
    smem = pl.BlockSpec(memory_space=pltpu.SMEM)

    def edge(off):
        return pl.BlockSpec((4, blk, LANES),
                            lambda b, i: (0, b * nb + jnp.clip(2 * i + off, 0, nb - 1), 0))

    def own():
        return pl.BlockSpec((4, step, LANES),
                            lambda b, i: (0, b * n_lat_steps + jnp.minimum(i, n_lat_steps - 1), 0))

    def ctx_spec():
        return pl.BlockSpec((4, ctx_len, LANES), lambda b, i: (0, (n_batch * seq) // ctx_len + b, 0))

    def q_row(b, i):
        return (jnp.where(i < n_lat_steps, b * n_lat_steps + i,
                          n_batch * n_lat_steps + b * n_ctx_steps + i - n_lat_steps), 0)

    return pl.pallas_call(
        functools.partial(_swa_kernel, seq=seq, n_lat_steps=n_lat_steps, need_ctx=need_ctx),
        grid=(n_batch, n_lat_steps + n_ctx_steps if need_ctx else n_lat_steps),
        in_specs=[smem, pl.BlockSpec((step, nq), q_row),
                  edge(-1), own(), edge(2), ctx_spec(), edge(-1), own(), edge(2), ctx_spec()],
        out_specs=pl.BlockSpec((step, nq), q_row),
        out_shape=jax.ShapeDtypeStruct((out_rows, nq), bf16),
        compiler_params=pltpu.CompilerParams(dimension_semantics=("arbitrary", "arbitrary"),
                                             vmem_limit_bytes=VMEM_LIMIT),
        name="swa_attn",
    )(sinks, q, kx, kx, kx, kx, vx, vx, vx, vx)


def _post_kernel(*refs, d, gla, final, hidden, n_x, n_lat_tiles, n_cast):
    x_refs, refs = refs[:n_x], refs[n_x:]
    if gla:
        of_ref, ob_ref, r_ref, on_ref = refs[:4]
        rest = refs[4:]
    else:
        a_ref = refs[0]
        rest = refs[1:]
    mod_ref, wo_ref, nw_ref, win_ref, wout_ref = rest[:5]
    rest = rest[5:]
    if final:
        nf_ref, rest = rest[0], rest[1:]
    cast_in, out_ref, cast_out = rest[:n_cast], rest[n_cast], rest[n_cast + 1:]
    _run_casts(cast_in, cast_out)

    _, _, g1 = _mod_slices(mod_ref, d, 0)
    sh2, sc2, g2 = _mod_slices(mod_ref, d, 3)

    def mixer_residual(rs):
        if gla:
            o = of_ref[rs, :].astype(f32) + ob_ref[rs, :].astype(f32)
            hv = d // GLA_HEADS
            parts = []
            for hh in range(GLA_HEADS):
                seg = o[:, hh * hv:(hh + 1) * hv]
                parts.append(_rms(seg, on_ref[:, hh * hv:(hh + 1) * hv]))
            a = (jnp.concatenate(parts, axis=1) * r_ref[rs, :].astype(f32)).astype(bf16)
        else:
            a = a_ref[rs, :]
        x1 = _load_x(x_refs, n_lat_tiles, rs) + g1 * _nn(a, wo_ref[...])
        return x1, (_rms(x1, nw_ref[...]) * (1.0 + sc2) + sh2).astype(bf16)

    subs = _sub_tiles(out_ref.shape[0])
    pre = [mixer_residual(rs) for rs in subs]
    for rs, (x1, hn) in zip(subs, pre):
        acts = []
        for off in range(0, hidden, FFN_CHUNK):
            gate = _nn(hn, win_ref[:, off:off + FFN_CHUNK])
            up = _nn(hn, win_ref[:, hidden + off:hidden + off + FFN_CHUNK])
            acts.append((_silu(gate) * up).astype(bf16))
        x2 = x1 + g2 * _nn(jnp.concatenate(acts, axis=1), wout_ref[...])
        if final:
            x2 = _rms(x2, nf_ref[...])
        out_ref[rs, :] = x2


def _post(x_src, mixer_in, mod, w_o16, norm_w, w_in16, w_out16, geo, gla, norm_final, n_tiles, o_norm=None,
          cast_jobs=()):
    hidden, d = w_out16.shape
    assert hidden % FFN_CHUNK == 0
    tm = ROW_TILE
    final = norm_final is not None
    row = lambda i: (i, 0)
    args, in_specs = _x_specs(x_src, geo[0])
    n_x = len(args)
    if gla:
        o_f, o_b, r = mixer_in
        args += [o_f, o_b, r, jnp.tile(o_norm, GLA_HEADS).reshape(1, d)]
        in_specs += [pl.BlockSpec((tm, d), row)] * 3 + [_const_spec((1, d))]
    else:
        args += [mixer_in]
        in_specs += [pl.BlockSpec((tm, d), row)]
    args += [mod, w_o16, norm_w.reshape(1, d), w_in16, w_out16]
    in_specs += [pl.BlockSpec((1, 1, 6 * d), _group_map(*geo)), _const_spec(w_o16.shape), _const_spec((1, d)),
                 _const_spec(w_in16.shape), _const_spec(w_out16.shape)]
    if final:
        args += [norm_final.reshape(1, d)]
        in_specs += [_const_spec((1, d))]
    c_args, c_in_specs, c_out_specs, c_out_shape = _cast_specs(cast_jobs, n_tiles)
    args += c_args
    in_specs += c_in_specs
    out_specs = [pl.BlockSpec((tm, d), row)] + c_out_specs
    out_shape = [jax.ShapeDtypeStruct((n_tiles * tm, d), f32)] + c_out_shape
    outs = pl.pallas_call(
        functools.partial(_post_kernel, d=d, gla=gla, final=final, hidden=hidden, n_x=n_x,
                          n_lat_tiles=geo[0], n_cast=len(cast_jobs)),
        grid=(n_tiles,),
        in_specs=in_specs,
        out_specs=out_specs,
        out_shape=out_shape,
        compiler_params=pltpu.CompilerParams(dimension_semantics=("arbitrary",),
                                             vmem_limit_bytes=VMEM_LIMIT),
        name="post_gla" if gla else "post_swa",
    )(*args)
    return outs[0], list(outs[1:])


def kernel(x, c, ctx, c_ctx, w_ada, b_ada, norm_mix, norm_ffn, gla_w_in, gla_w_gate_up, gla_b_gate_up,
           gla_w_o, gla_o_norm, swa_w_qkv, swa_b_qkv, swa_w_o, swa_sinks, ffn_w_in, ffn_w_out,
           norm_final):
    n_batch, seq, d = x.shape
    ctx_len = ctx.shape[1]
    depth = w_ada.shape[0]
    tm = ROW_TILE
    assert seq % tm == 0 and (n_batch * ctx_len) == tm
    n_lat_tiles = n_batch * seq // tm
    geo = (n_lat_tiles, seq // tm, n_batch)

    mod_all = _ada_mod(jnp.concatenate([c, c_ctx[None, :]], axis=0), w_ada, b_ada)
    t_all = n_batch * (seq + ctx_len)
    x_src = (x.reshape(n_batch * seq, d), ctx.reshape(n_batch * ctx_len, d))
    cos_l, sin_l = _rope_tables(seq, tm)

    def mixer_jobs(layer):
        jj = layer // 2
        mix = [(gla_w_o, jj)] if layer % 2 == 0 else [(swa_w_qkv, jj), (swa_w_o, jj)]
        return mix + [(ffn_w_in, layer), (ffn_w_out, layer)]

    gla_w_in16 = gla_w_in.astype(bf16)
    w16 = None
    for i in range(depth):
        last = i == depth - 1
        j = i // 2
        mod = mod_all[i].reshape(8, 1, 6 * d)
        n_tiles = n_lat_tiles if last else n_lat_tiles + 1
        nf = norm_final if last else None
        jobs = () if last else mixer_jobs(i + 1)
        if i % 2 == 0:
            (qk, r, vt, low), first_w16 = _gla_proj(x_src, t_all, mod, norm_mix[i], gla_w_in16, j, geo,
                                                    cast_jobs=mixer_jobs(i) if w16 is None else ())
            w16 = first_w16 if w16 is None else w16
        w_o16, w_ffn_in16, w_ffn_out16 = w16[-3:]
        if i % 2 == 0:
            o_f, o_b = _gla_scan(qk, vt, low, gla_w_gate_up[j], gla_b_gate_up[j], n_batch, seq, ctx_len)
            x_src, w16 = _post(x_src, (o_f, o_b, r), mod, w_o16, norm_ffn[i], w_ffn_in16, w_ffn_out16,
                               geo, True, nf, n_tiles, o_norm=gla_o_norm[j], cast_jobs=jobs)
        else:
            q, kx, vx = _swa_proj(x_src, mod, norm_mix[i], w16[0], swa_b_qkv[j], cos_l, sin_l, geo)
            attn = _swa_attention(q, kx, vx, swa_sinks[j], n_batch, seq, ctx_len, not last)
            x_src, w16 = _post(x_src, attn, mod, w_o16, norm_ffn[i], w_ffn_in16, w_ffn_out16,
                               geo, False, nf, n_tiles, cast_jobs=jobs)
    return x_src.reshape(n_batch, seq, d)
```

```python
import functools

import jax
import jax.numpy as jnp
import numpy as np
from jax import lax
from jax.experimental import pallas as pl
from jax.experimental.pallas import tpu as pltpu

f32 = jnp.float32
bf16 = jnp.bfloat16

EPS = 1e-6
LOG2E = 1.4426950408889634
NEG_INF = -1e30
GRID_W = 64
ROPE_THETA = 10000.0

GLA_HEADS = 4
GLA_GATE_RANK = 16
GLA_GATE_NORM = 16.0
GLA_CHUNK = 64
GLA_BLOCK = 256
SCAN_SKEW_STAGES = 2
SCAN_SKEW_GROUP = 4

SWA_HEADS = 16
SWA_KV_HEADS = 2
SWA_HD = 64
SWA_WINDOW = 128
SWA_BLOCK = 128
ATTN_STACK = 1

ATTN_AHEAD = 3
ATTN_OVERLAP = 1

LANES = 128
FFN_CHUNK = 256
ROW_TILE = 512
SUB_TILES = 2
VMEM_LIMIT = 56 * 1024 * 1024


def _nt(a, b):
    return lax.dot_general(a, b, (((1,), (1,)), ((), ())), preferred_element_type=f32)


def _nn(a, b):
    return jnp.dot(a, b, preferred_element_type=f32)


def _silu(v):
    return v * (1.0 / (1.0 + jnp.exp(-v)))


def _rms(v, w):
    return v * lax.rsqrt(jnp.mean(v * v, axis=-1, keepdims=True) + EPS) * w


def _const_spec(shape):
    zeros = (0,) * len(shape)
    return pl.BlockSpec(shape, lambda *_: zeros, pipeline_mode=pl.Buffered(1))


def _ada_kernel(cb_ref, *refs, n_rows, d_model, tn, streams):
    w_refs, b_ref, out_ref = refs[:streams], refs[streams], refs[streams + 1]
    ncol = tn // LANES
    for s, w_ref in enumerate(w_refs):
        def body(i, accs, w_ref=w_ref):
            d0 = pl.multiple_of(i * 8, 8)
            w8 = w_ref[0, pl.ds(d0, 8), :]
            new = []
            for r in range(n_rows):
                s8 = _silu(cb_ref[r, pl.ds(d0, 8), :])
                new.append(tuple(accs[r][j] + w8[:, j * LANES:(j + 1) * LANES] * s8 for j in range(ncol)))
            return tuple(new)

        init = tuple(tuple(jnp.zeros((8, LANES), f32) for _ in range(ncol)) for _ in range(n_rows))
        accs = lax.fori_loop(0, d_model // 8, body, init, unroll=8)
        rows = [jnp.concatenate([jnp.sum(a, axis=0, keepdims=True) for a in accs[r]], axis=1)
                for r in range(n_rows)]
        rows += [jnp.zeros((1, tn), f32)] * (8 - n_rows)
        cols = slice(s * tn, (s + 1) * tn)
        out_ref[0, :, cols] = jnp.concatenate(rows, axis=0) + b_ref[0, :, cols]


def _ada_mod(cvecs, w_ada, b_ada):
    n_rows, d_model = cvecs.shape
    depth, _, e = w_ada.shape
    tn, streams = 1536, 2
    step = tn * streams
    cb = jnp.broadcast_to(cvecs[:, :, None], (n_rows, d_model, LANES))
    w_specs = [pl.BlockSpec((1, d_model, tn), lambda l, j, s=s: (l, 0, j * streams + s)) for s in range(streams)]
    return pl.pallas_call(
        functools.partial(_ada_kernel, n_rows=n_rows, d_model=d_model, tn=tn, streams=streams),
        grid=(depth, e // step),
        in_specs=[pl.BlockSpec((n_rows, d_model, LANES), lambda l, j: (0, 0, 0))] + w_specs + [
            pl.BlockSpec((1, 1, step), lambda l, j: (l, 0, j)),
        ],
        out_specs=pl.BlockSpec((1, 8, step), lambda l, j: (l, 0, j)),
        out_shape=jax.ShapeDtypeStruct((depth, 8, e), f32),
        compiler_params=pltpu.CompilerParams(dimension_semantics=("arbitrary", "arbitrary"),
                                             vmem_limit_bytes=VMEM_LIMIT),
        name="ada_mod",
    )(cb, *([w_ada] * streams), b_ada.reshape(depth, 1, e))


def _mod_slices(mod_ref, d, first):
    return tuple(mod_ref[0, :, (first + k) * d:(first + k + 1) * d] for k in range(3))


def _x_specs(x_src, n_lat_tiles):
    tm = ROW_TILE
    if isinstance(x_src, tuple):
        lat, ctx = x_src
        d = lat.shape[1]
        assert ctx.shape == (tm, d)
        return [lat, ctx], [pl.BlockSpec((tm, d), lambda i: (jnp.minimum(i, n_lat_tiles - 1), 0)),
                            pl.BlockSpec((tm, d), lambda i: (0, 0))]
    return [x_src], [pl.BlockSpec((tm, x_src.shape[1]), lambda i: (i, 0))]


def _load_x(x_refs, n_lat_tiles, rs=slice(None)):
    if len(x_refs) == 2:
        return jnp.where(pl.program_id(0) < n_lat_tiles, x_refs[0][rs, :], x_refs[1][rs, :])
    return x_refs[0][rs, :]


def _sub_tiles(rows):
    step = rows // SUB_TILES
    return [slice(s * step, (s + 1) * step) for s in range(SUB_TILES)]


def _group_map(n_lat_tiles, tiles_per_batch, n_batch):
    def index_map(i):
        return (jnp.where(i < n_lat_tiles, i // tiles_per_batch, n_batch), 0, 0)
    return index_map


CAST_BLOCKS = 16


def _cast_specs(cast_jobs, n_steps):
    n_blocks = CAST_BLOCKS
    while n_blocks > n_steps:
        n_blocks //= 2
    last_block = n_blocks - 1
    args, in_specs, out_specs, out_shape = [], [], [], []
    for stacked, layer in cast_jobs:
        _, rows, cols = stacked.shape
        assert rows % (16 * n_blocks) == 0
        br = rows // n_blocks
        args.append(stacked)
        in_specs.append(pl.BlockSpec((None, br, cols),
                                     lambda i, layer=layer: (layer, jnp.minimum(i, last_block), 0)))
        out_specs.append(pl.BlockSpec((br, cols), lambda i: (jnp.minimum(i, last_block), 0)))
        out_shape.append(jax.ShapeDtypeStruct((rows, cols), bf16))
    return args, in_specs, out_specs, out_shape


def _run_casts(cast_in, cast_out):
    for src_ref, dst_ref in zip(cast_in, cast_out):
        dst_ref[...] = src_ref[...].astype(bf16)


def _gla_proj_kernel(*refs, d, dk, dv, heads, n_x, n_lat_tiles, n_cast):
    x_refs, (mod_ref, nw_ref, win_ref), rest = refs[:n_x], refs[n_x:n_x + 3], refs[n_x + 3:]
    cast_in, (qk_ref, r_ref, vt_ref, low_ref), rest = rest[:n_cast], rest[n_cast:n_cast + 4], rest[n_cast + 4:]
    cast_out, wvt_ref = rest[:n_cast], rest[n_cast]
    _run_casts(cast_in, cast_out)

    @pl.when(pl.program_id(0) == 0)
    def _():
        wvt_ref[...] = win_ref[:, 2 * dk:2 * dk + dv].astype(f32).T.astype(bf16)

    sh, sc, _ = _mod_slices(mod_ref, d, 0)
    hk = dk // heads
    subs = _sub_tiles(r_ref.shape[0])
    hs = [(_rms(_load_x(x_refs, n_lat_tiles, rs), nw_ref[...]) * (1.0 + sc) + sh).astype(bf16) for rs in subs]
    for rs, h in zip(subs, hs):
        low_ref[rs, :] = _nn(h, win_ref[:, 2 * dk + 2 * dv:]).astype(bf16)
        r_ref[rs, :] = _silu(_nn(h, win_ref[:, 2 * dk + dv:2 * dk + 2 * dv])).astype(bf16)
        qk = _nn(h, win_ref[:, :2 * dk])
        q = qk[:, :dk] * (hk ** -0.5)
        for hh in range(heads):
            qk_ref[hh, rs, :] = q[:, hh * hk:(hh + 1) * hk].astype(bf16)
            qk_ref[heads + hh, rs, :] = qk[:, dk + hh * hk:dk + (hh + 1) * hk].astype(bf16)
        vt_ref[:, rs] = _nt(wvt_ref[...], h).astype(bf16)


def _gla_proj(x_src, t_all, mod, norm_w, w_in16, layer, geo, cast_jobs=()):
    d = w_in16.shape[1]
    dk, dv, heads = d // 2, d, GLA_HEADS
    hk = dk // heads
    tm = ROW_TILE
    x_args, x_specs = _x_specs(x_src, geo[0])
    c_args, c_in_specs, c_out_specs, c_out_shape = _cast_specs(cast_jobs, t_all // tm)
    kern = functools.partial(_gla_proj_kernel, d=d, dk=dk, dv=dv, heads=heads, n_x=len(x_args),
                             n_lat_tiles=geo[0], n_cast=len(cast_jobs))
    outs = pl.pallas_call(
        kern,
        grid=(t_all // tm,),
        in_specs=x_specs + [
            pl.BlockSpec((1, 1, 6 * d), _group_map(*geo)),
            _const_spec((1, d)),
            pl.BlockSpec((None,) + w_in16.shape[1:], lambda i: (layer, 0, 0), pipeline_mode=pl.Buffered(1)),
        ] + c_in_specs,
        out_specs=[
            pl.BlockSpec((2 * heads, tm, hk), lambda i: (0, i, 0)),
            pl.BlockSpec((tm, dv), lambda i: (i, 0)),
            pl.BlockSpec((dv, tm), lambda i: (0, i)),
            pl.BlockSpec((tm, 2 * GLA_GATE_RANK), lambda i: (i, 0)),
        ] + c_out_specs,
        out_shape=[
            jax.ShapeDtypeStruct((2 * heads, t_all, hk), bf16),
            jax.ShapeDtypeStruct((t_all, dv), bf16),
            jax.ShapeDtypeStruct((dv, t_all), bf16),
            jax.ShapeDtypeStruct((t_all, 2 * GLA_GATE_RANK), bf16),
        ] + c_out_shape,
        scratch_shapes=[pltpu.VMEM((dv, d), bf16)],
        compiler_params=pltpu.CompilerParams(dimension_semantics=("arbitrary",),
                                             vmem_limit_bytes=VMEM_LIMIT),
        name="gla_proj",
    )(*x_args, mod, norm_w.reshape(1, d), w_in16, *c_args)
    return outs[:4], list(outs[4:])


def _gla_block(q, k, vt, z2, s_ref, o_ref, reverse):
    tb, hk = q.shape
    c = GLA_CHUNK
    half = tb // 2
    assert tb == 4 * c
    row = lax.broadcasted_iota(jnp.int32, (tb, tb), 0)
    col = lax.broadcasted_iota(jnp.int32, (tb, tb), 1)
    causal = ((row // c) == (col // c)) & ((col >= row) if reverse else (col <= row))
    cum = jnp.where(causal, 1.0, 0.0).astype(bf16)
    g = (jnp.minimum(z2, 0.0) - jnp.log2(1.0 + jnp.exp2(-jnp.abs(z2)))) * (1.0 / GLA_GATE_NORM)
    g_hi = g.astype(bf16)
    g_lo = (g - g_hi.astype(f32)).astype(bf16)
    b2 = _nn(cum, jnp.concatenate([g_hi, g_lo], axis=1))
    yield
    b = b2[:, :hk] + b2[:, hk:]
    order = (3, 2, 1, 0) if reverse else (0, 1, 2, 3)
    pos = {ch: s for s, ch in enumerate(order)}
    tot = []
    for ch in range(4):
        last = ch * c if reverse else ch * c + c - 1
        tot.append(b[last:last + 1, :])
    zero = jnp.zeros_like(tot[0])
    before = [sum((tot[order[s2]] for s2 in range(pos[ch])), zero) for ch in range(4)]
    after = [sum((tot[order[s2]] for s2 in range(pos[ch] + 1, 4)), zero) for ch in range(4)]

    def rows(vals):
        return jnp.concatenate([jnp.broadcast_to(v, (c, hk)) for v in vals], axis=0)

    def row_scales(vals):
        return rows([jnp.exp2(v) for v in vals])

    kf = k.astype(f32)
    qd = q.astype(f32) * jnp.exp2(b)
    ki = kf * jnp.exp2(-b)
    ke = kf * jnp.exp2(rows(tot) - b)
    qd16, ki16, ke16 = qd.astype(bf16), ki.astype(bf16), ke.astype(bf16)
    yield

    lrow = lax.broadcasted_iota(jnp.int32, (half, half), 0)
    lcol = lax.broadcasted_iota(jnp.int32, (half, half), 1)
    m_same = ((lrow // c) == (lcol // c)) & ((lcol >= lrow) if reverse else (lcol <= lrow))
    m_next = ((lrow < c) & (lcol >= c)) if reverse else ((lrow >= c) & (lcol < c))
    xs = [_nt(qd16[lo:lo + half], jnp.concatenate([ki16[lo:lo + half], ke16[lo:lo + half]], axis=0))
          for lo in (0, half)]

    late, early = (0, half) if reverse else (half, 0)
    q_far = order[3] * c - late
    k_far = order[0] * c - early
    q_scale = [zero, zero]
    q_scale[q_far // c] = tot[order[2]]
    k_scale = [zero, zero]
    k_scale[k_far // c] = tot[order[1]]
    q2 = (qd[late:late + half] * row_scales(q_scale)).astype(bf16)
    k2 = (ke[early:early + half] * row_scales(k_scale)).astype(bf16)
    cross_f32 = _nt(q2, k2)
    s = s_ref[...]
    inter = _nt((qd * row_scales(before)).astype(bf16), s.astype(bf16))
    k_end = (ke * row_scales(after)).astype(bf16)
    upd = _nn(vt, k_end)
    yield
    cross = cross_f32.astype(bf16)

    diag = [jnp.where(m_same, x[:, :half], jnp.where(m_next, x[:, half:], 0.0)).astype(bf16) for x in xs]
    if reverse:
        p_early = diag[1]
        p_late = jnp.concatenate([diag[0], cross], axis=1)
    else:
        p_early = diag[0]
        p_late = jnp.concatenate([cross, diag[1]], axis=1)
    pv_early = _nt(p_early, vt[:, early:early + half])
    pv_late = _nt(p_late, vt)
    yield
    o_ref[early:early + half, :] = (pv_early + inter[early:early + half]).astype(o_ref.dtype)
    o_ref[late:late + half, :] = (pv_late + inter[late:late + half]).astype(o_ref.dtype)
    s_ref[...] = s * jnp.exp2(before[order[3]] + tot[order[3]]) + upd


def _gla_scan_kernel(qkf_ref, vtf_ref, lowf_ref, qkb_ref, vtb_ref, lowb_ref, wup_ref, bup_ref,
                     of_ref, ob_ref, s_ref, *, heads):
    @pl.when(pl.program_id(1) == 0)
    def _():
        s_ref[...] = jnp.zeros_like(s_ref)

    hv = vtf_ref.shape[0] // heads
    hk = qkf_ref.shape[2]
    dk = heads * hk

    def gate_preact(low_ref, direction):
        cols = slice(direction * dk, (direction + 1) * dk)
        return _nn(low_ref[...], wup_ref[:, cols]) + bup_ref[:, cols]

    g_f = gate_preact(lowf_ref, 0)
    g_b = gate_preact(lowb_ref, 1)
    chains = []
    for hh in range(heads):
        cols = slice(hh * hv, (hh + 1) * hv)
        gcols = slice(hh * hk, (hh + 1) * hk)
        chains.append(_gla_block(qkf_ref[hh], qkf_ref[heads + hh], vtf_ref[cols, :], g_f[:, gcols],
                                 s_ref.at[0, hh], of_ref.at[:, cols], False))
        chains.append(_gla_block(qkb_ref[hh], qkb_ref[heads + hh], vtb_ref[cols, :], g_b[:, gcols],
                                 s_ref.at[1, hh], ob_ref.at[:, cols], True))
    groups = [chains[k:k + SCAN_SKEW_GROUP] for k in range(0, len(chains), SCAN_SKEW_GROUP)]
    tick = 0
    while any(groups):
        for k, group in enumerate(groups):
            if tick < k * SCAN_SKEW_STAGES:
                continue
            alive = []
            for chain in group:
                try:
                    next(chain)
                    alive.append(chain)
                except StopIteration:
                    pass
            groups[k] = alive
        tick += 1


def _gla_scan(qk, vt, low, w_gate_up, b_gate_up, n_batch, seq, ctx_len):
    heads = GLA_HEADS
    _, t_all, hk = qk.shape
    dk = heads * hk
    dv = vt.shape[0]
    hv = dv // heads
    rank = GLA_GATE_RANK
    tb = GLA_BLOCK
    assert ctx_len == tb and seq % tb == 0
    nlat = seq // tb
    ctx0 = n_batch * nlat
    wup = jnp.zeros((2 * rank, 2 * dk), f32)
    wup = (wup.at[:rank, :dk].set(w_gate_up[0]).at[rank:, dk:].set(w_gate_up[1]) * LOG2E).astype(bf16)
    bup = b_gate_up.reshape(1, 2 * dk) * LOG2E

    def fwd_row(b, t):
        return jnp.where(t == 0, ctx0 + b, b * nlat + t - 1)

    def bwd_row(b, t):
        return jnp.where(t == 0, ctx0 + b, b * nlat + nlat - t)

    def specs(row_fn):
        return [
            pl.BlockSpec((2 * heads, tb, hk), lambda b, t: (0, row_fn(b, t), 0)),
            pl.BlockSpec((dv, tb), lambda b, t: (0, row_fn(b, t))),
            pl.BlockSpec((tb, 2 * rank), lambda b, t: (row_fn(b, t), 0)),
        ]

    return pl.pallas_call(
        functools.partial(_gla_scan_kernel, heads=heads),
        grid=(n_batch, nlat + 1),
        in_specs=specs(fwd_row) + specs(bwd_row) + [_const_spec((2 * rank, 2 * dk)), _const_spec((1, 2 * dk))],
        out_specs=[
            pl.BlockSpec((tb, dv), lambda b, t: (fwd_row(b, t), 0)),
            pl.BlockSpec((tb, dv), lambda b, t: (bwd_row(b, t), 0)),
        ],
        out_shape=[jax.ShapeDtypeStruct((t_all, dv), bf16)] * 2,
        scratch_shapes=[pltpu.VMEM((2, heads, hv, hk), f32)],
        compiler_params=pltpu.CompilerParams(dimension_semantics=("arbitrary", "arbitrary"),
                                             vmem_limit_bytes=VMEM_LIMIT),
        name="gla_scan",
    )(qk, vt, low, qk, vt, low, wup, bup)


def _swa_proj_kernel(x_ref, mod_ref, nw_ref, w_ref, b_ref, cos_ref, sin_ref,
                     q_ref, kx_ref, vx_ref, *, d, nq):
    sh, sc, _ = _mod_slices(mod_ref, d, 0)
    subs = _sub_tiles(x_ref.shape[0])
    hs = [(_rms(x_ref[rs, :], nw_ref[...]) * (1.0 + sc) + sh).astype(bf16) for rs in subs]
    half = SWA_HD // 2
    for rs, h in zip(subs, hs):
        qkv = _nn(h, w_ref[...]) + b_ref[...]
        cos = cos_ref[rs, :]
        sin = sin_ref[rs, :]
        lane = lax.broadcasted_iota(jnp.int32, cos.shape, 1)
        first_half = (lane % SWA_HD) < half
        low_head = lane < SWA_HD

        def rope(v):
            partner = jnp.where(first_half, pltpu.roll(v, LANES - half, 1), pltpu.roll(v, half, 1))
            return v * cos + partner * sin

        for j in range(nq // LANES):
            q_ref[rs, j * LANES:(j + 1) * LANES] = (
                rope(qkv[:, j * LANES:(j + 1) * LANES]) * (SWA_HD ** -0.5 * LOG2E)).astype(bf16)
        k = rope(qkv[:, nq:nq + LANES])
        v = qkv[:, nq + LANES:nq + 2 * LANES]
        for src_val, dst in ((k, kx_ref), (v, vx_ref)):
            swapped = pltpu.roll(src_val, SWA_HD, 1)
            dst[0, rs, :] = jnp.where(low_head, src_val, 0.0).astype(bf16)
            dst[1, rs, :] = jnp.where(low_head, 0.0, swapped).astype(bf16)
            dst[2, rs, :] = jnp.where(low_head, swapped, 0.0).astype(bf16)
            dst[3, rs, :] = jnp.where(low_head, 0.0, src_val).astype(bf16)


def _rope_tables(seq, extra_rows):
    rows = seq // GRID_W
    row = np.repeat(np.arange(rows, dtype=np.float64), GRID_W)
    col = np.tile(np.arange(GRID_W, dtype=np.float64), rows)
    n_freq = SWA_HD // 4
    inv = ROPE_THETA ** (-np.arange(n_freq, dtype=np.float64) / n_freq)
    ang = np.concatenate([row[:, None] * inv, col[:, None] * inv], axis=-1)
    cos, sin = np.cos(ang), np.sin(ang)
    reps = LANES // SWA_HD
    cos_l = np.tile(np.concatenate([cos, cos], axis=-1), (1, reps))
    sin_l = np.tile(np.concatenate([-sin, sin], axis=-1), (1, reps))
    cos_l = np.concatenate([cos_l, np.ones((extra_rows, LANES))], axis=0)
    sin_l = np.concatenate([sin_l, np.zeros((extra_rows, LANES))], axis=0)
    return jnp.asarray(cos_l, f32), jnp.asarray(sin_l, f32)


def _swa_proj(x_all, mod, norm_w, w_qkv16, b_qkv, cos_l, sin_l, geo):
    t_all, d = x_all.shape
    nq = SWA_HEADS * SWA_HD
    nqkv = w_qkv16.shape[1]
    tm = ROW_TILE
    n_lat_tiles, tiles_per_batch, _ = geo

    def pos_map(i):
        return (jnp.where(i < n_lat_tiles, i % tiles_per_batch, tiles_per_batch), 0)

    return pl.pallas_call(
        functools.partial(_swa_proj_kernel, d=d, nq=nq),
        grid=(t_all // tm,),
        in_specs=[
            pl.BlockSpec((tm, d), lambda i: (i, 0)),
            pl.BlockSpec((1, 1, 6 * d), _group_map(*geo)),
            _const_spec((1, d)),
            _const_spec(w_qkv16.shape),
            _const_spec((1, nqkv)),
            pl.BlockSpec((tm, LANES), pos_map),
            pl.BlockSpec((tm, LANES), pos_map),
        ],
        out_specs=[
            pl.BlockSpec((tm, nq), lambda i: (i, 0)),
            pl.BlockSpec((4, tm, LANES), lambda i: (0, i, 0)),
            pl.BlockSpec((4, tm, LANES), lambda i: (0, i, 0)),
        ],
        out_shape=[
            jax.ShapeDtypeStruct((t_all, nq), bf16),
            jax.ShapeDtypeStruct((4, t_all, LANES), bf16),
            jax.ShapeDtypeStruct((4, t_all, LANES), bf16),
        ],
        compiler_params=pltpu.CompilerParams(dimension_semantics=("arbitrary",),
                                             vmem_limit_bytes=VMEM_LIMIT),
        name="swa_proj",
    )(x_all, mod, norm_w.reshape(1, d), w_qkv16, b_qkv.reshape(1, nqkv), cos_l, sin_l)


def _attend(q_ref, o_ref, sink_ref, k_pieces, v_pieces, edge_masks):
    tq = q_ref.shape[0]
    pairs = q_ref.shape[1] // LANES
    group = pairs // SWA_KV_HEADS
    stack = ATTN_STACK
    lane = lax.broadcasted_iota(jnp.int32, (tq, LANES), 1)
    low_head = lane < SWA_HD
    k_cat, v_cat = [], []
    for g in range(SWA_KV_HEADS):
        k_cat.append(jnp.concatenate(k_pieces[g][0] + k_pieces[g][1], axis=0))
        nk = k_cat[g].shape[0] // 2
        lane2 = lax.broadcasted_iota(jnp.int32, (nk, LANES), 1)
        ones_lo = jnp.where(lane2 < SWA_HD, 1.0, 0.0).astype(bf16)
        ones_hi = jnp.where(lane2 < SWA_HD, 0.0, 1.0).astype(bf16)
        v_cat.append(jnp.concatenate([
            jnp.concatenate([jnp.concatenate(v_pieces[g][0], axis=0), ones_lo], axis=1),
            jnp.concatenate([jnp.concatenate(v_pieces[g][1], axis=0), ones_hi], axis=1)], axis=0))
    units = [(u * stack) // group for u in range(pairs // stack)]

    def scores(u):
        q_stack = jnp.concatenate(
            [q_ref[:, (u * stack + jj) * LANES:(u * stack + jj + 1) * LANES] for jj in range(stack)], axis=0)
        return _nt(q_stack, k_cat[units[u]])

    ahead = [scores(u) for u in range(min(ATTN_AHEAD, len(units)))]
    yield
    for u, g in enumerate(units):
        s_all = ahead.pop(0)
        if u + ATTN_AHEAD < len(units):
            ahead.append(scores(u + ATTN_AHEAD))
        prob_rows, sink_rows = [], []
        for jj in range(stack):
            j = u * stack + jj
            probs, sink_terms = [], []
            for p in range(2):
                sp = s_all[jj * tq:(jj + 1) * tq, p * nk:(p + 1) * nk]
                if edge_masks is not None:
                    before, after = edge_masks
                    blk = before.shape[1]
                    sp = jnp.concatenate([jnp.where(before, sp[:, :blk], NEG_INF), sp[:, blk:2 * blk],
                                          jnp.where(after, sp[:, 2 * blk:3 * blk], NEG_INF), sp[:, 3 * blk:]],
                                         axis=1)
                sink = sink_ref[2 * j + p] * LOG2E
                m = jnp.maximum(jnp.max(sp, axis=-1, keepdims=True), sink)
                probs.append(jnp.exp2(sp - m).astype(bf16))
                sink_terms.append(jnp.exp2(sink - m))
            prob_rows.append(jnp.concatenate(probs, axis=1))
            sink_rows.append(jnp.where(low_head, sink_terms[0], sink_terms[1]))
        out_all = _nn(jnp.concatenate(prob_rows, axis=0), v_cat[g])
        for jj in range(stack):
            j = u * stack + jj
            out = out_all[jj * tq:(jj + 1) * tq]
            den = out[:, LANES:] + sink_rows[jj]
            o_ref[:, j * LANES:(j + 1) * LANES] = (out[:, :LANES] / den).astype(bf16)
        yield


def _trace_in_order(chains):
    chains = list(chains)
    stages = [list(range(SWA_HEADS // 2 // ATTN_STACK + 1)) for _ in chains]
    for c, chain in enumerate(chains):
        for s in stages[c]:
            if not (c > 0 and s == 0):
                next(chain)
            if c + 1 < len(chains) and s == len(stages[c]) - 1 - ATTN_OVERLAP:
                next(chains[c + 1])


def _swa_kernel(sink_ref, q_ref, kp_ref, ko_ref, kn_ref, kc_ref, vp_ref, vo_ref, vn_ref, vc_ref,
                o_ref, *, seq, n_lat_steps, need_ctx):
    i = pl.program_id(1)
    blk = SWA_BLOCK

    def pieces(refs_of_half):
        return [[[load(2 * g + p) for load in refs_of_half] for p in range(2)] for g in range(SWA_KV_HEADS)]

    def rows(ref, lo):
        return lambda idx: ref[idx, lo:lo + blk, :]

    @pl.when(i < n_lat_steps)
    def _():
        halves = []
        for half in range(2):
            first = 2 * i + half
            qi = first * blk + lax.broadcasted_iota(jnp.int32, (blk, blk), 0)
            local = lax.broadcasted_iota(jnp.int32, (blk, blk), 1)
            kj_before = (first - 1) * blk + local
            kj_after = (first + 1) * blk + local
            masks = ((qi - kj_before <= SWA_WINDOW) & (kj_before >= 0),
                     (kj_after - qi <= SWA_WINDOW) & (kj_after < seq))
            if half == 0:
                k_loads = (rows(kp_ref, 0), rows(ko_ref, 0), rows(ko_ref, blk), lambda idx: kc_ref[idx])
                v_loads = (rows(vp_ref, 0), rows(vo_ref, 0), rows(vo_ref, blk), lambda idx: vc_ref[idx])
            else:
                k_loads = (rows(ko_ref, 0), rows(ko_ref, blk), rows(kn_ref, 0), lambda idx: kc_ref[idx])
                v_loads = (rows(vo_ref, 0), rows(vo_ref, blk), rows(vn_ref, 0), lambda idx: vc_ref[idx])
            rs = slice(half * blk, (half + 1) * blk)
            halves.append(_attend(q_ref.at[rs, :], o_ref.at[rs, :], sink_ref, pieces(k_loads), pieces(v_loads),
                                  masks))
        _trace_in_order(halves)

    if need_ctx:
        @pl.when(i >= n_lat_steps)
        def _():
            halves = []
            for half in range(2):
                rs = slice(half * blk, (half + 1) * blk)
                halves.append(_attend(q_ref.at[rs, :], o_ref.at[rs, :], sink_ref, pieces((lambda idx: kc_ref[idx],)),
                                      pieces((lambda idx: vc_ref[idx],)), None))
            _trace_in_order(halves)


def _swa_attention(q, kx, vx, sinks, n_batch, seq, ctx_len, need_ctx):
    t_all, nq = q.shape
    blk = SWA_BLOCK
    step = 2 * blk
    assert seq % step == 0 and ctx_len % step == 0
    nb = seq // blk
    n_lat_steps = seq // step
    n_ctx_steps = ctx_len // step
    out_rows = t_all if need_ctx else n_batch * seq
    smem = pl.BlockSpec(memory_space=pltpu.SMEM)

    def edge(off):
        return pl.BlockSpec((4, blk, LANES),
                            lambda b, i: (0, b * nb + jnp.clip(2 * i + off, 0, nb - 1), 0))

    def own():
        return pl.BlockSpec((4, step, LANES),
                            lambda b, i: (0, b * n_lat_steps + jnp.minimum(i, n_lat_steps - 1), 0))

    def ctx_spec():
        return pl.BlockSpec((4, ctx_len, LANES), lambda b, i: (0, (n_batch * seq) // ctx_len + b, 0))

    def q_row(b, i):
        return (jnp.where(i < n_lat_steps, b * n_lat_steps + i,
                          n_batch * n_lat_steps + b * n_ctx_steps + i - n_lat_steps), 0)

    return pl.pallas_call(
        functools.partial(_swa_kernel, seq=seq, n_lat_steps=n_lat_steps, need_ctx=need_ctx),
        grid=(n_batch, n_lat_steps + n_ctx_steps if need_ctx else n_lat_steps),
        in_specs=[smem, pl.BlockSpec((step, nq), q_row),
                  edge(-1), own(), edge(2), ctx_spec(), edge(-1), own(), edge(2), ctx_spec()],
        out_specs=pl.BlockSpec((step, nq), q_row),
        out_shape=jax.ShapeDtypeStruct((out_rows, nq), bf16),
        compiler_params=pltpu.CompilerParams(dimension_semantics=("arbitrary", "arbitrary"),
                                             vmem_limit_bytes=VMEM_LIMIT),
        name="swa_attn",
    )(sinks, q, kx, kx, kx, kx, vx, vx, vx, vx)


def _post_kernel(*refs, d, gla, final, hidden, n_x, n_lat_tiles, n_cast):
    x_refs, refs = refs[:n_x], refs[n_x:]
    if gla:
        of_ref, ob_ref, r_ref, on_ref = refs[:4]
        rest = refs[4:]
    else:
        a_ref = refs[0]
        rest = refs[1:]
    mod_ref, wo_ref, nw_ref, win_ref, wout_ref = rest[:5]
    rest = rest[5:]
    if final:
        nf_ref, rest = rest[0], rest[1:]
    cast_in, out_ref, cast_out = rest[:n_cast], rest[n_cast], rest[n_cast + 1:]
    _run_casts(cast_in, cast_out)

    _, _, g1 = _mod_slices(mod_ref, d, 0)
    sh2, sc2, g2 = _mod_slices(mod_ref, d, 3)

    def mixer_residual(rs):
        if gla:
            o = of_ref[rs, :].astype(f32) + ob_ref[rs, :].astype(f32)
            hv = d // GLA_HEADS
            parts = []
            for hh in range(GLA_HEADS):
                seg = o[:, hh * hv:(hh + 1) * hv]
                parts.append(_rms(seg, on_ref[:, hh * hv:(hh + 1) * hv]))
            a = (jnp.concatenate(parts, axis=1) * r_ref[rs, :].astype(f32)).astype(bf16)
        else:
            a = a_ref[rs, :]
        x1 = _load_x(x_refs, n_lat_tiles, rs) + g1 * _nn(a, wo_ref[...])
        return x1, (_rms(x1, nw_ref[...]) * (1.0 + sc2) + sh2).astype(bf16)

    subs = _sub_tiles(out_ref.shape[0])
    pre = [mixer_residual(rs) for rs in subs]
    for rs, (x1, hn) in zip(subs, pre):
        acts = []
        for off in range(0, hidden, FFN_CHUNK):
            gate = _nn(hn, win_ref[:, off:off + FFN_CHUNK])
            up = _nn(hn, win_ref[:, hidden + off:hidden + off + FFN_CHUNK])
            acts.append((_silu(gate) * up).astype(bf16))
        x2 = x1 + g2 * _nn(jnp.concatenate(acts, axis=1), wout_ref[...])
        if final:
            x2 = _rms(x2, nf_ref[...])
        out_ref[rs, :] = x2


def _post(x_src, mixer_in, mod, w_o16, norm_w, w_in16, w_out16, geo, gla, norm_final, n_tiles, o_norm=None,
          cast_jobs=()):
    hidden, d = w_out16.shape
    assert hidden % FFN_CHUNK == 0
    tm = ROW_TILE
    final = norm_final is not None
    row = lambda i: (i, 0)
    args, in_specs = _x_specs(x_src, geo[0])
    n_x = len(args)
    if gla:
        o_f, o_b, r = mixer_in
        args += [o_f, o_b, r, jnp.tile(o_norm, GLA_HEADS).reshape(1, d)]
        in_specs += [pl.BlockSpec((tm, d), row)] * 3 + [_const_spec((1, d))]
    else:
        args += [mixer_in]
        in_specs += [pl.BlockSpec((tm, d), row)]
    args += [mod, w_o16, norm_w.reshape(1, d), w_in16, w_out16]
    in_specs += [pl.BlockSpec((1, 1, 6 * d), _group_map(*geo)), _const_spec(w_o16.shape), _const_spec((1, d)),
                 _const_spec(w_in16.shape), _const_spec(w_out16.shape)]
    if final:
        args += [norm_final.reshape(1, d)]
        in_specs += [_const_spec((1, d))]
    c_args, c_in_specs, c_out_specs, c_out_shape = _cast_specs(cast_jobs, n_tiles)
    args += c_args
    in_specs += c_in_specs
    out_specs = [pl.BlockSpec((tm, d), row)] + c_out_specs
    out_shape = [jax.ShapeDtypeStruct((n_tiles * tm, d), f32)] + c_out_shape
    outs = pl.pallas_call(
        functools.partial(_post_kernel, d=d, gla=gla, final=final, hidden=hidden, n_x=n_x,
                          n_lat_tiles=geo[0], n_cast=len(cast_jobs)),
        grid=(n_tiles,),
        in_specs=in_specs,
        out_specs=out_specs,
        out_shape=out_shape,
        compiler_params=pltpu.CompilerParams(dimension_semantics=("arbitrary",),
                                             vmem_limit_bytes=VMEM_LIMIT),
        name="post_gla" if gla else "post_swa",
    )(*args)
    return outs[0], list(outs[1:])


def kernel(x, c, ctx, c_ctx, w_ada, b_ada, norm_mix, norm_ffn, gla_w_in, gla_w_gate_up, gla_b_gate_up,
           gla_w_o, gla_o_norm, swa_w_qkv, swa_b_qkv, swa_w_o, swa_sinks, ffn_w_in, ffn_w_out,
           norm_final):
    n_batch, seq, d = x.shape
    ctx_len = ctx.shape[1]
    depth = w_ada.shape[0]
    tm = ROW_TILE
    assert seq % tm == 0 and (n_batch * ctx_len) == tm
    n_lat_tiles = n_batch * seq // tm
    geo = (n_lat_tiles, seq // tm, n_batch)

    mod_all = _ada_mod(jnp.concatenate([c, c_ctx[None, :]], axis=0), w_ada, b_ada)
    t_all = n_batch * (seq + ctx_len)
    x_src = (x.reshape(n_batch * seq, d), ctx.reshape(n_batch * ctx_len, d))
    cos_l, sin_l = _rope_tables(seq, tm)

    def mixer_jobs(layer):
        jj = layer // 2
        mix = [(gla_w_o, jj)] if layer % 2 == 0 else [(swa_w_qkv, jj), (swa_w_o, jj)]
        return mix + [(ffn_w_in, layer), (ffn_w_out, layer)]

    gla_w_in16 = gla_w_in.astype(bf16)
    w16 = None
    for i in range(depth):
        last = i == depth - 1
        j = i // 2
        mod = mod_all[i].reshape(8, 1, 6 * d)
        n_tiles = n_lat_tiles if last else n_lat_tiles + 1
        nf = norm_final if last else None
        jobs = () if last else mixer_jobs(i + 1)
        if i % 2 == 0:
            (qk, r, vt, low), first_w16 = _gla_proj(x_src, t_all, mod, norm_mix[i], gla_w_in16, j, geo,
                                                    cast_jobs=mixer_jobs(i) if w16 is None else ())
            w16 = first_w16 if w16 is None else w16
        w_o16, w_ffn_in16, w_ffn_out16 = w16[-3:]
        if i % 2 == 0:
            o_f, o_b = _gla_scan(qk, vt, low, gla_w_gate_up[j], gla_b_gate_up[j], n_batch, seq, ctx_len)
            x_src, w16 = _post(x_src, (o_f, o_b, r), mod, w_o16, norm_ffn[i], w_ffn_in16, w_ffn_out16,
                               geo, True, nf, n_tiles, o_norm=gla_o_norm[j], cast_jobs=jobs)
        else:
            q, kx, vx = _swa_proj(x_src, mod, norm_mix[i], w16[0], swa_b_qkv[j], cos_l, sin_l, geo)
            attn = _swa_attention(q, kx, vx, swa_sinks[j], n_batch, seq, ctx_len, not last)
            x_src, w16 = _post(x_src, attn, mod, w_o16, norm_ffn[i], w_ffn_in16, w_ffn_out16,
                               geo, False, nf, n_tiles, cast_jobs=jobs)
    return x_src.reshape(n_batch, seq, d)
```
